```python
import math
import jax
import jax.numpy as jnp
from jax import lax
import numpy as np

D_MODEL = 1024
BATCH = 16
SEQ = 256
DEPTH = 4
DEC_BATCH = 4
DEC_SEQ = 4096
PAST_LEN = 512

GRID_W = 64
RMS_EPS = 1e-6
S5_WIDTH = D_MODEL // 2
S5_GROUP = 16
S5_GROUPS = S5_WIDTH // S5_GROUP
S5_STATE = 64
HG_HEADS = 4
HG_DK = 128
HG_DV = (D_MODEL // 2) // HG_HEADS
HG_WIDTH = HG_HEADS * HG_DV
HG_KW = HG_HEADS * HG_DK
HG_CHUNK = 64
N_EXPERTS = 16
N_GROUPS = 4
EXPERTS_PER_GROUP = N_EXPERTS // N_GROUPS
TOP_K = 2
D_FF = D_MODEL
MOE_BLOCK = 128
IN_COLS = S5_WIDTH + 3 * HG_KW + 2 * HG_WIDTH + 2 * D_MODEL
IN_SPLITS = (S5_WIDTH, S5_WIDTH + HG_KW, S5_WIDTH + 2 * HG_KW, S5_WIDTH + 3 * HG_KW,
             S5_WIDTH + 3 * HG_KW + HG_WIDTH, S5_WIDTH + 3 * HG_KW + 2 * HG_WIDTH,
             S5_WIDTH + 3 * HG_KW + 2 * HG_WIDTH + D_MODEL)

kernel_name = 'hybrid_s5_hgrn2_moe_diffusion_step'


def rmsnorm(x, w):
    xf = x.astype(jnp.float32)
    y = xf * lax.rsqrt(jnp.mean(xf * xf, axis=-1, keepdims=True) + RMS_EPS)
    return y.astype(x.dtype) * w


def adaln(cvec, w_ada, b_ada):
    m = jax.nn.silu(cvec) @ w_ada + b_ada
    return jnp.split(m[:, None, :], 6, axis=-1)


def grid_pos_embed(rows, dtype):
    f32 = jnp.float32
    r = jnp.repeat(jnp.arange(rows, dtype=f32), GRID_W)
    col = jnp.tile(jnp.arange(GRID_W, dtype=f32), rows)
    quarter = D_MODEL // 4
    omega = 1.0 / (10000.0 ** (jnp.arange(quarter, dtype=f32) / quarter))

    def axis_embed(p):
        ang = p[:, None] * omega[None, :]
        return jnp.concatenate([jnp.sin(ang), jnp.cos(ang)], axis=-1)

    return jnp.concatenate([axis_embed(r), axis_embed(col)], axis=-1).astype(dtype)


def s5_scan(bu, lam_bar):
    a = jnp.broadcast_to(lam_bar, bu.shape)

    def combine(e1, e2):
        a1, b1 = e1
        a2, b2 = e2
        return a1 * a2, a2 * b1 + b2

    return lax.associative_scan(combine, (a, bu), axis=1)[1]


def s5_mixer(u, h0, lam_re, lam_im, log_step, b_re, b_im, c_re, c_im, s5_d, w_glu, b_glu):
    f32 = jnp.float32
    bsz, L, _ = u.shape
    uf = u.astype(f32).reshape(bsz, L, S5_GROUPS, S5_GROUP)
    uc = uf.astype(jnp.complex64)
    b_c = lax.complex(b_re.astype(f32), b_im.astype(f32))
    c_c = lax.complex(c_re.astype(f32), c_im.astype(f32))
    total = None
    finals = []
    for d in range(2):
        lam = lax.complex(jnp.minimum(lam_re[d].astype(f32), -1e-4), lam_im[d].astype(f32))
        step = jnp.exp(log_step[d].astype(f32))[:, None]
        lam_bar = jnp.exp(lam * step)
        b_bar = ((lam_bar - 1.0) / lam)[:, :, None] * b_c
        ud = uc if d == 0 else uc[:, ::-1]
        bu = jnp.einsum('blgc,gpc->blgp', ud, b_bar)
        if h0 is not None:
            bu = bu.at[:, 0].add(lam_bar * h0[:, d])
        st = s5_scan(bu, lam_bar)
        finals.append(st[:, -1])
        st = st if d == 0 else st[:, ::-1]
        total = st if total is None else total + st
    y = jnp.einsum('gcp,blgp->blgc', c_c, total).real + s5_d.astype(f32) * uf
    g = jax.nn.gelu(y.reshape(bsz, L, S5_WIDTH))
    out = g * jax.nn.sigmoid(g @ w_glu.astype(f32) + b_glu.astype(f32))
    return out.astype(u.dtype), jnp.stack(finals, axis=1)


def gla_chunked(q, k, v, logf, h0):
    bsz, L, H, _ = q.shape
    dv = v.shape[-1]
    n = L // HG_CHUNK

    def chunks(t):
        return t.reshape(bsz, n, HG_CHUNK, H, t.shape[-1]).transpose(1, 0, 3, 2, 4)

    tri = jnp.tril(jnp.ones((HG_CHUNK, HG_CHUNK), dtype=bool))[:, :, None]

    def step(S, blk):
        qb, kb, vb, gb = blk
        bcum = jnp.cumsum(gb, axis=2)
        rel = bcum[:, :, :, None, :] - bcum[:, :, None, :, :]
        decay = jnp.exp(jnp.where(tri, rel, -jnp.inf))
        att = jnp.einsum('bhtd,bhsd,bhtsd->bhts', qb, kb, decay)
        o = jnp.einsum('bhts,bhsv->bhtv', att, vb) + jnp.einsum('bhtd,bhdv->bhtv', qb * jnp.exp(bcum), S)
        btot = bcum[:, :, -1:]
        S = S * jnp.exp(btot[:, :, 0])[..., None] + jnp.einsum('bhsd,bhsv->bhdv', kb * jnp.exp(btot - bcum), vb)
        return S, o

    S, o = lax.scan(step, h0, (chunks(q), chunks(k), chunks(v), chunks(logf)))
    return o.transpose(1, 0, 3, 2, 4).reshape(bsz, L, H, dv), S


def hgrn2_mixer(q_in, f_fwd, f_bwd, i_in, g_out, h0, lb, norm_w):
    f32 = jnp.float32
    bsz, L, _ = q_in.shape
    q = jax.nn.silu(q_in.astype(f32).reshape(bsz, L, HG_HEADS, HG_DK)) * (HG_DK ** -0.5)
    v = i_in.astype(f32).reshape(bsz, L, HG_HEADS, HG_DV)
    total = None
    finals = []
    for d, f_raw in enumerate((f_fwd, f_bwd)):
        f = lb[d] + (1.0 - lb[d]) * jax.nn.sigmoid(f_raw.astype(f32))
        f = f.reshape(bsz, L, HG_HEADS, HG_DK)
        k, logf = 1.0 - f, jnp.log(f)
        args = (q, k, v, logf) if d == 0 else (q[:, ::-1], k[:, ::-1], v[:, ::-1], logf[:, ::-1])
        s0 = jnp.zeros((bsz, HG_HEADS, HG_DK, HG_DV), f32) if h0 is None else h0[:, d]
        o, S = gla_chunked(*args, s0)
        finals.append(S)
        o = o if d == 0 else o[:, ::-1]
        total = o if total is None else total + o
    o = total * lax.rsqrt(jnp.mean(total * total, axis=-1, keepdims=True) + RMS_EPS)
    o = o.reshape(bsz, L, HG_WIDTH) * norm_w.astype(f32) * jax.nn.silu(g_out.astype(f32))
    return o.astype(q_in.dtype), jnp.stack(finals, axis=1)


def moe_ffn(x, router_w, router_b, w_gate, w_up, w_down):
    f32 = jnp.float32
    n_tok, dm = x.shape
    scores = jax.nn.sigmoid(x.astype(f32) @ router_w.astype(f32))
    sel = (scores + router_b.astype(f32)).reshape(n_tok, N_GROUPS, EXPERTS_PER_GROUP)
    group_score = lax.top_k(sel, 2)[0].sum(-1)
    best = jnp.argmax(group_score, axis=-1)
    in_group = jnp.take_along_axis(sel, best[:, None, None], axis=1)[:, 0]
    _, local = lax.top_k(in_group, TOP_K)
    expert = best[:, None] * EXPERTS_PER_GROUP + local
    wts = jnp.take_along_axis(scores, expert, axis=1)
    wts = wts / jnp.sum(wts, axis=-1, keepdims=True)
    n_slots = n_tok * TOP_K
    flat_e = expert.reshape(-1)
    order = jnp.argsort(flat_e)
    e_sorted = flat_e[order]
    counts = jnp.zeros((N_EXPERTS,), jnp.int32).at[flat_e].add(1)
    padded = (counts + MOE_BLOCK - 1) // MOE_BLOCK * MOE_BLOCK
    pad_end = jnp.cumsum(padded)
    pad_start = pad_end - padded
    raw_start = jnp.cumsum(counts) - counts
    dest = pad_start[e_sorted] + jnp.arange(n_slots, dtype=jnp.int32) - raw_start[e_sorted]
    n_blocks = -(-n_slots // MOE_BLOCK) + N_EXPERTS
    tok = order // TOP_K
    xb = jnp.zeros((n_blocks * MOE_BLOCK, dm), x.dtype).at[dest].set(x[tok])
    blk_expert = jnp.minimum(
        jnp.searchsorted(pad_end, jnp.arange(n_blocks, dtype=jnp.int32) * MOE_BLOCK, side='right'),
        N_EXPERTS - 1)

    def expert_block(args):
        xe, e = args
        return (jax.nn.silu(xe @ w_gate[e]) * (xe @ w_up[e])) @ w_down[e]

    yb = lax.map(expert_block, (xb.reshape(n_blocks, MOE_BLOCK, dm), blk_expert)).reshape(-1, dm)
    y_slot = yb[dest] * wts.reshape(-1)[order][:, None].astype(x.dtype)
    return jax.ops.segment_sum(y_slot, tok, num_segments=n_tok)


def trunk_layer(x, cvec, s5_h0, hg_h0, lb, w_ada, b_ada, norm1_w, norm2_w, w_in,
                lam_re, lam_im, log_step, b_re, b_im, c_re, c_im, s5_d, w_glu, b_glu,
                hg_norm_w, w_br_s5, w_br_hg, w_out, router_w, router_b, w_gate, w_up, w_down):
    shift1, scale1, gate1, shift2, scale2, gate2 = adaln(cvec, w_ada, b_ada)
    h = rmsnorm(x, norm1_w) * (1.0 + scale1) + shift1
    u, q_in, f_fwd, f_bwd, i_in, g_out, gate_s5, gate_hg = jnp.split(h @ w_in, IN_SPLITS, axis=-1)
    y_s5, s5_fin = s5_mixer(u, s5_h0, lam_re, lam_im, log_step, b_re, b_im, c_re, c_im, s5_d, w_glu, b_glu)
    y_hg, hg_fin = hgrn2_mixer(q_in, f_fwd, f_bwd, i_in, g_out, hg_h0, lb, hg_norm_w)
    merged = jax.nn.sigmoid(gate_s5) * (y_s5 @ w_br_s5) + jax.nn.sigmoid(gate_hg) * (y_hg @ w_br_hg)
    x = x + gate1 * (merged @ w_out)
    h = rmsnorm(x, norm2_w) * (1.0 + scale2) + shift2
    bsz, L, _ = x.shape
    ffn = moe_ffn(h.reshape(bsz * L, D_MODEL), router_w, router_b, w_gate, w_up, w_down)
    x = x + gate2 * ffn.reshape(bsz, L, D_MODEL)
    return x, s5_fin, hg_fin


def setup_inputs(seed: int = 0) -> dict:
    key = jax.random.key(seed)
    ks = iter(jax.random.split(key, 40))
    f32 = jnp.float32

    def nrm(shape, scale):
        return scale * jax.random.normal(next(ks), shape, f32)

    G, P = S5_GROUPS, S5_STATE
    n_idx = jnp.arange(P, dtype=f32)
    return {
        'x_prompt': nrm((BATCH, SEQ, D_MODEL), 1.0),
        'x_sample': nrm((DEC_BATCH, DEC_SEQ, D_MODEL), 1.0),
        'state_s5_re': nrm((DEC_BATCH, DEPTH, 2, G, P), 0.3),
        'state_s5_im': nrm((DEC_BATCH, DEPTH, 2, G, P), 0.3),
        'state_hg': nrm((DEC_BATCH, DEPTH, 2, HG_HEADS, HG_DK, HG_DV), 0.3),
        'c': nrm((DEC_BATCH, D_MODEL), 1.0),
        'c_ctx': nrm((D_MODEL,), 1.0),
        'w_ada': nrm((DEPTH, D_MODEL, 6 * D_MODEL), 0.5 * D_MODEL ** -0.5),
        'b_ada': nrm((DEPTH, 6 * D_MODEL), 0.01),
        'norm1_w': 1.0 + nrm((DEPTH, D_MODEL), 0.01),
        'norm2_w': 1.0 + nrm((DEPTH, D_MODEL), 0.01),
        'w_in': nrm((DEPTH, D_MODEL, IN_COLS), D_MODEL ** -0.5),
        's5_lam_re': -0.5 + nrm((DEPTH, 2, G, P), 0.01),
        's5_lam_im': math.pi * n_idx + nrm((DEPTH, 2, G, P), 0.01),
        's5_log_step': jax.random.uniform(next(ks), (DEPTH, 2, G), f32, math.log(1e-3), math.log(1e-1)),
        's5_b_re': nrm((DEPTH, G, P, S5_GROUP), (2 * S5_GROUP) ** -0.5),
        's5_b_im': nrm((DEPTH, G, P, S5_GROUP), (2 * S5_GROUP) ** -0.5),
        's5_c_re': nrm((DEPTH, G, S5_GROUP, P), (2 * P) ** -0.5),
        's5_c_im': nrm((DEPTH, G, S5_GROUP, P), (2 * P) ** -0.5),
        's5_d': nrm((DEPTH, G, S5_GROUP), 0.5),
        's5_w_glu': nrm((DEPTH, S5_WIDTH, S5_WIDTH), S5_WIDTH ** -0.5),
        's5_b_glu': nrm((DEPTH, S5_WIDTH), 0.01),
        'hg_lb_logits': nrm((DEPTH, 2, HG_KW), 0.1),
        'hg_norm_w': 1.0 + nrm((DEPTH, HG_WIDTH), 0.01),
        'w_br_s5': nrm((DEPTH, S5_WIDTH, D_MODEL), S5_WIDTH ** -0.5),
        'w_br_hg': nrm((DEPTH, HG_WIDTH, D_MODEL), HG_WIDTH ** -0.5),
        'w_out': nrm((DEPTH, D_MODEL, D_MODEL), D_MODEL ** -0.5),
        'router_w': nrm((D_MODEL, N_EXPERTS), D_MODEL ** -0.5),
        'router_b': nrm((N_EXPERTS,), 0.01),
        'moe_w_gate': nrm((DEPTH, N_EXPERTS, D_MODEL, D_FF), D_MODEL ** -0.5),
        'moe_w_up': nrm((DEPTH, N_EXPERTS, D_MODEL, D_FF), D_MODEL ** -0.5),
        'moe_w_down': nrm((DEPTH, N_EXPERTS, D_FF, D_MODEL), D_FF ** -0.5),
        'norm_f_w': 1.0 + nrm((D_MODEL,), 0.01),
    }


def reference(x_prompt, x_sample, state_s5_re, state_s5_im, state_hg, c, c_ctx, w_ada, b_ada,
              norm1_w, norm2_w, w_in, s5_lam_re, s5_lam_im, s5_log_step, s5_b_re, s5_b_im,
              s5_c_re, s5_c_im, s5_d, s5_w_glu, s5_b_glu, hg_lb_logits, hg_norm_w, w_br_s5,
              w_br_hg, w_out, router_w, router_b, moe_w_gate, moe_w_up, moe_w_down, norm_f_w):
    f32 = jnp.float32
    probs = jax.nn.softmax(hg_lb_logits.astype(f32), axis=0)
    lower_bounds = jnp.cumsum(probs, axis=0) - probs[0]
    rows = x_sample.shape[1] // GRID_W
    xs = x_sample + grid_pos_embed(rows, x_sample.dtype)[None]
    xp = x_prompt
    new_s5, new_hg = [], []
    for l in range(DEPTH):
        lw = (lower_bounds[l], w_ada[l], b_ada[l], norm1_w[l], norm2_w[l], w_in[l],
              s5_lam_re[l], s5_lam_im[l], s5_log_step[l], s5_b_re[l], s5_b_im[l], s5_c_re[l], s5_c_im[l],
              s5_d[l], s5_w_glu[l], s5_b_glu[l], hg_norm_w[l], w_br_s5[l], w_br_hg[l], w_out[l],
              router_w, router_b, moe_w_gate[l], moe_w_up[l], moe_w_down[l])
        xp, s5_fin, hg_fin = trunk_layer(xp, c_ctx[None], None, None, *lw)
        new_s5.append(s5_fin)
        new_hg.append(hg_fin)
        s5_h0 = lax.complex(state_s5_re[:, l].astype(f32), state_s5_im[:, l].astype(f32))
        xs, _, _ = trunk_layer(xs, c, s5_h0, state_hg[:, l].astype(f32), *lw)
    y_prompt = rmsnorm(xp, norm_f_w)
    y_sample = rmsnorm(xs, norm_f_w)
    s5_all = jnp.stack(new_s5, axis=1)
    new_state_s5_re = s5_all.real.astype(state_s5_re.dtype)
    new_state_s5_im = s5_all.imag.astype(state_s5_im.dtype)
    new_state_hg = jnp.stack(new_hg, axis=1).astype(state_hg.dtype)
    return (y_prompt, y_sample, new_state_s5_re, new_state_s5_im, new_state_hg)
```

```python
import functools
import math

import jax
import jax.numpy as jnp
import numpy as np
from jax import lax
from jax.experimental import pallas as pl
from jax.experimental.pallas import tpu as pltpu

F32 = jnp.float32
BF16 = jnp.bfloat16

D_MODEL = 1024
DEPTH = 4
GRID_W = 64
RMS_EPS = 1e-6
S5_WIDTH = 512
S5_GROUP = 16
S5_GROUPS = 32
S5_STATE = 64
HG_HEADS = 4
HG_DK = 128
HG_DV = 128
HG_WIDTH = 512
HG_KW = 512
HG_CHUNK = 64
HG_SUB = 16
N_EXPERTS = 16
N_GROUPS = 4
EXPERTS_PER_GROUP = 4
TOP_K = 2
IN_COLS = 5120

LANES = 128
SUBLANES = 8
VMEM_LIMIT_BYTES = 56 * 1024 * 1024

TOKEN_BLOCK = 256
S5_SEG = TOKEN_BLOCK // SUBLANES
S5_COLBLOCKS = S5_WIDTH // LANES
S5_BLOCK_STATES = (LANES // S5_GROUP) * S5_STATE
MOE_ROWS = 256


def _block_tables(n_ctx_seq, ctx_len, n_lat_seq, lat_len):
    seq, first, last, modrow = [], [], [], []
    for s in range(n_ctx_seq):
        nb = ctx_len // TOKEN_BLOCK
        for b in range(nb):
            seq.append(s); first.append(int(b == 0)); last.append(int(b == nb - 1)); modrow.append(0)
    for s in range(n_lat_seq):
        nb = lat_len // TOKEN_BLOCK
        for b in range(nb):
            seq.append(n_ctx_seq + s); first.append(int(b == 0)); last.append(int(b == nb - 1))
            modrow.append(1 + s)
    return (np.asarray(seq, np.int32), np.asarray(first, np.int32),
            np.asarray(last, np.int32), np.asarray(modrow, np.int32))


def _cparams(n_axes):
    return pltpu.CompilerParams(dimension_semantics=("arbitrary",) * n_axes,
                                vmem_limit_bytes=VMEM_LIMIT_BYTES)


def _split_bf16(a):
    hi = a.astype(BF16)
    lo = (a - hi.astype(F32)).astype(BF16)
    return hi, lo


def _dot_nt(a, b):
    return lax.dot_general(a, b, (((1,), (1,)), ((), ())), preferred_element_type=F32)


def _adaln_body(c_ref, w_ref, b_ref, o_ref):
    a = c_ref[...]
    a = a * jax.nn.sigmoid(a)
    a_hi, a_lo = _split_bf16(a)
    w_hi, w_lo = _split_bf16(w_ref[0])
    acc = jnp.dot(a_hi, w_hi, preferred_element_type=F32)
    acc += jnp.dot(a_lo, w_hi, preferred_element_type=F32)
    acc += jnp.dot(a_hi, w_lo, preferred_element_type=F32)
    o_ref[0] = acc + b_ref[0]


def _adaln(cvecs, w_ada, b_ada):
    tn = 1536
    n_tiles = (6 * D_MODEL) // tn
    return pl.pallas_call(
        _adaln_body,
        out_shape=jax.ShapeDtypeStruct((DEPTH, SUBLANES, 6 * D_MODEL), F32),
        grid=(DEPTH, n_tiles),
        in_specs=[
            pl.BlockSpec((SUBLANES, D_MODEL), lambda l, n: (0, 0)),
            pl.BlockSpec((1, D_MODEL, tn), lambda l, n: (l, 0, n)),
            pl.BlockSpec((1, 1, tn), lambda l, n: (l, 0, n)),
        ],
        out_specs=pl.BlockSpec((1, SUBLANES, tn), lambda l, n: (l, 0, n)),
        compiler_params=_cparams(2),
        name="adaln",
    )(cvecs, w_ada, b_ada.reshape(DEPTH, 1, 6 * D_MODEL))


_IN_SPLITS = ((0, 512), (512, 1024), (1024, 2048), (2048, 2560), (2560, 3072), (3072, 4096), (4096, 5120))


def _rms_modulate(x, nw, scale, shift):
    ms = jnp.mean(x * x, axis=-1, keepdims=True)
    return (x * lax.rsqrt(ms + RMS_EPS)) * nw * (1.0 + scale) + shift


def _inproj_body(rowmap_ref, x_ref, r_ref, gate_ref, mod_ref, nw_ref, w_ref,
                 xo_ref, u_ref, q_ref, f_ref, i_ref, g_ref, gs_ref, gh_ref):
    row = rowmap_ref[pl.program_id(0)]
    gate = gate_ref[pl.ds(row, 1), :]
    xn = x_ref[...] + gate * r_ref[...]
    xo_ref[...] = xn
    shift = mod_ref[pl.ds(row, 1), 0:D_MODEL]
    scale = mod_ref[pl.ds(row, 1), D_MODEL:2 * D_MODEL]
    h = _rms_modulate(xn, nw_ref[...], scale, shift).astype(BF16)
    outs = (u_ref, q_ref, f_ref, i_ref, g_ref, gs_ref, gh_ref)
    for ref, (a, b) in zip(outs, _IN_SPLITS):
        ref[...] = jnp.dot(h, w_ref[:, a:b], preferred_element_type=F32).astype(ref.dtype)


def _inproj(rowmap, x, r, gate_rows, mod, nw, w_bf16):
    n_tok = x.shape[0]
    nblk = n_tok // TOKEN_BLOCK
    tok = lambda w: pl.BlockSpec((TOKEN_BLOCK, w), lambda j, rm: (j, 0))
    full = lambda shape: pl.BlockSpec(shape, lambda j, rm: (0,) * len(shape))
    widths = [b - a for a, b in _IN_SPLITS]
    dtypes = [F32, F32, F32, BF16, F32, F32, F32]
    return pl.pallas_call(
        _inproj_body,
        out_shape=[jax.ShapeDtypeStruct((n_tok, D_MODEL), F32)]
        + [jax.ShapeDtypeStruct((n_tok, w), dt) for w, dt in zip(widths, dtypes)],
        grid_spec=pltpu.PrefetchScalarGridSpec(
            num_scalar_prefetch=1, grid=(nblk,),
            in_specs=[tok(D_MODEL), tok(D_MODEL), full((SUBLANES, D_MODEL)), full((SUBLANES, 6 * D_MODEL)),
                      full((1, D_MODEL)), full((D_MODEL, IN_COLS))],
            out_specs=[tok(D_MODEL)] + [tok(w) for w in widths]),
        compiler_params=_cparams(1),
        name="inproj",
    )(rowmap, x, r, gate_rows, mod, nw, w_bf16)


def _s5_body(rev, jmap_ref, seq_ref, first_ref, last_ref,
             u_ref, bd_ref, cd_ref, lre_ref, lim_ref, lkre_ref, lkim_ref, h0re_ref, h0im_ref,
             y_ref, fre_ref, fim_ref,
             up_ref, bu_ref, sb_ref, cre_ref, cim_ref):
    p = pl.program_id(0)
    a = pl.program_id(1)
    ns = S5_BLOCK_STATES

    @pl.when(first_ref[p] == 1)
    def _():
        cre_ref[a] = h0re_ref[0, 0]
        cim_ref[a] = h0im_ref[0, 0]

    for k in range(S5_SEG):
        up_ref[pl.ds(SUBLANES * k, SUBLANES), :] = u_ref[pl.ds(k, SUBLANES, stride=S5_SEG), :]
    bu_ref[...] = jnp.dot(up_ref[...].astype(BF16), bd_ref[0], preferred_element_type=F32)

    lre = lre_ref[0]
    lim = lim_ref[0]
    order = list(range(S5_SEG))
    if rev:
        order = order[::-1]

    def step(sr, si, k):
        br = bu_ref[pl.ds(SUBLANES * k, SUBLANES), 0:ns]
        bi = bu_ref[pl.ds(SUBLANES * k, SUBLANES), ns:2 * ns]
        return lre * sr - lim * si + br, lre * si + lim * sr + bi

    sr = jnp.zeros((SUBLANES, ns), F32)
    si = jnp.zeros((SUBLANES, ns), F32)
    for k in order:
        sr, si = step(sr, si, k)

    cr = cre_ref[a]
    ci = cim_ref[a]
    lkre = lkre_ref[0]
    lkim = lkim_ref[0]
    start_r = [None] * SUBLANES
    start_i = [None] * SUBLANES
    seg_order = list(range(SUBLANES))
    if rev:
        seg_order = seg_order[::-1]
    for i in seg_order:
        start_r[i] = cr
        start_i[i] = ci
        er = sr[i:i + 1]
        ei = si[i:i + 1]
        cr, ci = lkre * cr - lkim * ci + er, lkre * ci + lkim * cr + ei
    cre_ref[a] = cr
    cim_ref[a] = ci
    fre_ref[0, 0] = cr
    fim_ref[0, 0] = ci

    sr = jnp.concatenate(start_r, axis=0)
    si = jnp.concatenate(start_i, axis=0)
    for n in range(0, S5_SEG, 2):
        k0, k1 = order[n], order[n + 1]
        sr0, si0 = step(sr, si, k0)
        sr, si = step(sr0, si0, k1)
        lo_r, hi_r = (sr0, sr) if k0 < k1 else (sr, sr0)
        lo_i, hi_i = (si0, si) if k0 < k1 else (si, si0)
        base = SUBLANES * min(k0, k1)
        sb_ref[pl.ds(base, 2 * SUBLANES), 0:ns] = jnp.concatenate([lo_r, hi_r], axis=0).astype(BF16)
        sb_ref[pl.ds(base, 2 * SUBLANES), ns:2 * ns] = jnp.concatenate([lo_i, hi_i], axis=0).astype(BF16)

    y = jnp.dot(sb_ref[...], cd_ref[0], preferred_element_type=F32)
    for k in range(S5_SEG):
        y_ref[pl.ds(k, SUBLANES, stride=S5_SEG), :] = y[SUBLANES * k:SUBLANES * (k + 1), :]


def _s5_scan(rev, tables, u, bd, cd, lre, lim, lkre, lkim, h0re, h0im):
    jmap, seq, first, last = tables
    n_tok = u.shape[0]
    nblk = n_tok // TOKEN_BLOCK
    n_seq = h0re.shape[0]
    ns = S5_BLOCK_STATES
    grid = (nblk, S5_COLBLOCKS)
    col = lambda shape: pl.BlockSpec(shape, lambda p, a, jm, sq, fi, la: (a,) + (0,) * (len(shape) - 1))
    st = pl.BlockSpec((1, 1, 1, ns), lambda p, a, jm, sq, fi, la: (sq[p], a, 0, 0))
    return pl.pallas_call(
        functools.partial(_s5_body, rev),
        out_shape=[jax.ShapeDtypeStruct((n_tok, S5_WIDTH), F32),
                   jax.ShapeDtypeStruct((n_seq, S5_COLBLOCKS, 1, ns), F32),
                   jax.ShapeDtypeStruct((n_seq, S5_COLBLOCKS, 1, ns), F32)],
        grid_spec=pltpu.PrefetchScalarGridSpec(
            num_scalar_prefetch=4, grid=grid,
            in_specs=[
                pl.BlockSpec((TOKEN_BLOCK, LANES), lambda p, a, jm, sq, fi, la: (jm[p], a)),
                col((1, LANES, 2 * ns)), col((1, 2 * ns, LANES)),
                col((1, SUBLANES, ns)), col((1, SUBLANES, ns)), col((1, 1, ns)), col((1, 1, ns)),
                st, st],
            out_specs=[pl.BlockSpec((TOKEN_BLOCK, LANES), lambda p, a, jm, sq, fi, la: (jm[p], a)), st, st],
            scratch_shapes=[pltpu.VMEM((TOKEN_BLOCK, LANES), F32),
                            pltpu.VMEM((TOKEN_BLOCK, 2 * ns), F32),
                            pltpu.VMEM((TOKEN_BLOCK, 2 * ns), BF16),
                            pltpu.VMEM((S5_COLBLOCKS, 1, ns), F32),
                            pltpu.VMEM((S5_COLBLOCKS, 1, ns), F32)]),
        compiler_params=_cparams(2),
        name="s5_bwd" if rev else "s5_fwd",
    )(jmap, seq, first, last, u, bd, cd, lre, lim, lkre, lkim, h0re, h0im)


def _s5_params(lam_re, lam_im, log_step, b_re, b_im, c_re, c_im):
    g, p, c = S5_GROUPS, S5_STATE, S5_GROUP
    gl = LANES // c
    lr = jnp.minimum(lam_re, -1e-4)
    li = lam_im
    step = jnp.exp(log_step)[:, None]
    mag = jnp.exp(lr * step)
    ar, ai = mag * jnp.cos(li * step), mag * jnp.sin(li * step)
    magk = jnp.exp(lr * step * S5_SEG)
    akr, aki = magk * jnp.cos(li * step * S5_SEG), magk * jnp.sin(li * step * S5_SEG)
    den = lr * lr + li * li
    cr = ((ar - 1.0) * lr + ai * li) / den
    ci = (ai * lr - (ar - 1.0) * li) / den
    bbr = cr[:, :, None] * b_re - ci[:, :, None] * b_im
    bbi = cr[:, :, None] * b_im + ci[:, :, None] * b_re
    eye = jnp.eye(gl, dtype=F32)

    def bmat(bb):
        bb = bb.reshape(S5_COLBLOCKS, gl, p, c)
        return jnp.einsum('agpc,gh->agchp', bb, eye).reshape(S5_COLBLOCKS, gl * c, gl * p)

    def cmat(cc):
        cc = cc.reshape(S5_COLBLOCKS, gl, c, p)
        return jnp.einsum('agcp,gh->agphc', cc, eye).reshape(S5_COLBLOCKS, gl * p, gl * c)

    bd = jnp.concatenate([bmat(bbr), bmat(bbi)], axis=2).astype(BF16)
    cd = jnp.concatenate([cmat(c_re), -cmat(c_im)], axis=1).astype(BF16)

    def lanes(v, rows):
        v = v.reshape(S5_COLBLOCKS, 1, gl * p)
        return jnp.broadcast_to(v, (S5_COLBLOCKS, rows, gl * p))

    return bd, cd, lanes(ar, SUBLANES), lanes(ai, SUBLANES), lanes(akr, 1), lanes(aki, 1)


def _hg_body(rev, jmap_ref, seq_ref, first_ref, last_ref,
             q_ref, f_ref, v_ref, lb_ref, h0_ref, o_ref, s_ref):
    p = pl.program_id(0)
    c_len = HG_CHUNK

    @pl.when(first_ref[p] == 1)
    def _():
        s_ref[...] = h0_ref[...]

    sub = HG_SUB
    n_sub = c_len // sub
    ri = lax.broadcasted_iota(jnp.int32, (c_len, c_len), 0)
    ci = lax.broadcasted_iota(jnp.int32, (c_len, c_len), 1)
    mask = (ri <= ci) if rev else (ri >= ci)
    tri = mask.astype(BF16)
    rb, cb = ri // sub, ci // sub
    diag_mask = mask & (rb == cb)
    off_mask = (cb > rb) if rev else (cb < rb)
    lb = lb_ref[...]
    tot_row = 0 if rev else c_len - 1
    n_chunks = TOKEN_BLOCK // c_len
    chunk_order = range(n_chunks - 1, -1, -1) if rev else range(n_chunks)
    for c in chunk_order:
        rows = pl.ds(c * c_len, c_len)
        qin = q_ref[rows, :]
        q = qin * jax.nn.sigmoid(qin) * (HG_DK ** -0.5)
        f = lb + (1.0 - lb) * jax.nn.sigmoid(f_ref[rows, :])
        k = 1.0 - f
        g = jnp.log(f)
        g0 = g.astype(BF16)
        r1 = g - g0.astype(F32)
        g1 = r1.astype(BF16)
        g2 = (r1 - g1.astype(F32)).astype(BF16)
        b = (jnp.dot(tri, g0, preferred_element_type=F32) + jnp.dot(tri, g1, preferred_element_type=F32)
             + jnp.dot(tri, g2, preferred_element_type=F32))
        btot = b[tot_row:tot_row + 1, :]
        bmid = jnp.concatenate(
            [jnp.broadcast_to(b[i * sub + sub // 2:i * sub + sub // 2 + 1, :], (sub, HG_KW)) for i in range(n_sub)], axis=0)
        qd = (q * jnp.exp(b - bmid)).astype(BF16)
        kd = (k * jnp.exp(bmid - b)).astype(BF16)
        qo, ko = {}, {}
        for i in range(n_sub):
            edge = (i + 1) * sub if rev else i * sub - 1
            if 0 <= edge < c_len:
                bref = b[edge:edge + 1, :]
                rws = slice(i * sub, (i + 1) * sub)
                qo[i] = (q[rws] * jnp.exp(b[rws] - bref)).astype(BF16)
                ko[i] = (k * jnp.exp(jnp.minimum(bref - b, 0.0))).astype(BF16)
        qs = (q * jnp.exp(b)).astype(BF16)
        ks = (k * jnp.exp(btot - b)).astype(BF16)
        dec = jnp.exp(btot)
        v = v_ref[rows, :]
        vt = v.astype(F32).T.astype(BF16)
        outs = []
        for h in range(HG_HEADS):
            sl = slice(h * HG_DK, (h + 1) * HG_DK)
            near = _dot_nt(qd[:, sl], kd[:, sl])
            far = jnp.concatenate(
                [_dot_nt(qo[i][:, sl], ko[i][:, sl]) if i in qo else jnp.zeros((sub, c_len), F32)
                 for i in range(n_sub)], axis=0)
            att = jnp.where(diag_mask, near, jnp.where(off_mask, far, 0.0)).astype(BF16)
            s_old = s_ref[0, h]
            o_h = jnp.dot(att, v[:, sl], preferred_element_type=F32) + _dot_nt(qs[:, sl], s_old.astype(BF16))
            s_ref[0, h] = s_old * dec[:, sl] + jnp.dot(vt[sl, :], ks[:, sl], preferred_element_type=F32)
            outs.append(o_h)
        o_ref[rows, :] = jnp.concatenate(outs, axis=1)


def _hg_scan(rev, tables, q, f, v, lb, h0):
    jmap, seq, first, last = tables
    n_tok = q.shape[0]
    nblk = n_tok // TOKEN_BLOCK
    n_seq = h0.shape[0]
    d = 1 if rev else 0
    tok = pl.BlockSpec((TOKEN_BLOCK, HG_WIDTH), lambda p, jm, sq, fi, la: (jm[p], 0))
    st = pl.BlockSpec((1, HG_HEADS, HG_DV, HG_DK), lambda p, jm, sq, fi, la: (sq[p], 0, 0, 0))
    return pl.pallas_call(
        functools.partial(_hg_body, rev),
        out_shape=[jax.ShapeDtypeStruct((n_tok, HG_WIDTH), F32),
                   jax.ShapeDtypeStruct((n_seq, HG_HEADS, HG_DV, HG_DK), F32)],
        grid_spec=pltpu.PrefetchScalarGridSpec(
            num_scalar_prefetch=4, grid=(nblk,),
            in_specs=[tok,
                      pl.BlockSpec((TOKEN_BLOCK, HG_KW), lambda p, jm, sq, fi, la: (jm[p], d)),
                      tok,
                      pl.BlockSpec((1, HG_KW), lambda p, jm, sq, fi, la: (0, 0)),
                      st],
            out_specs=[tok, st]),
        compiler_params=_cparams(1),
        name="hg_bwd" if rev else "hg_fwd",
    )(jmap, seq, first, last, q, f, v, lb, h0)


def _merge_body(rowmap_ref, x_ref, ysf_ref, ysb_ref, u_ref, ohf_ref, ohb_ref, g_ref, gs_ref, gh_ref,
                mod_ref, d_ref, bglu_ref, hnw_ref, n2w_ref, wglu_ref, wbs_ref, wbh_ref, wout_ref,
                rhi_ref, rlo_ref, xm_ref, h2_ref, sc_ref):
    row = rowmap_ref[pl.program_id(0)]
    mrow = lambda n: mod_ref[pl.ds(row, 1), n * D_MODEL:(n + 1) * D_MODEL]
    gate1, shift2, scale2 = mrow(2), mrow(3), mrow(4)

    y = ysf_ref[...] + ysb_ref[...] + d_ref[...] * u_ref[...]
    gl = jax.nn.gelu(y)
    z = jnp.dot(gl.astype(BF16), wglu_ref[...], preferred_element_type=F32) + bglu_ref[...]
    y_s5 = gl * jax.nn.sigmoid(z)

    o = ohf_ref[...] + ohb_ref[...]
    normed = []
    for h in range(HG_HEADS):
        oh = o[:, h * HG_DV:(h + 1) * HG_DV]
        normed.append(oh * lax.rsqrt(jnp.mean(oh * oh, axis=-1, keepdims=True) + RMS_EPS))
    gout = g_ref[...]
    y_hg = jnp.concatenate(normed, axis=1) * hnw_ref[...] * (gout * jax.nn.sigmoid(gout))

    merged = (jax.nn.sigmoid(gs_ref[...]) * jnp.dot(y_s5.astype(BF16), wbs_ref[...], preferred_element_type=F32)
              + jax.nn.sigmoid(gh_ref[...]) * jnp.dot(y_hg.astype(BF16), wbh_ref[...], preferred_element_type=F32))
    xm = x_ref[...] + gate1 * jnp.dot(merged.astype(BF16), wout_ref[...], preferred_element_type=F32)
    xm_ref[...] = xm
    h2 = _rms_modulate(xm, n2w_ref[...], scale2, shift2)
    h2_ref[...] = h2.astype(BF16)
    h_hi, h_lo = _split_bf16(h2)
    logits = _dot_nt(rhi_ref[...], h_hi) + _dot_nt(rhi_ref[...], h_lo) + _dot_nt(rlo_ref[...], h_hi)
    sc_ref[...] = jax.nn.sigmoid(logits)


def _merge(rowmap, x, ysf, ysb, u, ohf, ohb, g, gs, gh, mod, d, bglu, hnw, n2w,
           wglu, wbs, wbh, wout, rhi, rlo):
    n_tok = x.shape[0]
    nblk = n_tok // TOKEN_BLOCK
    tok = lambda w: pl.BlockSpec((TOKEN_BLOCK, w), lambda j, rm: (j, 0))
    full = lambda shape: pl.BlockSpec(shape, lambda j, rm: (0,) * len(shape))
    return pl.pallas_call(
        _merge_body,
        out_shape=[jax.ShapeDtypeStruct((n_tok, D_MODEL), F32),
                   jax.ShapeDtypeStruct((n_tok, D_MODEL), BF16),
                   jax.ShapeDtypeStruct((N_EXPERTS, n_tok), F32)],
        grid_spec=pltpu.PrefetchScalarGridSpec(
            num_scalar_prefetch=1, grid=(nblk,),
            in_specs=[tok(D_MODEL), tok(S5_WIDTH), tok(S5_WIDTH), tok(S5_WIDTH), tok(HG_WIDTH), tok(HG_WIDTH),
                      tok(HG_WIDTH), tok(D_MODEL), tok(D_MODEL),
                      full((SUBLANES, 6 * D_MODEL)), full((1, S5_WIDTH)), full((1, S5_WIDTH)),
                      full((1, HG_WIDTH)), full((1, D_MODEL)),
                      full((S5_WIDTH, S5_WIDTH)), full((S5_WIDTH, D_MODEL)), full((HG_WIDTH, D_MODEL)),
                      full((D_MODEL, D_MODEL)), full((N_EXPERTS, D_MODEL)), full((N_EXPERTS, D_MODEL))],
            out_specs=[tok(D_MODEL), tok(D_MODEL),
                       pl.BlockSpec((N_EXPERTS, TOKEN_BLOCK), lambda j, rm: (0, j))]),
        compiler_params=_cparams(1),
        name="merge",
    )(rowmap, x, ysf, ysb, u, ohf, ohb, g, gs, gh, mod, d, bglu, hnw, n2w, wglu, wbs, wbh, wout, rhi, rlo)


def _experts_body(be_ref, nused_ref, x_ref, wg_ref, wu_ref, wd_ref, y_ref, wgb_ref, wub_ref, wdb_ref):
    b = pl.program_id(0)
    e = be_ref[b]
    prev = be_ref[jnp.maximum(b - 1, 0)]

    @pl.when((b == 0) | (e != prev))
    def _():
        wgb_ref[...] = wg_ref[0].astype(BF16)
        wub_ref[...] = wu_ref[0].astype(BF16)
        wdb_ref[...] = wd_ref[0].astype(BF16)

    @pl.when(b < nused_ref[0])
    def _():
        x = x_ref[...]
        gate = jnp.dot(x, wgb_ref[...], preferred_element_type=F32)
        up = jnp.dot(x, wub_ref[...], preferred_element_type=F32)
        hid = (gate * jax.nn.sigmoid(gate) * up).astype(BF16)
        y_ref[...] = jnp.dot(hid, wdb_ref[...], preferred_element_type=F32)

    @pl.when(b >= nused_ref[0])
    def _():
        y_ref[...] = jnp.zeros_like(y_ref)


def _experts(blk_expert, nused, xb, w_gate, w_up, w_down):
    n_rows = xb.shape[0]
    nblk = n_rows // MOE_ROWS
    wspec = pl.BlockSpec((1, D_MODEL, D_MODEL), lambda b, be, nu: (be[b], 0, 0))
    return pl.pallas_call(
        _experts_body,
        out_shape=jax.ShapeDtypeStruct((n_rows, D_MODEL), F32),
        grid_spec=pltpu.PrefetchScalarGridSpec(
            num_scalar_prefetch=2, grid=(nblk,),
            in_specs=[pl.BlockSpec((MOE_ROWS, D_MODEL), lambda b, be, nu: (b, 0)), wspec, wspec, wspec],
            out_specs=pl.BlockSpec((MOE_ROWS, D_MODEL), lambda b, be, nu: (b, 0)),
            scratch_shapes=[pltpu.VMEM((D_MODEL, D_MODEL), BF16)] * 3),
        compiler_params=_cparams(1),
        name="experts",
    )(blk_expert, nused, xb, w_gate, w_up, w_down)


def _route(scores_t, router_b):
    n_tok = scores_t.shape[1]
    scores = scores_t.T
    sel = (scores + router_b).reshape(n_tok, N_GROUPS, EXPERTS_PER_GROUP)
    group_score = lax.top_k(sel, 2)[0].sum(-1)
    best = jnp.argmax(group_score, axis=-1)
    in_group = jnp.take_along_axis(sel, best[:, None, None], axis=1)[:, 0]
    _, local = lax.top_k(in_group, TOP_K)
    expert = (best[:, None] * EXPERTS_PER_GROUP + local).astype(jnp.int32)
    wts = jnp.take_along_axis(scores, expert, axis=1)
    wts = wts / jnp.sum(wts, axis=-1, keepdims=True)

    member = (jax.nn.one_hot(expert[:, 0], N_EXPERTS, dtype=jnp.int32)
              + jax.nn.one_hot(expert[:, 1], N_EXPERTS, dtype=jnp.int32))
    counts = member.sum(0)
    rank = jnp.cumsum(member, axis=0) - member
    padded = (counts + MOE_ROWS - 1) // MOE_ROWS * MOE_ROWS
    pad_end = jnp.cumsum(padded)
    pad_start = pad_end - padded
    dest = pad_start[expert] + jnp.take_along_axis(rank, expert, axis=1)
    n_blocks = -(-n_tok * TOP_K // MOE_ROWS) + N_EXPERTS
    blk_expert = jnp.minimum(
        jnp.searchsorted(pad_end, jnp.arange(n_blocks, dtype=jnp.int32) * MOE_ROWS, side='right'),
        N_EXPERTS - 1).astype(jnp.int32)
    nused = (pad_end[-1:] // MOE_ROWS).astype(jnp.int32)
    row_tok = jnp.zeros((n_blocks * MOE_ROWS,), jnp.int32).at[dest.reshape(-1)].set(
        jnp.repeat(jnp.arange(n_tok, dtype=jnp.int32), TOP_K))
    return dest, wts, blk_expert, nused, row_tok


def _final_body(rowmap_ref, x_ref, r_ref, gate_ref, nw_ref, o_ref):
    row = rowmap_ref[pl.program_id(0)]
    xn = x_ref[...] + gate_ref[pl.ds(row, 1), :] * r_ref[...]
    ms = jnp.mean(xn * xn, axis=-1, keepdims=True)
    o_ref[...] = xn * lax.rsqrt(ms + RMS_EPS) * nw_ref[...]


def _final(rowmap, x, r, gate_rows, nw):
    n_tok = x.shape[0]
    tok = pl.BlockSpec((TOKEN_BLOCK, D_MODEL), lambda j, rm: (j, 0))
    full = lambda shape: pl.BlockSpec(shape, lambda j, rm: (0,) * len(shape))
    return pl.pallas_call(
        _final_body,
        out_shape=jax.ShapeDtypeStruct((n_tok, D_MODEL), F32),
        grid_spec=pltpu.PrefetchScalarGridSpec(
            num_scalar_prefetch=1, grid=(n_tok // TOKEN_BLOCK,),
            in_specs=[tok, tok, full((SUBLANES, D_MODEL)), full((1, D_MODEL))],
            out_specs=tok),
        compiler_params=_cparams(1),
        name="final_norm",
    )(rowmap, x, r, gate_rows, nw)


def _grid_pos_embed(rows):
    r = jnp.repeat(jnp.arange(rows, dtype=F32), GRID_W)
    col = jnp.tile(jnp.arange(GRID_W, dtype=F32), rows)
    quarter = D_MODEL // 4
    omega = 1.0 / (10000.0 ** (jnp.arange(quarter, dtype=F32) / quarter))

    def axis_embed(pos):
        ang = pos[:, None] * omega[None, :]
        return jnp.concatenate([jnp.sin(ang), jnp.cos(ang)], axis=-1)

    return jnp.concatenate([axis_embed(r), axis_embed(col)], axis=-1)


def kernel(x_prompt, x_sample, state_s5_re, state_s5_im, state_hg, c, c_ctx, w_ada, b_ada, norm1_w, norm2_w, w_in, s5_lam_re, s5_lam_im, s5_log_step, s5_b_re, s5_b_im, s5_c_re, s5_c_im, s5_d, s5_w_glu, s5_b_glu, hg_lb_logits, hg_norm_w, w_br_s5, w_br_hg, w_out, router_w, router_b, moe_w_gate, moe_w_up, moe_w_down, norm_f_w):
    n_ctx, ctx_len, _ = x_prompt.shape
    n_lat, lat_len, _ = x_sample.shape
    n_ctx_tok = n_ctx * ctx_len
    n_seq = n_ctx + n_lat
    assert n_lat + 1 <= SUBLANES and ctx_len % TOKEN_BLOCK == 0 and lat_len % TOKEN_BLOCK == 0

    seq_np, first_np, last_np, modrow_np = _block_tables(n_ctx, ctx_len, n_lat, lat_len)
    nblk = seq_np.shape[0]
    jmap_f = np.arange(nblk, dtype=np.int32)
    jmap_b = jmap_f[::-1].copy()
    tab_f = tuple(jnp.asarray(t) for t in (jmap_f, seq_np, first_np, last_np))
    tab_b = tuple(jnp.asarray(t) for t in (jmap_b, seq_np[::-1].copy(), last_np[::-1].copy(), first_np[::-1].copy()))
    rowmap = jnp.asarray(modrow_np)

    cvecs = jnp.zeros((SUBLANES, D_MODEL), F32).at[0].set(c_ctx).at[1:1 + n_lat].set(c)
    mod_all = _adaln(cvecs, w_ada, b_ada)

    probs = jax.nn.softmax(hg_lb_logits.astype(F32), axis=0)
    lower_bounds = jnp.cumsum(probs, axis=0) - probs[0]

    pos = _grid_pos_embed(lat_len // GRID_W)
    x = jnp.concatenate([x_prompt.reshape(n_ctx_tok, D_MODEL), x_sample.reshape(n_lat * lat_len, D_MODEL)], axis=0)
    resid = jnp.concatenate([jnp.zeros((n_ctx_tok, D_MODEL), F32), jnp.tile(pos, (n_lat, 1))], axis=0)
    gate_rows = jnp.ones((SUBLANES, D_MODEL), F32)

    router_t = router_w.T.astype(F32)
    r_hi, r_lo = _split_bf16(router_t)
    ns = S5_BLOCK_STATES
    zeros_s5 = jnp.zeros((n_ctx, S5_COLBLOCKS, 1, ns), F32)
    zeros_hg = jnp.zeros((n_ctx, HG_HEADS, HG_DV, HG_DK), F32)

    new_s5_re, new_s5_im, new_hg = [], [], []
    for l in range(DEPTH):
        mod = mod_all[l]
        x, u, q_in, f_raw, i_in, g_out, gate_s5, gate_hg = _inproj(
            rowmap, x, resid, gate_rows, mod, norm1_w[l].reshape(1, D_MODEL), w_in[l].astype(BF16))

        ys, s5_fin_re, s5_fin_im, os_, hg_fin = [], [], [], [], []
        for d, tabs in enumerate((tab_f, tab_b)):
            bd, cd, lre, lim, lkre, lkim = _s5_params(
                s5_lam_re[l, d], s5_lam_im[l, d], s5_log_step[l, d], s5_b_re[l], s5_b_im[l], s5_c_re[l], s5_c_im[l])
            h0re = jnp.concatenate([zeros_s5, state_s5_re[:, l, d].astype(F32).reshape(n_lat, S5_COLBLOCKS, 1, ns)], axis=0)
            h0im = jnp.concatenate([zeros_s5, state_s5_im[:, l, d].astype(F32).reshape(n_lat, S5_COLBLOCKS, 1, ns)], axis=0)
            y_d, fre, fim = _s5_scan(bool(d), tabs, u, bd, cd, lre, lim, lkre, lkim, h0re, h0im)
            ys.append(y_d)
            s5_fin_re.append(fre[:n_ctx].reshape(n_ctx, S5_GROUPS, S5_STATE))
            s5_fin_im.append(fim[:n_ctx].reshape(n_ctx, S5_GROUPS, S5_STATE))

            h0 = jnp.concatenate([zeros_hg, jnp.swapaxes(state_hg[:, l, d].astype(F32), -1, -2)], axis=0)
            o_d, hfin = _hg_scan(bool(d), tabs, q_in, f_raw, i_in, lower_bounds[l, d].reshape(1, HG_KW), h0)
            os_.append(o_d)
            hg_fin.append(jnp.swapaxes(hfin[:n_ctx], -1, -2))
        new_s5_re.append(jnp.stack(s5_fin_re, axis=1))
        new_s5_im.append(jnp.stack(s5_fin_im, axis=1))
        new_hg.append(jnp.stack(hg_fin, axis=1))

        x, h2, scores_t = _merge(
            rowmap, x, ys[0], ys[1], u, os_[0], os_[1], g_out, gate_s5, gate_hg, mod,
            s5_d[l].reshape(1, S5_WIDTH), s5_b_glu[l].reshape(1, S5_WIDTH), hg_norm_w[l].reshape(1, HG_WIDTH),
            norm2_w[l].reshape(1, D_MODEL), s5_w_glu[l].astype(BF16), w_br_s5[l].astype(BF16),
            w_br_hg[l].astype(BF16), w_out[l].astype(BF16), r_hi, r_lo)

        dest, wts, blk_expert, nused, row_tok = _route(scores_t, router_b.astype(F32))
        yb = _experts(blk_expert, nused, h2[row_tok], moe_w_gate[l], moe_w_up[l], moe_w_down[l])
        resid = yb[dest[:, 0]] * wts[:, 0:1] + yb[dest[:, 1]] * wts[:, 1:2]
        gate_rows = mod[:, 5 * D_MODEL:6 * D_MODEL]

    y = _final(rowmap, x, resid, gate_rows, norm_f_w.reshape(1, D_MODEL))
    y_prompt = y[:n_ctx_tok].reshape(n_ctx, ctx_len, D_MODEL)
    y_sample = y[n_ctx_tok:].reshape(n_lat, lat_len, D_MODEL)
    new_state_s5_re = jnp.stack(new_s5_re, axis=1).astype(state_s5_re.dtype)
    new_state_s5_im = jnp.stack(new_s5_im, axis=1).astype(state_s5_im.dtype)
    new_state_hg = jnp.stack(new_hg, axis=1).astype(state_hg.dtype)
    return (y_prompt, y_sample, new_state_s5_re, new_state_s5_im, new_state_hg)
```

```python
import functools
import math

import jax
import jax.numpy as jnp
import numpy as np
from jax import lax
from jax.experimental import pallas as pl
from jax.experimental.pallas import tpu as pltpu

F32 = jnp.float32
BF16 = jnp.bfloat16

D_MODEL = 1024
DEPTH = 4
GRID_W = 64
RMS_EPS = 1e-6
S5_WIDTH = 512
S5_GROUP = 16
S5_GROUPS = 32
S5_STATE = 64
HG_HEADS = 4
HG_DK = 128
HG_DV = 128
HG_WIDTH = 512
HG_KW = 512
HG_CHUNK = 64
HG_SUB = 16
N_EXPERTS = 16
N_GROUPS = 4
EXPERTS_PER_GROUP = 4
TOP_K = 2
IN_COLS = 5120

LANES = 128
SUBLANES = 8
VMEM_LIMIT_BYTES = 56 * 1024 * 1024

TOKEN_BLOCK = 256
S5_SEG = TOKEN_BLOCK // SUBLANES
MXU_DEPTH = 256
S5_CHANNELS = MXU_DEPTH
S5_COLBLOCKS = S5_WIDTH // S5_CHANNELS
S5_BLOCK_STATES = (S5_CHANNELS // S5_GROUP) * S5_STATE
assert S5_CHANNELS == 2 * LANES
MOE_ROWS = 256


def _block_tables(n_ctx_seq, ctx_len, n_lat_seq, lat_len):
    seq, first, last, modrow = [], [], [], []
    for s in range(n_ctx_seq):
        nb = ctx_len // TOKEN_BLOCK
        for b in range(nb):
            seq.append(s); first.append(int(b == 0)); last.append(int(b == nb - 1)); modrow.append(0)
    for s in range(n_lat_seq):
        nb = lat_len // TOKEN_BLOCK
        for b in range(nb):
            seq.append(n_ctx_seq + s); first.append(int(b == 0)); last.append(int(b == nb - 1))
            modrow.append(1 + s)
    return (np.asarray(seq, np.int32), np.asarray(first, np.int32),
            np.asarray(last, np.int32), np.asarray(modrow, np.int32))


def _cparams(n_axes):
    return pltpu.CompilerParams(dimension_semantics=("arbitrary",) * n_axes,
                                vmem_limit_bytes=VMEM_LIMIT_BYTES)


def _split_bf16(a):
    hi = a.astype(BF16)
    lo = (a - hi.astype(F32)).astype(BF16)
    return hi, lo


def _dot_nt(a, b):
    return lax.dot_general(a, b, (((1,), (1,)), ((), ())), preferred_element_type=F32)


def _adaln_body(c_ref, w_ref, b_ref, o_ref):
    a = c_ref[...]
    a = a * jax.nn.sigmoid(a)
    a_hi, a_lo = _split_bf16(a)
    w_hi, w_lo = _split_bf16(w_ref[0])
    acc = jnp.dot(a_hi, w_hi, preferred_element_type=F32)
    acc += jnp.dot(a_lo, w_hi, preferred_element_type=F32)
    acc += jnp.dot(a_hi, w_lo, preferred_element_type=F32)
    o_ref[0] = acc + b_ref[0]


def _adaln(cvecs, w_ada, b_ada):
    tn = 1536
    n_tiles = (6 * D_MODEL) // tn
    return pl.pallas_call(
        _adaln_body,
        out_shape=jax.ShapeDtypeStruct((DEPTH, SUBLANES, 6 * D_MODEL), F32),
        grid=(DEPTH, n_tiles),
        in_specs=[
            pl.BlockSpec((SUBLANES, D_MODEL), lambda l, n: (0, 0)),
            pl.BlockSpec((1, D_MODEL, tn), lambda l, n: (l, 0, n)),
            pl.BlockSpec((1, 1, tn), lambda l, n: (l, 0, n)),
        ],
        out_specs=pl.BlockSpec((1, SUBLANES, tn), lambda l, n: (l, 0, n)),
        compiler_params=_cparams(2),
        name="adaln",
    )(cvecs, w_ada, b_ada.reshape(DEPTH, 1, 6 * D_MODEL))


_IN_SPLITS = ((0, 512), (512, 1024), (1024, 2048), (2048, 2560), (2560, 3072), (3072, 4096), (4096, 5120))


def _rms_modulate(x, nw, scale, shift):
    ms = jnp.mean(x * x, axis=-1, keepdims=True)
    return (x * lax.rsqrt(ms + RMS_EPS)) * nw * (1.0 + scale) + shift


def _inproj_body(rowmap_ref, x_ref, r_ref, gate_ref, mod_ref, nw_ref, w_ref,
                 xo_ref, u_ref, q_ref, f_ref, i_ref, g_ref, gs_ref, gh_ref):
    row = rowmap_ref[pl.program_id(0)]
    gate = gate_ref[pl.ds(row, 1), :]
    xn = x_ref[...] + gate * r_ref[...]
    xo_ref[...] = xn
    shift = mod_ref[pl.ds(row, 1), 0:D_MODEL]
    scale = mod_ref[pl.ds(row, 1), D_MODEL:2 * D_MODEL]
    h = _rms_modulate(xn, nw_ref[...], scale, shift).astype(BF16)
    outs = (u_ref, q_ref, f_ref, i_ref, g_ref, gs_ref, gh_ref)
    for ref, (a, b) in zip(outs, _IN_SPLITS):
        ref[...] = jnp.dot(h, w_ref[:, a:b], preferred_element_type=F32).astype(ref.dtype)


def _inproj(rowmap, x, r, gate_rows, mod, nw, w_bf16):
    n_tok = x.shape[0]
    nblk = n_tok // TOKEN_BLOCK
    tok = lambda w: pl.BlockSpec((TOKEN_BLOCK, w), lambda j, rm: (j, 0))
    full = lambda shape: pl.BlockSpec(shape, lambda j, rm: (0,) * len(shape))
    widths = [b - a for a, b in _IN_SPLITS]
    dtypes = [F32, F32, F32, BF16, F32, F32, F32]
    return pl.pallas_call(
        _inproj_body,
        out_shape=[jax.ShapeDtypeStruct((n_tok, D_MODEL), F32)]
        + [jax.ShapeDtypeStruct((n_tok, w), dt) for w, dt in zip(widths, dtypes)],
        grid_spec=pltpu.PrefetchScalarGridSpec(
            num_scalar_prefetch=1, grid=(nblk,),
            in_specs=[tok(D_MODEL), tok(D_MODEL), full((SUBLANES, D_MODEL)), full((SUBLANES, 6 * D_MODEL)),
                      full((1, D_MODEL)), full((D_MODEL, IN_COLS))],
            out_specs=[tok(D_MODEL)] + [tok(w) for w in widths]),
        compiler_params=_cparams(1),
        name="inproj",
    )(rowmap, x, r, gate_rows, mod, nw, w_bf16)


def _s5_body(rev, jmap_ref, seq_ref, first_ref, last_ref,
             ua_ref, ub_ref, bd_ref, cd_ref, lre_ref, lim_ref, lkre_ref, lkim_ref, h0re_ref, h0im_ref,
             y_ref, fre_ref, fim_ref,
             up_ref, bu_ref, sb_ref, yp_ref, cre_ref, cim_ref):
    p = pl.program_id(1)
    ns = S5_BLOCK_STATES

    @pl.when(first_ref[p] == 1)
    def _():
        cre_ref[...] = h0re_ref[0, 0]
        cim_ref[...] = h0im_ref[0, 0]

    for h, u_ref in enumerate((ua_ref, ub_ref)):
        for k in range(S5_SEG):
            up_ref[pl.ds(SUBLANES * k, SUBLANES), h * LANES:(h + 1) * LANES] = u_ref[pl.ds(k, SUBLANES, stride=S5_SEG), :]
    bu_ref[...] = jnp.dot(up_ref[...].astype(BF16), bd_ref[0], preferred_element_type=F32)

    lre = lre_ref[0]
    lim = lim_ref[0]
    order = list(range(S5_SEG))
    if rev:
        order = order[::-1]

    def step(sr, si, k):
        br = bu_ref[pl.ds(SUBLANES * k, SUBLANES), 0:ns]
        bi = bu_ref[pl.ds(SUBLANES * k, SUBLANES), ns:2 * ns]
        return lre * sr - lim * si + br, lre * si + lim * sr + bi

    sr = jnp.zeros((SUBLANES, ns), F32)
    si = jnp.zeros((SUBLANES, ns), F32)
    for k in order:
        sr, si = step(sr, si, k)

    cr = cre_ref[...]
    ci = cim_ref[...]
    lkre = lkre_ref[0]
    lkim = lkim_ref[0]
    start_r = [None] * SUBLANES
    start_i = [None] * SUBLANES
    seg_order = list(range(SUBLANES))
    if rev:
        seg_order = seg_order[::-1]
    for i in seg_order:
        start_r[i] = cr
        start_i[i] = ci
        er = sr[i:i + 1]
        ei = si[i:i + 1]
        cr, ci = lkre * cr - lkim * ci + er, lkre * ci + lkim * cr + ei
    cre_ref[...] = cr
    cim_ref[...] = ci
    fre_ref[0, 0] = cr
    fim_ref[0, 0] = ci

    sr = jnp.concatenate(start_r, axis=0)
    si = jnp.concatenate(start_i, axis=0)
    for n in range(0, S5_SEG, 2):
        k0, k1 = order[n], order[n + 1]
        sr0, si0 = step(sr, si, k0)
        sr, si = step(sr0, si0, k1)
        lo_r, hi_r = (sr0, sr) if k0 < k1 else (sr, sr0)
        lo_i, hi_i = (si0, si) if k0 < k1 else (si, si0)
        base = SUBLANES * min(k0, k1)
        sb_ref[pl.ds(base, 2 * SUBLANES), 0:ns] = jnp.concatenate([lo_r, hi_r], axis=0).astype(BF16)
        sb_ref[pl.ds(base, 2 * SUBLANES), ns:2 * ns] = jnp.concatenate([lo_i, hi_i], axis=0).astype(BF16)

    y = jnp.dot(sb_ref[...], cd_ref[0], preferred_element_type=F32)
    for h in range(S5_CHANNELS // LANES):
        for k in range(S5_SEG):
            yp_ref[h, pl.ds(k, SUBLANES, stride=S5_SEG), :] = y[SUBLANES * k:SUBLANES * (k + 1), h * LANES:(h + 1) * LANES]
        y_ref[:, h * LANES:(h + 1) * LANES] = yp_ref[h]


def _s5_scan(rev, tables, u, bd, cd, lre, lim, lkre, lkim, h0re, h0im):
    jmap, seq, first, last = tables
    n_tok = u.shape[0]
    nblk = n_tok // TOKEN_BLOCK
    n_seq = h0re.shape[0]
    ns = S5_BLOCK_STATES
    grid = (S5_COLBLOCKS, nblk)
    col = lambda shape: pl.BlockSpec(shape, lambda a, p, jm, sq, fi, la: (a,) + (0,) * (len(shape) - 1))
    st = pl.BlockSpec((1, 1, 1, ns), lambda a, p, jm, sq, fi, la: (sq[p], a, 0, 0))
    return pl.pallas_call(
        functools.partial(_s5_body, rev),
        out_shape=[jax.ShapeDtypeStruct((n_tok, S5_WIDTH), F32),
                   jax.ShapeDtypeStruct((n_seq, S5_COLBLOCKS, 1, ns), F32),
                   jax.ShapeDtypeStruct((n_seq, S5_COLBLOCKS, 1, ns), F32)],
        grid_spec=pltpu.PrefetchScalarGridSpec(
            num_scalar_prefetch=4, grid=grid,
            in_specs=[
                pl.BlockSpec((TOKEN_BLOCK, LANES), lambda a, p, jm, sq, fi, la: (jm[p], 2 * a)),
                pl.BlockSpec((TOKEN_BLOCK, LANES), lambda a, p, jm, sq, fi, la: (jm[p], 2 * a + 1)),
                col((1, S5_CHANNELS, 2 * ns)), col((1, 2 * ns, S5_CHANNELS)),
                col((1, SUBLANES, ns)), col((1, SUBLANES, ns)), col((1, 1, ns)), col((1, 1, ns)),
                st, st],
            out_specs=[pl.BlockSpec((TOKEN_BLOCK, S5_CHANNELS), lambda a, p, jm, sq, fi, la: (jm[p], a)), st, st],
            scratch_shapes=[pltpu.VMEM((TOKEN_BLOCK, S5_CHANNELS), F32),
                            pltpu.VMEM((TOKEN_BLOCK, 2 * ns), F32),
                            pltpu.VMEM((TOKEN_BLOCK, 2 * ns), BF16),
                            pltpu.VMEM((S5_CHANNELS // LANES, TOKEN_BLOCK, LANES), F32),
                            pltpu.VMEM((1, ns), F32),
                            pltpu.VMEM((1, ns), F32)]),
        compiler_params=_cparams(2),
        name="s5_bwd" if rev else "s5_fwd",
    )(jmap, seq, first, last, u, u, bd, cd, lre, lim, lkre, lkim, h0re, h0im)


def _s5_params(lam_re, lam_im, log_step, b_re, b_im, c_re, c_im):
    g, p, c = S5_GROUPS, S5_STATE, S5_GROUP
    gl = S5_CHANNELS // c
    lr = jnp.minimum(lam_re, -1e-4)
    li = lam_im
    step = jnp.exp(log_step)[:, None]
    mag = jnp.exp(lr * step)
    ar, ai = mag * jnp.cos(li * step), mag * jnp.sin(li * step)
    magk = jnp.exp(lr * step * S5_SEG)
    akr, aki = magk * jnp.cos(li * step * S5_SEG), magk * jnp.sin(li * step * S5_SEG)
    den = lr * lr + li * li
    cr = ((ar - 1.0) * lr + ai * li) / den
    ci = (ai * lr - (ar - 1.0) * li) / den
    bbr = cr[:, :, None] * b_re - ci[:, :, None] * b_im
    bbi = cr[:, :, None] * b_im + ci[:, :, None] * b_re
    eye = jnp.eye(gl, dtype=F32)

    def bmat(bb):
        bb = bb.reshape(S5_COLBLOCKS, gl, p, c)
        return jnp.einsum('agpc,gh->agchp', bb, eye).reshape(S5_COLBLOCKS, gl * c, gl * p)

    def cmat(cc):
        cc = cc.reshape(S5_COLBLOCKS, gl, c, p)
        return jnp.einsum('agcp,gh->agphc', cc, eye).reshape(S5_COLBLOCKS, gl * p, gl * c)

    bd = jnp.concatenate([bmat(bbr), bmat(bbi)], axis=2).astype(BF16)
    cd = jnp.concatenate([cmat(c_re), -cmat(c_im)], axis=1).astype(BF16)

    def lanes(v, rows):
        v = v.reshape(S5_COLBLOCKS, 1, gl * p)
        return jnp.broadcast_to(v, (S5_COLBLOCKS, rows, gl * p))

    return bd, cd, lanes(ar, SUBLANES), lanes(ai, SUBLANES), lanes(akr, 1), lanes(aki, 1)


def _hg_body(rev, jmap_ref, seq_ref, first_ref, last_ref,
             q_ref, f_ref, v_ref, lb_ref, h0_ref, o_ref, s_ref):
    p = pl.program_id(0)
    c_len = HG_CHUNK

    @pl.when(first_ref[p] == 1)
    def _():
        s_ref[...] = h0_ref[...]

    sub = HG_SUB
    n_sub = c_len // sub
    ri = lax.broadcasted_iota(jnp.int32, (c_len, c_len), 0)
    ci = lax.broadcasted_iota(jnp.int32, (c_len, c_len), 1)
    mask = (ri <= ci) if rev else (ri >= ci)
    tri = mask.astype(BF16)
    rb, cb = ri // sub, ci // sub
    diag_mask = mask & (rb == cb)
    off_mask = (cb > rb) if rev else (cb < rb)
    lb = lb_ref[...]
    tot_row = 0 if rev else c_len - 1
    n_chunks = TOKEN_BLOCK // c_len
    chunk_order = range(n_chunks - 1, -1, -1) if rev else range(n_chunks)
    for c in chunk_order:
        rows = pl.ds(c * c_len, c_len)
        qin = q_ref[rows, :]
        q = qin * jax.nn.sigmoid(qin) * (HG_DK ** -0.5)
        f = lb + (1.0 - lb) * jax.nn.sigmoid(f_ref[rows, :])
        k = 1.0 - f
        g = jnp.log(f)
        g0 = g.astype(BF16)
        r1 = g - g0.astype(F32)
        g1 = r1.astype(BF16)
        g2 = (r1 - g1.astype(F32)).astype(BF16)
        b = (jnp.dot(tri, g0, preferred_element_type=F32) + jnp.dot(tri, g1, preferred_element_type=F32)
             + jnp.dot(tri, g2, preferred_element_type=F32))
        btot = b[tot_row:tot_row + 1, :]
        bmid = jnp.concatenate(
            [jnp.broadcast_to(b[i * sub + sub // 2:i * sub + sub // 2 + 1, :], (sub, HG_KW)) for i in range(n_sub)], axis=0)
        qd = (q * jnp.exp(b - bmid)).astype(BF16)
        kd = (k * jnp.exp(bmid - b)).astype(BF16)
        qo, ko = {}, {}
        for i in range(n_sub):
            edge = (i + 1) * sub if rev else i * sub - 1
            if 0 <= edge < c_len:
                bref = b[edge:edge + 1, :]
                rws = slice(i * sub, (i + 1) * sub)
                qo[i] = (q[rws] * jnp.exp(b[rws] - bref)).astype(BF16)
                ko[i] = (k * jnp.exp(jnp.minimum(bref - b, 0.0))).astype(BF16)
        qs = (q * jnp.exp(b)).astype(BF16)
        ks = (k * jnp.exp(btot - b)).astype(BF16)
        dec = jnp.exp(btot)
        v = v_ref[rows, :]
        vt = v.astype(F32).T.astype(BF16)
        outs = []
        for h in range(HG_HEADS):
            sl = slice(h * HG_DK, (h + 1) * HG_DK)
            near = _dot_nt(qd[:, sl], kd[:, sl])
            far = jnp.concatenate(
                [_dot_nt(qo[i][:, sl], ko[i][:, sl]) if i in qo else jnp.zeros((sub, c_len), F32)
                 for i in range(n_sub)], axis=0)
            att = jnp.where(diag_mask, near, jnp.where(off_mask, far, 0.0)).astype(BF16)
            s_old = s_ref[0, h]
            o_h = jnp.dot(att, v[:, sl], preferred_element_type=F32) + _dot_nt(qs[:, sl], s_old.astype(BF16))
            s_ref[0, h] = s_old * dec[:, sl] + jnp.dot(vt[sl, :], ks[:, sl], preferred_element_type=F32)
            outs.append(o_h)
        o_ref[rows, :] = jnp.concatenate(outs, axis=1)


def _hg_scan(rev, tables, q, f, v, lb, h0):
    jmap, seq, first, last = tables
    n_tok = q.shape[0]
    nblk = n_tok // TOKEN_BLOCK
    n_seq = h0.shape[0]
    d = 1 if rev else 0
    tok = pl.BlockSpec((TOKEN_BLOCK, HG_WIDTH), lambda p, jm, sq, fi, la: (jm[p], 0))
    st = pl.BlockSpec((1, HG_HEADS, HG_DV, HG_DK), lambda p, jm, sq, fi, la: (sq[p], 0, 0, 0))
    return pl.pallas_call(
        functools.partial(_hg_body, rev),
        out_shape=[jax.ShapeDtypeStruct((n_tok, HG_WIDTH), F32),
                   jax.ShapeDtypeStruct((n_seq, HG_HEADS, HG_DV, HG_DK), F32)],
        grid_spec=pltpu.PrefetchScalarGridSpec(
            num_scalar_prefetch=4, grid=(nblk,),
            in_specs=[tok,
                      pl.BlockSpec((TOKEN_BLOCK, HG_KW), lambda p, jm, sq, fi, la: (jm[p], d)),
                      tok,
                      pl.BlockSpec((1, HG_KW), lambda p, jm, sq, fi, la: (0, 0)),
                      st],
            out_specs=[tok, st]),
        compiler_params=_cparams(1),
        name="hg_bwd" if rev else "hg_fwd",
    )(jmap, seq, first, last, q, f, v, lb, h0)


def _merge_body(rowmap_ref, x_ref, ysf_ref, ysb_ref, u_ref, ohf_ref, ohb_ref, g_ref, gs_ref, gh_ref,
                mod_ref, d_ref, bglu_ref, hnw_ref, n2w_ref, wglu_ref, wbs_ref, wbh_ref, wout_ref,
                rhi_ref, rlo_ref, xm_ref, h2_ref, sc_ref):
    row = rowmap_ref[pl.program_id(0)]
    mrow = lambda n: mod_ref[pl.ds(row, 1), n * D_MODEL:(n + 1) * D_MODEL]
    gate1, shift2, scale2 = mrow(2), mrow(3), mrow(4)

    y = ysf_ref[...] + ysb_ref[...] + d_ref[...] * u_ref[...]
    gl = jax.nn.gelu(y)
    z = jnp.dot(gl.astype(BF16), wglu_ref[...], preferred_element_type=F32) + bglu_ref[...]
    y_s5 = gl * jax.nn.sigmoid(z)

    o = ohf_ref[...] + ohb_ref[...]
    normed = []
    for h in range(HG_HEADS):
        oh = o[:, h * HG_DV:(h + 1) * HG_DV]
        normed.append(oh * lax.rsqrt(jnp.mean(oh * oh, axis=-1, keepdims=True) + RMS_EPS))
    gout = g_ref[...]
    y_hg = jnp.concatenate(normed, axis=1) * hnw_ref[...] * (gout * jax.nn.sigmoid(gout))

    merged = (jax.nn.sigmoid(gs_ref[...]) * jnp.dot(y_s5.astype(BF16), wbs_ref[...], preferred_element_type=F32)
              + jax.nn.sigmoid(gh_ref[...]) * jnp.dot(y_hg.astype(BF16), wbh_ref[...], preferred_element_type=F32))
    xm = x_ref[...] + gate1 * jnp.dot(merged.astype(BF16), wout_ref[...], preferred_element_type=F32)
    xm_ref[...] = xm
    h2 = _rms_modulate(xm, n2w_ref[...], scale2, shift2)
    h2_ref[...] = h2
    h_hi, h_lo = _split_bf16(h2)
    logits = _dot_nt(rhi_ref[...], h_hi) + _dot_nt(rhi_ref[...], h_lo) + _dot_nt(rlo_ref[...], h_hi)
    sc_ref[...] = jax.nn.sigmoid(logits)


def _merge(rowmap, x, ysf, ysb, u, ohf, ohb, g, gs, gh, mod, d, bglu, hnw, n2w,
           wglu, wbs, wbh, wout, rhi, rlo):
    n_tok = x.shape[0]
    nblk = n_tok // TOKEN_BLOCK
    tok = lambda w: pl.BlockSpec((TOKEN_BLOCK, w), lambda j, rm: (j, 0))
    full = lambda shape: pl.BlockSpec(shape, lambda j, rm: (0,) * len(shape))
    return pl.pallas_call(
        _merge_body,
        out_shape=[jax.ShapeDtypeStruct((n_tok, D_MODEL), F32),
                   jax.ShapeDtypeStruct((n_tok, D_MODEL), F32),
                   jax.ShapeDtypeStruct((N_EXPERTS, n_tok), F32)],
        grid_spec=pltpu.PrefetchScalarGridSpec(
            num_scalar_prefetch=1, grid=(nblk,),
            in_specs=[tok(D_MODEL), tok(S5_WIDTH), tok(S5_WIDTH), tok(S5_WIDTH), tok(HG_WIDTH), tok(HG_WIDTH),
                      tok(HG_WIDTH), tok(D_MODEL), tok(D_MODEL),
                      full((SUBLANES, 6 * D_MODEL)), full((1, S5_WIDTH)), full((1, S5_WIDTH)),
                      full((1, HG_WIDTH)), full((1, D_MODEL)),
                      full((S5_WIDTH, S5_WIDTH)), full((S5_WIDTH, D_MODEL)), full((HG_WIDTH, D_MODEL)),
                      full((D_MODEL, D_MODEL)), full((N_EXPERTS, D_MODEL)), full((N_EXPERTS, D_MODEL))],
            out_specs=[tok(D_MODEL), tok(D_MODEL),
                       pl.BlockSpec((N_EXPERTS, TOKEN_BLOCK), lambda j, rm: (0, j))]),
        compiler_params=_cparams(1),
        name="merge",
    )(rowmap, x, ysf, ysb, u, ohf, ohb, g, gs, gh, mod, d, bglu, hnw, n2w, wglu, wbs, wbh, wout, rhi, rlo)


def _experts_body(be_ref, nused_ref, x_ref, wg_ref, wu_ref, wd_ref, y_ref, wgb_ref, wub_ref, wdb_ref):
    b = pl.program_id(0)
    e = be_ref[b]
    prev = be_ref[jnp.maximum(b - 1, 0)]

    @pl.when((b == 0) | (e != prev))
    def _():
        wgb_ref[...] = wg_ref[0, 0].astype(BF16)
        wub_ref[...] = wu_ref[0, 0].astype(BF16)
        wdb_ref[...] = wd_ref[0, 0].astype(BF16)

    @pl.when(b < nused_ref[0])
    def _():
        x = x_ref[...].astype(BF16)
        gate = jnp.dot(x, wgb_ref[...], preferred_element_type=F32)
        up = jnp.dot(x, wub_ref[...], preferred_element_type=F32)
        hid = (gate * jax.nn.sigmoid(gate) * up).astype(BF16)
        y_ref[...] = jnp.dot(hid, wdb_ref[...], preferred_element_type=F32)

    @pl.when(b >= nused_ref[0])
    def _():
        y_ref[...] = jnp.zeros_like(y_ref)


def _experts(layer, blk_expert, nused, xb, w_gate, w_up, w_down):
    n_rows = xb.shape[0]
    nblk = n_rows // MOE_ROWS
    wspec = pl.BlockSpec((1, 1, D_MODEL, D_MODEL), lambda b, be, nu: (layer, be[b], 0, 0))
    return pl.pallas_call(
        _experts_body,
        out_shape=jax.ShapeDtypeStruct((n_rows, D_MODEL), F32),
        grid_spec=pltpu.PrefetchScalarGridSpec(
            num_scalar_prefetch=2, grid=(nblk,),
            in_specs=[pl.BlockSpec((MOE_ROWS, D_MODEL), lambda b, be, nu: (b, 0)), wspec, wspec, wspec],
            out_specs=pl.BlockSpec((MOE_ROWS, D_MODEL), lambda b, be, nu: (b, 0)),
            scratch_shapes=[pltpu.VMEM((D_MODEL, D_MODEL), BF16)] * 3),
        compiler_params=_cparams(1),
        name="experts",
    )(blk_expert, nused, xb, w_gate, w_up, w_down)


def _first_max(vals):
    best, idx = vals[0], jnp.zeros(vals[0].shape, jnp.int32)
    for j in range(1, len(vals)):
        upd = vals[j] > best
        idx = jnp.where(upd, j, idx)
        best = jnp.where(upd, vals[j], best)
    return best, idx


def _pick(idx, vals):
    out = vals[-1]
    for j in range(len(vals) - 2, -1, -1):
        out = jnp.where(idx == j, vals[j], out)
    return out


def _router_body(sc_ref, rb_ref, eid_ref, wts_ref, rank_ref, cnt_ref, base_ref):
    j = pl.program_id(0)

    @pl.when(j == 0)
    def _():
        base_ref[...] = jnp.zeros_like(base_ref)

    s = sc_ref[...]
    sel = s + rb_ref[...]
    srow = [s[e:e + 1, :] for e in range(N_EXPERTS)]
    brow = [sel[e:e + 1, :] for e in range(N_EXPERTS)]
    epg = EXPERTS_PER_GROUP
    gscore = []
    for g in range(N_GROUPS):
        x = brow[g * epg:(g + 1) * epg]
        pairs = [x[a] + x[b] for a in range(epg) for b in range(a + 1, epg)]
        gscore.append(functools.reduce(jnp.maximum, pairs))
    _, bestg = _first_max(gscore)
    y = [_pick(bestg, [brow[g * epg + i] for g in range(N_GROUPS)]) for i in range(epg)]
    sy = [_pick(bestg, [srow[g * epg + i] for g in range(N_GROUPS)]) for i in range(epg)]
    _, i1 = _first_max(y)
    v2 = jnp.full(y[0].shape, -jnp.inf, F32)
    i2 = jnp.zeros(y[0].shape, jnp.int32)
    for i in range(epg):
        upd = (i1 != i) & (y[i] > v2)
        i2 = jnp.where(upd, i, i2)
        v2 = jnp.where(upd, y[i], v2)
    e0 = bestg * epg + i1
    e1 = bestg * epg + i2
    w0 = _pick(i1, sy)
    w1 = _pick(i2, sy)
    wsum = w0 + w1
    eid_ref[0:1, :] = e0
    eid_ref[1:2, :] = e1
    wts_ref[0:1, :] = w0 / wsum
    wts_ref[1:2, :] = w1 / wsum

    eiota = lax.broadcasted_iota(jnp.int32, s.shape, 0)
    m0 = eiota == e0
    m1 = eiota == e1
    member = (m0 | m1).astype(BF16)
    tb = s.shape[1]
    before = (lax.broadcasted_iota(jnp.int32, (tb, tb), 0) < lax.broadcasted_iota(jnp.int32, (tb, tb), 1)).astype(BF16)
    pos = base_ref[...] + jnp.dot(member, before, preferred_element_type=F32)
    rank_ref[0:1, :] = jnp.sum(jnp.where(m0, pos, 0.0), axis=0, keepdims=True).astype(jnp.int32)
    rank_ref[1:2, :] = jnp.sum(jnp.where(m1, pos, 0.0), axis=0, keepdims=True).astype(jnp.int32)
    base = base_ref[...] + jnp.sum(member.astype(F32), axis=1, keepdims=True)
    base_ref[...] = base
    cnt_ref[...] = jnp.broadcast_to(base, cnt_ref.shape).astype(jnp.int32)


def _router(scores_t, router_b):
    n_tok = scores_t.shape[1]
    tok = lambda rows: pl.BlockSpec((rows, TOKEN_BLOCK), lambda j: (0, j))
    return pl.pallas_call(
        _router_body,
        out_shape=[jax.ShapeDtypeStruct((TOP_K, n_tok), jnp.int32), jax.ShapeDtypeStruct((TOP_K, n_tok), F32),
                   jax.ShapeDtypeStruct((TOP_K, n_tok), jnp.int32), jax.ShapeDtypeStruct((N_EXPERTS, LANES), jnp.int32)],
        grid=(n_tok // TOKEN_BLOCK,),
        in_specs=[tok(N_EXPERTS), pl.BlockSpec((N_EXPERTS, 1), lambda j: (0, 0))],
        out_specs=[tok(TOP_K), tok(TOP_K), tok(TOP_K), pl.BlockSpec((N_EXPERTS, LANES), lambda j: (0, 0))],
        scratch_shapes=[pltpu.VMEM((N_EXPERTS, 1), F32)],
        compiler_params=_cparams(1),
        name="router",
    )(scores_t, router_b.reshape(N_EXPERTS, 1))


def _dispatch_plan(eid, rank, counts):
    n_tok = eid.shape[1]
    n_slots = n_tok * TOP_K
    padded = (counts + MOE_ROWS - 1) // MOE_ROWS * MOE_ROWS
    pad_end = jnp.cumsum(padded)
    pad_start = pad_end - padded
    raw_start = jnp.cumsum(counts) - counts
    dest = pad_start[eid] + rank
    n_blocks = -(-n_slots // MOE_ROWS) + N_EXPERTS
    blk_expert = jnp.minimum(
        jnp.searchsorted(pad_end, jnp.arange(n_blocks, dtype=jnp.int32) * MOE_ROWS, side='right'),
        N_EXPERTS - 1).astype(jnp.int32)
    nused = (pad_end[-1:] // MOE_ROWS).astype(jnp.int32)
    toks = jnp.tile(jnp.arange(n_tok, dtype=jnp.int32), TOP_K)
    _, tok_sorted = lax.sort_key_val(dest.reshape(-1), toks)
    row = jnp.arange(n_blocks * MOE_ROWS, dtype=jnp.int32)
    row_e = jnp.repeat(blk_expert, MOE_ROWS)
    r = row - pad_start[row_e]
    valid = r < counts[row_e]
    row_tok = jnp.where(valid, tok_sorted[jnp.clip(raw_start[row_e] + r, 0, n_slots - 1)], 0)
    return dest, blk_expert, nused, row_tok


def _final_body(rowmap_ref, x_ref, r_ref, gate_ref, nw_ref, o_ref):
    row = rowmap_ref[pl.program_id(0)]
    xn = x_ref[...] + gate_ref[pl.ds(row, 1), :] * r_ref[...]
    ms = jnp.mean(xn * xn, axis=-1, keepdims=True)
    o_ref[...] = xn * lax.rsqrt(ms + RMS_EPS) * nw_ref[...]


def _final(rowmap, x, r, gate_rows, nw):
    n_tok = x.shape[0]
    tok = pl.BlockSpec((TOKEN_BLOCK, D_MODEL), lambda j, rm: (j, 0))
    full = lambda shape: pl.BlockSpec(shape, lambda j, rm: (0,) * len(shape))
    return pl.pallas_call(
        _final_body,
        out_shape=jax.ShapeDtypeStruct((n_tok, D_MODEL), F32),
        grid_spec=pltpu.PrefetchScalarGridSpec(
            num_scalar_prefetch=1, grid=(n_tok // TOKEN_BLOCK,),
            in_specs=[tok, tok, full((SUBLANES, D_MODEL)), full((1, D_MODEL))],
            out_specs=tok),
        compiler_params=_cparams(1),
        name="final_norm",
    )(rowmap, x, r, gate_rows, nw)


def _grid_pos_embed(rows):
    r = jnp.repeat(jnp.arange(rows, dtype=F32), GRID_W)
    col = jnp.tile(jnp.arange(GRID_W, dtype=F32), rows)
    quarter = D_MODEL // 4
    omega = 1.0 / (10000.0 ** (jnp.arange(quarter, dtype=F32) / quarter))

    def axis_embed(pos):
        ang = pos[:, None] * omega[None, :]
        return jnp.concatenate([jnp.sin(ang), jnp.cos(ang)], axis=-1)

    return jnp.concatenate([axis_embed(r), axis_embed(col)], axis=-1)


def kernel(x_prompt, x_sample, state_s5_re, state_s5_im, state_hg, c, c_ctx, w_ada, b_ada, norm1_w, norm2_w, w_in, s5_lam_re, s5_lam_im, s5_log_step, s5_b_re, s5_b_im, s5_c_re, s5_c_im, s5_d, s5_w_glu, s5_b_glu, hg_lb_logits, hg_norm_w, w_br_s5, w_br_hg, w_out, router_w, router_b, moe_w_gate, moe_w_up, moe_w_down, norm_f_w):
    n_ctx, ctx_len, _ = x_prompt.shape
    n_lat, lat_len, _ = x_sample.shape
    n_ctx_tok = n_ctx * ctx_len
    n_seq = n_ctx + n_lat
    assert n_lat + 1 <= SUBLANES and ctx_len % TOKEN_BLOCK == 0 and lat_len % TOKEN_BLOCK == 0

    seq_np, first_np, last_np, modrow_np = _block_tables(n_ctx, ctx_len, n_lat, lat_len)
    nblk = seq_np.shape[0]
    jmap_f = np.arange(nblk, dtype=np.int32)
    jmap_b = jmap_f[::-1].copy()
    tab_f = tuple(jnp.asarray(t) for t in (jmap_f, seq_np, first_np, last_np))
    tab_b = tuple(jnp.asarray(t) for t in (jmap_b, seq_np[::-1].copy(), last_np[::-1].copy(), first_np[::-1].copy()))
    rowmap = jnp.asarray(modrow_np)

    cvecs = jnp.zeros((SUBLANES, D_MODEL), F32).at[0].set(c_ctx).at[1:1 + n_lat].set(c)
    mod_all = _adaln(cvecs, w_ada, b_ada)

    probs = jax.nn.softmax(hg_lb_logits.astype(F32), axis=0)
    lower_bounds = jnp.cumsum(probs, axis=0) - probs[0]

    pos = _grid_pos_embed(lat_len // GRID_W)
    x = jnp.concatenate([x_prompt.reshape(n_ctx_tok, D_MODEL), x_sample.reshape(n_lat * lat_len, D_MODEL)], axis=0)
    resid = jnp.concatenate([jnp.zeros((n_ctx_tok, D_MODEL), F32), jnp.tile(pos, (n_lat, 1))], axis=0)
    gate_rows = jnp.ones((SUBLANES, D_MODEL), F32)

    router_t = router_w.T.astype(F32)
    r_hi, r_lo = _split_bf16(router_t)
    ns = S5_BLOCK_STATES
    zeros_s5 = jnp.zeros((n_ctx, S5_COLBLOCKS, 1, ns), F32)
    zeros_hg = jnp.zeros((n_ctx, HG_HEADS, HG_DV, HG_DK), F32)

    new_s5_re, new_s5_im, new_hg = [], [], []
    for l in range(DEPTH):
        mod = mod_all[l]
        x, u, q_in, f_raw, i_in, g_out, gate_s5, gate_hg = _inproj(
            rowmap, x, resid, gate_rows, mod, norm1_w[l].reshape(1, D_MODEL), w_in[l].astype(BF16))

        ys, s5_fin_re, s5_fin_im, os_, hg_fin = [], [], [], [], []
        for d, tabs in enumerate((tab_f, tab_b)):
            bd, cd, lre, lim, lkre, lkim = _s5_params(
                s5_lam_re[l, d], s5_lam_im[l, d], s5_log_step[l, d], s5_b_re[l], s5_b_im[l], s5_c_re[l], s5_c_im[l])
            h0re = jnp.concatenate([zeros_s5, state_s5_re[:, l, d].astype(F32).reshape(n_lat, S5_COLBLOCKS, 1, ns)], axis=0)
            h0im = jnp.concatenate([zeros_s5, state_s5_im[:, l, d].astype(F32).reshape(n_lat, S5_COLBLOCKS, 1, ns)], axis=0)
            y_d, fre, fim = _s5_scan(bool(d), tabs, u, bd, cd, lre, lim, lkre, lkim, h0re, h0im)
            ys.append(y_d)
            s5_fin_re.append(fre[:n_ctx].reshape(n_ctx, S5_GROUPS, S5_STATE))
            s5_fin_im.append(fim[:n_ctx].reshape(n_ctx, S5_GROUPS, S5_STATE))

            h0 = jnp.concatenate([zeros_hg, jnp.swapaxes(state_hg[:, l, d].astype(F32), -1, -2)], axis=0)
            o_d, hfin = _hg_scan(bool(d), tabs, q_in, f_raw, i_in, lower_bounds[l, d].reshape(1, HG_KW), h0)
            os_.append(o_d)
            hg_fin.append(jnp.swapaxes(hfin[:n_ctx], -1, -2))
        new_s5_re.append(jnp.stack(s5_fin_re, axis=1))
        new_s5_im.append(jnp.stack(s5_fin_im, axis=1))
        new_hg.append(jnp.stack(hg_fin, axis=1))

        x, h2, scores_t = _merge(
            rowmap, x, ys[0], ys[1], u, os_[0], os_[1], g_out, gate_s5, gate_hg, mod,
            s5_d[l].reshape(1, S5_WIDTH), s5_b_glu[l].reshape(1, S5_WIDTH), hg_norm_w[l].reshape(1, HG_WIDTH),
            norm2_w[l].reshape(1, D_MODEL), s5_w_glu[l].astype(BF16), w_br_s5[l].astype(BF16),
            w_br_hg[l].astype(BF16), w_out[l].astype(BF16), r_hi, r_lo)

        eid, wts, rank, counts = _router(scores_t, router_b.astype(F32))
        dest, blk_expert, nused, row_tok = _dispatch_plan(eid, rank, counts[:, 0])
        yb = _experts(l, blk_expert, nused, h2[row_tok], moe_w_gate, moe_w_up, moe_w_down)
        resid = yb[dest[0]] * wts[0][:, None] + yb[dest[1]] * wts[1][:, None]
        gate_rows = mod[:, 5 * D_MODEL:6 * D_MODEL]

    y = _final(rowmap, x, resid, gate_rows, norm_f_w.reshape(1, D_MODEL))
    y_prompt = y[:n_ctx_tok].reshape(n_ctx, ctx_len, D_MODEL)
    y_sample = y[n_ctx_tok:].reshape(n_lat, lat_len, D_MODEL)
    new_state_s5_re = jnp.stack(new_s5_re, axis=1).astype(state_s5_re.dtype)
    new_state_s5_im = jnp.stack(new_s5_im, axis=1).astype(state_s5_im.dtype)
    new_state_hg = jnp.stack(new_hg, axis=1).astype(state_hg.dtype)
    return (y_prompt, y_sample, new_state_s5_re, new_state_s5_im, new_state_hg)
```

```python
import functools
import math

import jax
import jax.numpy as jnp
import numpy as np
from jax import lax
from jax.experimental import pallas as pl
from jax.experimental.pallas import tpu as pltpu

F32 = jnp.float32
BF16 = jnp.bfloat16

D_MODEL = 1024
DEPTH = 4
GRID_W = 64
RMS_EPS = 1e-6
S5_WIDTH = 512
S5_GROUP = 16
S5_GROUPS = 32
S5_STATE = 64
HG_HEADS = 4
HG_DK = 128
HG_DV = 128
HG_WIDTH = 512
HG_KW = 512
HG_CHUNK = 64
HG_SUB = 16
N_EXPERTS = 16
N_GROUPS = 4
EXPERTS_PER_GROUP = 4
TOP_K = 2
IN_COLS = 5120

LANES = 128
SUBLANES = 8
VMEM_LIMIT_BYTES = 56 * 1024 * 1024

TOKEN_BLOCK = 256
S5_SEG = TOKEN_BLOCK // SUBLANES
MXU_DEPTH = 256
S5_CHANNELS = MXU_DEPTH
S5_COLBLOCKS = S5_WIDTH // S5_CHANNELS
S5_BLOCK_STATES = (S5_CHANNELS // S5_GROUP) * S5_STATE
S5_PITCH = S5_SEG + 4
_S5_PERM = np.zeros((TOKEN_BLOCK, TOKEN_BLOCK), np.float32)
_S5_PERM[np.arange(TOKEN_BLOCK), (np.arange(TOKEN_BLOCK) % SUBLANES) * S5_SEG + np.arange(TOKEN_BLOCK) // SUBLANES] = 1.0
MOE_ROWS = 256
MOE_CHUNKS = 4


def _block_tables(n_ctx_seq, ctx_len, n_lat_seq, lat_len):
    seq, first, last, modrow = [], [], [], []
    for s in range(n_ctx_seq):
        nb = ctx_len // TOKEN_BLOCK
        for b in range(nb):
            seq.append(s); first.append(int(b == 0)); last.append(int(b == nb - 1)); modrow.append(0)
    for s in range(n_lat_seq):
        nb = lat_len // TOKEN_BLOCK
        for b in range(nb):
            seq.append(n_ctx_seq + s); first.append(int(b == 0)); last.append(int(b == nb - 1))
            modrow.append(1 + s)
    return (np.asarray(seq, np.int32), np.asarray(first, np.int32),
            np.asarray(last, np.int32), np.asarray(modrow, np.int32))


def _cparams(n_axes):
    return pltpu.CompilerParams(dimension_semantics=("arbitrary",) * n_axes,
                                vmem_limit_bytes=VMEM_LIMIT_BYTES)


def _split_bf16(a):
    hi = a.astype(BF16)
    lo = (a - hi.astype(F32)).astype(BF16)
    return hi, lo


def _dot_nt(a, b):
    return lax.dot_general(a, b, (((1,), (1,)), ((), ())), preferred_element_type=F32)


def _adaln_body(c_ref, w_ref, b_ref, o_ref):
    a = c_ref[...]
    a = a * jax.nn.sigmoid(a)
    a_hi, a_lo = _split_bf16(a)
    w_hi, w_lo = _split_bf16(w_ref[0])
    acc = jnp.dot(a_hi, w_hi, preferred_element_type=F32)
    acc += jnp.dot(a_lo, w_hi, preferred_element_type=F32)
    acc += jnp.dot(a_hi, w_lo, preferred_element_type=F32)
    o_ref[0] = acc + b_ref[0]


def _adaln(cvecs, w_ada, b_ada):
    tn = 1536
    n_tiles = (6 * D_MODEL) // tn
    return pl.pallas_call(
        _adaln_body,
        out_shape=jax.ShapeDtypeStruct((DEPTH, SUBLANES, 6 * D_MODEL), F32),
        grid=(DEPTH, n_tiles),
        in_specs=[
            pl.BlockSpec((SUBLANES, D_MODEL), lambda l, n: (0, 0)),
            pl.BlockSpec((1, D_MODEL, tn), lambda l, n: (l, 0, n)),
            pl.BlockSpec((1, 1, tn), lambda l, n: (l, 0, n)),
        ],
        out_specs=pl.BlockSpec((1, SUBLANES, tn), lambda l, n: (l, 0, n)),
        compiler_params=_cparams(2),
        name="adaln",
    )(cvecs, w_ada, b_ada.reshape(DEPTH, 1, 6 * D_MODEL))


_IN_SPLITS = ((0, 512), (512, 1024), (1024, 2048), (2048, 2560), (2560, 3072), (3072, 4096), (4096, 5120))


def _rms_modulate(x, nw, scale, shift):
    ms = jnp.mean(x * x, axis=-1, keepdims=True)
    return (x * lax.rsqrt(ms + RMS_EPS)) * nw * (1.0 + scale) + shift


def _inproj_body(rowmap_ref, x_ref, r_ref, gate_ref, mod_ref, nw_ref, w_ref,
                 xo_ref, u_ref, q_ref, f_ref, i_ref, g_ref, gs_ref, gh_ref):
    row = rowmap_ref[pl.program_id(0)]
    gate = gate_ref[pl.ds(row, 1), :]
    xn = x_ref[...] + gate * r_ref[...]
    xo_ref[...] = xn
    shift = mod_ref[pl.ds(row, 1), 0:D_MODEL]
    scale = mod_ref[pl.ds(row, 1), D_MODEL:2 * D_MODEL]
    h = _rms_modulate(xn, nw_ref[...], scale, shift).astype(BF16)
    outs = (u_ref, q_ref, f_ref, i_ref, g_ref, gs_ref, gh_ref)
    for ref, (a, b) in zip(outs, _IN_SPLITS):
        ref[...] = jnp.dot(h, w_ref[:, a:b], preferred_element_type=F32).astype(ref.dtype)


def _inproj(rowmap, x, r, gate_rows, mod, nw, w_bf16):
    n_tok = x.shape[0]
    nblk = n_tok // TOKEN_BLOCK
    tok = lambda w: pl.BlockSpec((TOKEN_BLOCK, w), lambda j, rm: (j, 0))
    full = lambda shape: pl.BlockSpec(shape, lambda j, rm: (0,) * len(shape))
    widths = [b - a for a, b in _IN_SPLITS]
    dtypes = [F32, F32, F32, BF16, F32, F32, F32]
    return pl.pallas_call(
        _inproj_body,
        out_shape=[jax.ShapeDtypeStruct((n_tok, D_MODEL), F32)]
        + [jax.ShapeDtypeStruct((n_tok, w), dt) for w, dt in zip(widths, dtypes)],
        grid_spec=pltpu.PrefetchScalarGridSpec(
            num_scalar_prefetch=1, grid=(nblk,),
            in_specs=[tok(D_MODEL), tok(D_MODEL), full((SUBLANES, D_MODEL)), full((SUBLANES, 6 * D_MODEL)),
                      full((1, D_MODEL)), full((D_MODEL, IN_COLS))],
            out_specs=[tok(D_MODEL)] + [tok(w) for w in widths]),
        compiler_params=_cparams(1),
        name="inproj",
    )(rowmap, x, r, gate_rows, mod, nw, w_bf16)


def _s5_body(rev, jmap_ref, seq_ref, first_ref, last_ref,
             u_ref, perm_ref, bd_ref, cd_ref, lre_ref, lim_ref, lkre_ref, lkim_ref, h0re_ref, h0im_ref,
             y_ref, fre_ref, fim_ref,
             bu_ref, sb_ref, yp_ref, cre_ref, cim_ref):
    p = pl.program_id(1)
    ns = S5_BLOCK_STATES

    @pl.when(first_ref[p] == 1)
    def _():
        cre_ref[...] = h0re_ref[0, 0]
        cim_ref[...] = h0im_ref[0, 0]

    up = jnp.dot(perm_ref[...], u_ref[...].astype(BF16), preferred_element_type=F32).astype(BF16)
    bu_ref[...] = jnp.dot(up, bd_ref[0], preferred_element_type=F32)

    lre = lre_ref[0]
    lim = lim_ref[0]
    order = list(range(S5_SEG))
    if rev:
        order = order[::-1]

    def step(sr, si, k):
        br = bu_ref[pl.ds(SUBLANES * k, SUBLANES), 0:ns]
        bi = bu_ref[pl.ds(SUBLANES * k, SUBLANES), ns:2 * ns]
        return lre * sr - lim * si + br, lre * si + lim * sr + bi

    sr = jnp.zeros((SUBLANES, ns), F32)
    si = jnp.zeros((SUBLANES, ns), F32)
    for k in order:
        sr, si = step(sr, si, k)

    cr = cre_ref[...]
    ci = cim_ref[...]
    lkre = lkre_ref[0]
    lkim = lkim_ref[0]
    start_r = [None] * SUBLANES
    start_i = [None] * SUBLANES
    seg_order = list(range(SUBLANES))
    if rev:
        seg_order = seg_order[::-1]
    for i in seg_order:
        start_r[i] = cr
        start_i[i] = ci
        er = sr[i:i + 1]
        ei = si[i:i + 1]
        cr, ci = lkre * cr - lkim * ci + er, lkre * ci + lkim * cr + ei
    cre_ref[...] = cr
    cim_ref[...] = ci
    fre_ref[0, 0] = cr
    fim_ref[0, 0] = ci

    sr = jnp.concatenate(start_r, axis=0)
    si = jnp.concatenate(start_i, axis=0)
    for n in range(0, S5_SEG, 2):
        k0, k1 = order[n], order[n + 1]
        sr0, si0 = step(sr, si, k0)
        sr, si = step(sr0, si0, k1)
        lo_r, hi_r = (sr0, sr) if k0 < k1 else (sr, sr0)
        lo_i, hi_i = (si0, si) if k0 < k1 else (si, si0)
        base = SUBLANES * min(k0, k1)
        sb_ref[pl.ds(base, 2 * SUBLANES), 0:ns] = jnp.concatenate([lo_r, hi_r], axis=0).astype(BF16)
        sb_ref[pl.ds(base, 2 * SUBLANES), ns:2 * ns] = jnp.concatenate([lo_i, hi_i], axis=0).astype(BF16)

    y = jnp.dot(sb_ref[...], cd_ref[0], preferred_element_type=F32)
    for h in range(S5_CHANNELS // LANES):
        for k in range(S5_SEG):
            yp_ref[h, pl.ds(k, SUBLANES, stride=S5_PITCH), :] = y[SUBLANES * k:SUBLANES * (k + 1), h * LANES:(h + 1) * LANES]
        for i in range(SUBLANES):
            y_ref[i * S5_SEG:(i + 1) * S5_SEG, h * LANES:(h + 1) * LANES] = yp_ref[h, pl.ds(i * S5_PITCH, S5_SEG), :]


def _s5_scan(rev, tables, u, bd, cd, lre, lim, lkre, lkim, h0re, h0im):
    jmap, seq, first, last = tables
    n_tok = u.shape[0]
    nblk = n_tok // TOKEN_BLOCK
    n_seq = h0re.shape[0]
    ns = S5_BLOCK_STATES
    grid = (S5_COLBLOCKS, nblk)
    col = lambda shape: pl.BlockSpec(shape, lambda a, p, jm, sq, fi, la: (a,) + (0,) * (len(shape) - 1))
    st = pl.BlockSpec((1, 1, 1, ns), lambda a, p, jm, sq, fi, la: (sq[p], a, 0, 0))
    return pl.pallas_call(
        functools.partial(_s5_body, rev),
        out_shape=[jax.ShapeDtypeStruct((n_tok, S5_WIDTH), F32),
                   jax.ShapeDtypeStruct((n_seq, S5_COLBLOCKS, 1, ns), F32),
                   jax.ShapeDtypeStruct((n_seq, S5_COLBLOCKS, 1, ns), F32)],
        grid_spec=pltpu.PrefetchScalarGridSpec(
            num_scalar_prefetch=4, grid=grid,
            in_specs=[
                pl.BlockSpec((TOKEN_BLOCK, S5_CHANNELS), lambda a, p, jm, sq, fi, la: (jm[p], a)),
                pl.BlockSpec((TOKEN_BLOCK, TOKEN_BLOCK), lambda a, p, jm, sq, fi, la: (0, 0)),
                col((1, S5_CHANNELS, 2 * ns)), col((1, 2 * ns, S5_CHANNELS)),
                col((1, SUBLANES, ns)), col((1, SUBLANES, ns)), col((1, 1, ns)), col((1, 1, ns)),
                st, st],
            out_specs=[pl.BlockSpec((TOKEN_BLOCK, S5_CHANNELS), lambda a, p, jm, sq, fi, la: (jm[p], a)), st, st],
            scratch_shapes=[
                            pltpu.VMEM((TOKEN_BLOCK, 2 * ns), F32),
                            pltpu.VMEM((TOKEN_BLOCK, 2 * ns), BF16),
                            pltpu.VMEM((S5_CHANNELS // LANES, SUBLANES * S5_PITCH, LANES), F32),
                            pltpu.VMEM((1, ns), F32),
                            pltpu.VMEM((1, ns), F32)]),
        compiler_params=_cparams(2),
        name="s5_bwd" if rev else "s5_fwd",
    )(jmap, seq, first, last, u, jnp.asarray(_S5_PERM, BF16), bd, cd, lre, lim, lkre, lkim, h0re, h0im)


def _s5_params(lam_re, lam_im, log_step, b_re, b_im, c_re, c_im):
    g, p, c = S5_GROUPS, S5_STATE, S5_GROUP
    gl = S5_CHANNELS // c
    lr = jnp.minimum(lam_re, -1e-4)
    li = lam_im
    step = jnp.exp(log_step)[:, None]
    mag = jnp.exp(lr * step)
    ar, ai = mag * jnp.cos(li * step), mag * jnp.sin(li * step)
    magk = jnp.exp(lr * step * S5_SEG)
    akr, aki = magk * jnp.cos(li * step * S5_SEG), magk * jnp.sin(li * step * S5_SEG)
    den = lr * lr + li * li
    cr = ((ar - 1.0) * lr + ai * li) / den
    ci = (ai * lr - (ar - 1.0) * li) / den
    bbr = cr[:, :, None] * b_re - ci[:, :, None] * b_im
    bbi = cr[:, :, None] * b_im + ci[:, :, None] * b_re
    eye = jnp.eye(gl, dtype=F32)

    def bmat(bb):
        bb = bb.reshape(S5_COLBLOCKS, gl, p, c)
        return jnp.einsum('agpc,gh->agchp', bb, eye).reshape(S5_COLBLOCKS, gl * c, gl * p)

    def cmat(cc):
        cc = cc.reshape(S5_COLBLOCKS, gl, c, p)
        return jnp.einsum('agcp,gh->agphc', cc, eye).reshape(S5_COLBLOCKS, gl * p, gl * c)

    bd = jnp.concatenate([bmat(bbr), bmat(bbi)], axis=2).astype(BF16)
    cd = jnp.concatenate([cmat(c_re), -cmat(c_im)], axis=1).astype(BF16)

    def lanes(v, rows):
        v = v.reshape(S5_COLBLOCKS, 1, gl * p)
        return jnp.broadcast_to(v, (S5_COLBLOCKS, rows, gl * p))

    return bd, cd, lanes(ar, SUBLANES), lanes(ai, SUBLANES), lanes(akr, 1), lanes(aki, 1)


def _hg_body(rev, jmap_ref, seq_ref, first_ref, last_ref,
             q_ref, f_ref, v_ref, lb_ref, h0_ref, o_ref, s_ref):
    p = pl.program_id(0)
    c_len = HG_CHUNK

    @pl.when(first_ref[p] == 1)
    def _():
        s_ref[...] = h0_ref[...]

    sub = HG_SUB
    n_sub = c_len // sub
    ri = lax.broadcasted_iota(jnp.int32, (c_len, c_len), 0)
    ci = lax.broadcasted_iota(jnp.int32, (c_len, c_len), 1)
    mask = (ri <= ci) if rev else (ri >= ci)
    tri = mask.astype(BF16)
    rb, cb = ri // sub, ci // sub
    diag_mask = mask & (rb == cb)
    off_mask = (cb > rb) if rev else (cb < rb)
    lb = lb_ref[...]
    tot_row = 0 if rev else c_len - 1
    n_chunks = TOKEN_BLOCK // c_len
    chunk_order = range(n_chunks - 1, -1, -1) if rev else range(n_chunks)
    for c in chunk_order:
        rows = pl.ds(c * c_len, c_len)
        qin = q_ref[rows, :]
        q = qin * jax.nn.sigmoid(qin) * (HG_DK ** -0.5)
        f = lb + (1.0 - lb) * jax.nn.sigmoid(f_ref[rows, :])
        k = 1.0 - f
        g = jnp.log(f)
        g0 = g.astype(BF16)
        r1 = g - g0.astype(F32)
        g1 = r1.astype(BF16)
        g2 = (r1 - g1.astype(F32)).astype(BF16)
        b = (jnp.dot(tri, g0, preferred_element_type=F32) + jnp.dot(tri, g1, preferred_element_type=F32)
             + jnp.dot(tri, g2, preferred_element_type=F32))
        btot = b[tot_row:tot_row + 1, :]
        bmid = jnp.concatenate(
            [jnp.broadcast_to(b[i * sub + sub // 2:i * sub + sub // 2 + 1, :], (sub, HG_KW)) for i in range(n_sub)], axis=0)
        qd = (q * jnp.exp(b - bmid)).astype(BF16)
        kd = (k * jnp.exp(bmid - b)).astype(BF16)
        qo, ko = {}, {}
        for i in range(n_sub):
            edge = (i + 1) * sub if rev else i * sub - 1
            if 0 <= edge < c_len:
                bref = b[edge:edge + 1, :]
                rws = slice(i * sub, (i + 1) * sub)
                qo[i] = (q[rws] * jnp.exp(b[rws] - bref)).astype(BF16)
                ko[i] = (k * jnp.exp(jnp.minimum(bref - b, 0.0))).astype(BF16)
        qs = (q * jnp.exp(b)).astype(BF16)
        ks = (k * jnp.exp(btot - b)).astype(BF16)
        dec = jnp.exp(btot)
        v = v_ref[rows, :]
        vt = v.astype(F32).T.astype(BF16)
        outs = []
        for h in range(HG_HEADS):
            sl = slice(h * HG_DK, (h + 1) * HG_DK)
            near = _dot_nt(qd[:, sl], kd[:, sl])
            far = jnp.concatenate(
                [_dot_nt(qo[i][:, sl], ko[i][:, sl]) if i in qo else jnp.zeros((sub, c_len), F32)
                 for i in range(n_sub)], axis=0)
            att = jnp.where(diag_mask, near, jnp.where(off_mask, far, 0.0)).astype(BF16)
            s_old = s_ref[0, h]
            o_h = jnp.dot(att, v[:, sl], preferred_element_type=F32) + _dot_nt(qs[:, sl], s_old.astype(BF16))
            s_ref[0, h] = s_old * dec[:, sl] + jnp.dot(vt[sl, :], ks[:, sl], preferred_element_type=F32)
            outs.append(o_h)
        o_ref[rows, :] = jnp.concatenate(outs, axis=1)


def _hg_scan(rev, tables, q, f, v, lb, h0):
    jmap, seq, first, last = tables
    n_tok = q.shape[0]
    nblk = n_tok // TOKEN_BLOCK
    n_seq = h0.shape[0]
    d = 1 if rev else 0
    tok = pl.BlockSpec((TOKEN_BLOCK, HG_WIDTH), lambda p, jm, sq, fi, la: (jm[p], 0))
    st = pl.BlockSpec((1, HG_HEADS, HG_DV, HG_DK), lambda p, jm, sq, fi, la: (sq[p], 0, 0, 0))
    return pl.pallas_call(
        functools.partial(_hg_body, rev),
        out_shape=[jax.ShapeDtypeStruct((n_tok, HG_WIDTH), F32),
                   jax.ShapeDtypeStruct((n_seq, HG_HEADS, HG_DV, HG_DK), F32)],
        grid_spec=pltpu.PrefetchScalarGridSpec(
            num_scalar_prefetch=4, grid=(nblk,),
            in_specs=[tok,
                      pl.BlockSpec((TOKEN_BLOCK, HG_KW), lambda p, jm, sq, fi, la: (jm[p], d)),
                      tok,
                      pl.BlockSpec((1, HG_KW), lambda p, jm, sq, fi, la: (0, 0)),
                      st],
            out_specs=[tok, st]),
        compiler_params=_cparams(1),
        name="hg_bwd" if rev else "hg_fwd",
    )(jmap, seq, first, last, q, f, v, lb, h0)


def _merge_body(rowmap_ref, x_ref, ysf_ref, ysb_ref, u_ref, ohf_ref, ohb_ref, g_ref, gs_ref, gh_ref,
                mod_ref, d_ref, bglu_ref, hnw_ref, n2w_ref, wglu_ref, wbs_ref, wbh_ref, wout_ref,
                rhi_ref, rlo_ref, xm_ref, h2_ref, sc_ref):
    row = rowmap_ref[pl.program_id(0)]
    mrow = lambda n: mod_ref[pl.ds(row, 1), n * D_MODEL:(n + 1) * D_MODEL]
    gate1, shift2, scale2 = mrow(2), mrow(3), mrow(4)

    y = ysf_ref[...] + ysb_ref[...] + d_ref[...] * u_ref[...]
    gl = jax.nn.gelu(y)
    z = jnp.dot(gl.astype(BF16), wglu_ref[...], preferred_element_type=F32) + bglu_ref[...]
    y_s5 = gl * jax.nn.sigmoid(z)

    o = ohf_ref[...] + ohb_ref[...]
    normed = []
    for h in range(HG_HEADS):
        oh = o[:, h * HG_DV:(h + 1) * HG_DV]
        normed.append(oh * lax.rsqrt(jnp.mean(oh * oh, axis=-1, keepdims=True) + RMS_EPS))
    gout = g_ref[...]
    y_hg = jnp.concatenate(normed, axis=1) * hnw_ref[...] * (gout * jax.nn.sigmoid(gout))

    merged = (jax.nn.sigmoid(gs_ref[...]) * jnp.dot(y_s5.astype(BF16), wbs_ref[...], preferred_element_type=F32)
              + jax.nn.sigmoid(gh_ref[...]) * jnp.dot(y_hg.astype(BF16), wbh_ref[...], preferred_element_type=F32))
    xm = x_ref[...] + gate1 * jnp.dot(merged.astype(BF16), wout_ref[...], preferred_element_type=F32)
    xm_ref[...] = xm
    h2 = _rms_modulate(xm, n2w_ref[...], scale2, shift2)
    h2_ref[...] = h2
    h_hi, h_lo = _split_bf16(h2)
    logits = _dot_nt(rhi_ref[...], h_hi) + _dot_nt(rhi_ref[...], h_lo) + _dot_nt(rlo_ref[...], h_hi)
    sc_ref[...] = jax.nn.sigmoid(logits)


def _merge(rowmap, x, ysf, ysb, u, ohf, ohb, g, gs, gh, mod, d, bglu, hnw, n2w,
           wglu, wbs, wbh, wout, rhi, rlo):
    n_tok = x.shape[0]
    nblk = n_tok // TOKEN_BLOCK
    tok = lambda w: pl.BlockSpec((TOKEN_BLOCK, w), lambda j, rm: (j, 0))
    full = lambda shape: pl.BlockSpec(shape, lambda j, rm: (0,) * len(shape))
    return pl.pallas_call(
        _merge_body,
        out_shape=[jax.ShapeDtypeStruct((n_tok, D_MODEL), F32),
                   jax.ShapeDtypeStruct((n_tok, D_MODEL), F32),
                   jax.ShapeDtypeStruct((N_EXPERTS, n_tok), F32)],
        grid_spec=pltpu.PrefetchScalarGridSpec(
            num_scalar_prefetch=1, grid=(nblk,),
            in_specs=[tok(D_MODEL), tok(S5_WIDTH), tok(S5_WIDTH), tok(S5_WIDTH), tok(HG_WIDTH), tok(HG_WIDTH),
                      tok(HG_WIDTH), tok(D_MODEL), tok(D_MODEL),
                      full((SUBLANES, 6 * D_MODEL)), full((1, S5_WIDTH)), full((1, S5_WIDTH)),
                      full((1, HG_WIDTH)), full((1, D_MODEL)),
                      full((S5_WIDTH, S5_WIDTH)), full((S5_WIDTH, D_MODEL)), full((HG_WIDTH, D_MODEL)),
                      full((D_MODEL, D_MODEL)), full((N_EXPERTS, D_MODEL)), full((N_EXPERTS, D_MODEL))],
            out_specs=[tok(D_MODEL), tok(D_MODEL),
                       pl.BlockSpec((N_EXPERTS, TOKEN_BLOCK), lambda j, rm: (0, j))]),
        compiler_params=_cparams(1),
        name="merge",
    )(rowmap, x, ysf, ysb, u, ohf, ohb, g, gs, gh, mod, d, bglu, hnw, n2w, wglu, wbs, wbh, wout, rhi, rlo)


def _experts_body(has_prev, be_ref, nused_ref, x_ref, wg_ref, wu_ref, wd_ref, *rest):
    y_ref, wgb_ref, wub_ref, wdb_ref = rest[1:] if has_prev else rest
    b = pl.program_id(0)
    e = be_ref[b]
    prev = be_ref[jnp.maximum(b - 1, 0)]

    @pl.when((b == 0) | (e != prev))
    def _():
        wgb_ref[...] = wg_ref[0, 0].astype(BF16)
        wub_ref[...] = wu_ref[0, 0].astype(BF16)
        wdb_ref[...] = wd_ref[0, 0].astype(BF16)

    @pl.when(b < nused_ref[0])
    def _():
        x = x_ref[...].astype(BF16)
        gate = jnp.dot(x, wgb_ref[...], preferred_element_type=F32)
        up = jnp.dot(x, wub_ref[...], preferred_element_type=F32)
        hid = (gate * jax.nn.sigmoid(gate) * up).astype(BF16)
        y_ref[...] = jnp.dot(hid, wdb_ref[...], preferred_element_type=F32)

    @pl.when(b >= nused_ref[0])
    def _():
        y_ref[...] = jnp.zeros_like(y_ref)


def _experts(layer, first_blk, n_rows_total, blk_expert, nused, xb, w_gate, w_up, w_down, y_prev):
    nblk = xb.shape[0] // MOE_ROWS
    wspec = pl.BlockSpec((1, 1, D_MODEL, D_MODEL), lambda b, be, nu: (layer, be[b], 0, 0))
    has_prev = y_prev is not None
    in_specs = [pl.BlockSpec((MOE_ROWS, D_MODEL), lambda b, be, nu: (b, 0)), wspec, wspec, wspec]
    args = [blk_expert, nused, xb, w_gate, w_up, w_down]
    if has_prev:
        in_specs.append(pl.BlockSpec(memory_space=pl.ANY))
        args.append(y_prev)
    return pl.pallas_call(
        functools.partial(_experts_body, has_prev),
        out_shape=jax.ShapeDtypeStruct((n_rows_total, D_MODEL), F32),
        grid_spec=pltpu.PrefetchScalarGridSpec(
            num_scalar_prefetch=2, grid=(nblk,),
            in_specs=in_specs,
            out_specs=pl.BlockSpec((MOE_ROWS, D_MODEL), lambda b, be, nu: (first_blk + b, 0)),
            scratch_shapes=[pltpu.VMEM((D_MODEL, D_MODEL), BF16)] * 3),
        input_output_aliases={len(args) - 1: 0} if has_prev else {},
        compiler_params=_cparams(1),
        name="experts",
    )(*args)


def _first_max(vals):
    best, idx = vals[0], jnp.zeros(vals[0].shape, jnp.int32)
    for j in range(1, len(vals)):
        upd = vals[j] > best
        idx = jnp.where(upd, j, idx)
        best = jnp.where(upd, vals[j], best)
    return best, idx


def _pick(idx, vals):
    out = vals[-1]
    for j in range(len(vals) - 2, -1, -1):
        out = jnp.where(idx == j, vals[j], out)
    return out


def _router_body(sc_ref, rb_ref, eid_ref, wts_ref, rank_ref, cnt_ref, base_ref):
    j = pl.program_id(0)

    @pl.when(j == 0)
    def _():
        base_ref[...] = jnp.zeros_like(base_ref)

    s = sc_ref[...]
    sel = s + rb_ref[...]
    srow = [s[e:e + 1, :] for e in range(N_EXPERTS)]
    brow = [sel[e:e + 1, :] for e in range(N_EXPERTS)]
    epg = EXPERTS_PER_GROUP
    gscore = []
    for g in range(N_GROUPS):
        x = brow[g * epg:(g + 1) * epg]
        pairs = [x[a] + x[b] for a in range(epg) for b in range(a + 1, epg)]
        gscore.append(functools.reduce(jnp.maximum, pairs))
    _, bestg = _first_max(gscore)
    y = [_pick(bestg, [brow[g * epg + i] for g in range(N_GROUPS)]) for i in range(epg)]
    sy = [_pick(bestg, [srow[g * epg + i] for g in range(N_GROUPS)]) for i in range(epg)]
    _, i1 = _first_max(y)
    v2 = jnp.full(y[0].shape, -jnp.inf, F32)
    i2 = jnp.zeros(y[0].shape, jnp.int32)
    for i in range(epg):
        upd = (i1 != i) & (y[i] > v2)
        i2 = jnp.where(upd, i, i2)
        v2 = jnp.where(upd, y[i], v2)
    e0 = bestg * epg + i1
    e1 = bestg * epg + i2
    w0 = _pick(i1, sy)
    w1 = _pick(i2, sy)
    wsum = w0 + w1
    eid_ref[0:1, :] = e0
    eid_ref[1:2, :] = e1
    wts_ref[0:1, :] = w0 / wsum
    wts_ref[1:2, :] = w1 / wsum

    eiota = lax.broadcasted_iota(jnp.int32, s.shape, 0)
    m0 = eiota == e0
    m1 = eiota == e1
    member = (m0 | m1).astype(BF16)
    tb = s.shape[1]
    before = (lax.broadcasted_iota(jnp.int32, (tb, tb), 0) < lax.broadcasted_iota(jnp.int32, (tb, tb), 1)).astype(BF16)
    pos = base_ref[...] + jnp.dot(member, before, preferred_element_type=F32)
    rank_ref[0:1, :] = jnp.sum(jnp.where(m0, pos, 0.0), axis=0, keepdims=True).astype(jnp.int32)
    rank_ref[1:2, :] = jnp.sum(jnp.where(m1, pos, 0.0), axis=0, keepdims=True).astype(jnp.int32)
    base = base_ref[...] + jnp.sum(member.astype(F32), axis=1, keepdims=True)
    base_ref[...] = base
    cnt_ref[...] = jnp.broadcast_to(base, cnt_ref.shape).astype(jnp.int32)


def _router(scores_t, router_b):
    n_tok = scores_t.shape[1]
    tok = lambda rows: pl.BlockSpec((rows, TOKEN_BLOCK), lambda j: (0, j))
    return pl.pallas_call(
        _router_body,
        out_shape=[jax.ShapeDtypeStruct((TOP_K, n_tok), jnp.int32), jax.ShapeDtypeStruct((TOP_K, n_tok), F32),
                   jax.ShapeDtypeStruct((TOP_K, n_tok), jnp.int32), jax.ShapeDtypeStruct((N_EXPERTS, LANES), jnp.int32)],
        grid=(n_tok // TOKEN_BLOCK,),
        in_specs=[tok(N_EXPERTS), pl.BlockSpec((N_EXPERTS, 1), lambda j: (0, 0))],
        out_specs=[tok(TOP_K), tok(TOP_K), tok(TOP_K), pl.BlockSpec((N_EXPERTS, LANES), lambda j: (0, 0))],
        scratch_shapes=[pltpu.VMEM((N_EXPERTS, 1), F32)],
        compiler_params=_cparams(1),
        name="router",
    )(scores_t, router_b.reshape(N_EXPERTS, 1))


def _dispatch_plan(eid, rank, counts):
    n_tok = eid.shape[1]
    n_slots = n_tok * TOP_K
    padded = (counts + MOE_ROWS - 1) // MOE_ROWS * MOE_ROWS
    pad_end = jnp.cumsum(padded)
    pad_start = pad_end - padded
    e_ids = jnp.arange(N_EXPERTS, dtype=jnp.int32)
    dest = jnp.sum(jnp.where(eid[..., None] == e_ids, pad_start, 0), axis=-1) + rank
    n_blocks = -(-n_slots // MOE_ROWS) + N_EXPERTS
    n_rows = n_blocks * MOE_ROWS
    blk_start = jnp.arange(n_blocks, dtype=jnp.int32) * MOE_ROWS
    blk_expert = jnp.minimum(jnp.sum(pad_end[None, :] <= blk_start[:, None], axis=-1), N_EXPERTS - 1).astype(jnp.int32)
    nused = (pad_end[-1:] // MOE_ROWS).astype(jnp.int32)
    n_fill = n_rows - n_slots
    gap = padded - counts
    gap_before = jnp.cumsum(gap) - gap
    j = jnp.arange(n_fill, dtype=jnp.int32)
    in_gap = (j[:, None] >= gap_before[None, :]) & (j[:, None] < (gap_before + gap)[None, :])
    fill_pos = jnp.where(jnp.any(in_gap, axis=-1),
                         jnp.sum(jnp.where(in_gap, (pad_start + counts - gap_before)[None, :], 0), axis=-1) + j,
                         pad_end[-1] - jnp.sum(gap) + j)
    keys = jnp.concatenate([dest.reshape(-1), fill_pos])
    vals = jnp.concatenate([jnp.tile(jnp.arange(n_tok, dtype=jnp.int32), TOP_K), j % n_tok])
    _, row_tok = lax.sort_key_val(keys, vals)
    return dest, blk_expert, nused, row_tok


def _final_body(rowmap_ref, x_ref, r_ref, gate_ref, nw_ref, o_ref):
    row = rowmap_ref[pl.program_id(0)]
    xn = x_ref[...] + gate_ref[pl.ds(row, 1), :] * r_ref[...]
    ms = jnp.mean(xn * xn, axis=-1, keepdims=True)
    o_ref[...] = xn * lax.rsqrt(ms + RMS_EPS) * nw_ref[...]


def _final(rowmap, x, r, gate_rows, nw):
    n_tok = x.shape[0]
    tok = pl.BlockSpec((TOKEN_BLOCK, D_MODEL), lambda j, rm: (j, 0))
    full = lambda shape: pl.BlockSpec(shape, lambda j, rm: (0,) * len(shape))
    return pl.pallas_call(
        _final_body,
        out_shape=jax.ShapeDtypeStruct((n_tok, D_MODEL), F32),
        grid_spec=pltpu.PrefetchScalarGridSpec(
            num_scalar_prefetch=1, grid=(n_tok // TOKEN_BLOCK,),
            in_specs=[tok, tok, full((SUBLANES, D_MODEL)), full((1, D_MODEL))],
            out_specs=tok),
        compiler_params=_cparams(1),
        name="final_norm",
    )(rowmap, x, r, gate_rows, nw)


def _grid_pos_embed(rows):
    r = jnp.repeat(jnp.arange(rows, dtype=F32), GRID_W)
    col = jnp.tile(jnp.arange(GRID_W, dtype=F32), rows)
    quarter = D_MODEL // 4
    omega = 1.0 / (10000.0 ** (jnp.arange(quarter, dtype=F32) / quarter))

    def axis_embed(pos):
        ang = pos[:, None] * omega[None, :]
        return jnp.concatenate([jnp.sin(ang), jnp.cos(ang)], axis=-1)

    return jnp.concatenate([axis_embed(r), axis_embed(col)], axis=-1)


def kernel(x_prompt, x_sample, state_s5_re, state_s5_im, state_hg, c, c_ctx, w_ada, b_ada, norm1_w, norm2_w, w_in, s5_lam_re, s5_lam_im, s5_log_step, s5_b_re, s5_b_im, s5_c_re, s5_c_im, s5_d, s5_w_glu, s5_b_glu, hg_lb_logits, hg_norm_w, w_br_s5, w_br_hg, w_out, router_w, router_b, moe_w_gate, moe_w_up, moe_w_down, norm_f_w):
    n_ctx, ctx_len, _ = x_prompt.shape
    n_lat, lat_len, _ = x_sample.shape
    n_ctx_tok = n_ctx * ctx_len
    n_seq = n_ctx + n_lat
    assert n_lat + 1 <= SUBLANES and ctx_len % TOKEN_BLOCK == 0 and lat_len % TOKEN_BLOCK == 0

    seq_np, first_np, last_np, modrow_np = _block_tables(n_ctx, ctx_len, n_lat, lat_len)
    nblk = seq_np.shape[0]
    jmap_f = np.arange(nblk, dtype=np.int32)
    jmap_b = jmap_f[::-1].copy()
    tab_f = tuple(jnp.asarray(t) for t in (jmap_f, seq_np, first_np, last_np))
    tab_b = tuple(jnp.asarray(t) for t in (jmap_b, seq_np[::-1].copy(), last_np[::-1].copy(), first_np[::-1].copy()))
    rowmap = jnp.asarray(modrow_np)

    cvecs = jnp.zeros((SUBLANES, D_MODEL), F32).at[0].set(c_ctx).at[1:1 + n_lat].set(c)
    mod_all = _adaln(cvecs, w_ada, b_ada)

    probs = jax.nn.softmax(hg_lb_logits.astype(F32), axis=0)
    lower_bounds = jnp.cumsum(probs, axis=0) - probs[0]

    pos = _grid_pos_embed(lat_len // GRID_W)
    x = jnp.concatenate([x_prompt.reshape(n_ctx_tok, D_MODEL), x_sample.reshape(n_lat * lat_len, D_MODEL)], axis=0)
    resid = jnp.concatenate([jnp.zeros((n_ctx_tok, D_MODEL), F32), jnp.tile(pos, (n_lat, 1))], axis=0)
    gate_rows = jnp.ones((SUBLANES, D_MODEL), F32)

    router_t = router_w.T.astype(F32)
    r_hi, r_lo = _split_bf16(router_t)
    ns = S5_BLOCK_STATES
    zeros_s5 = jnp.zeros((n_ctx, S5_COLBLOCKS, 1, ns), F32)
    zeros_hg = jnp.zeros((n_ctx, HG_HEADS, HG_DV, HG_DK), F32)

    new_s5_re, new_s5_im, new_hg = [], [], []
    for l in range(DEPTH):
        mod = mod_all[l]
        x, u, q_in, f_raw, i_in, g_out, gate_s5, gate_hg = _inproj(
            rowmap, x, resid, gate_rows, mod, norm1_w[l].reshape(1, D_MODEL), w_in[l].astype(BF16))

        ys, s5_fin_re, s5_fin_im, os_, hg_fin = [], [], [], [], []
        for d, tabs in enumerate((tab_f, tab_b)):
            bd, cd, lre, lim, lkre, lkim = _s5_params(
                s5_lam_re[l, d], s5_lam_im[l, d], s5_log_step[l, d], s5_b_re[l], s5_b_im[l], s5_c_re[l], s5_c_im[l])
            h0re = jnp.concatenate([zeros_s5, state_s5_re[:, l, d].astype(F32).reshape(n_lat, S5_COLBLOCKS, 1, ns)], axis=0)
            h0im = jnp.concatenate([zeros_s5, state_s5_im[:, l, d].astype(F32).reshape(n_lat, S5_COLBLOCKS, 1, ns)], axis=0)
            y_d, fre, fim = _s5_scan(bool(d), tabs, u, bd, cd, lre, lim, lkre, lkim, h0re, h0im)
            ys.append(y_d)
            s5_fin_re.append(fre[:n_ctx].reshape(n_ctx, S5_GROUPS, S5_STATE))
            s5_fin_im.append(fim[:n_ctx].reshape(n_ctx, S5_GROUPS, S5_STATE))

            h0 = jnp.concatenate([zeros_hg, jnp.swapaxes(state_hg[:, l, d].astype(F32), -1, -2)], axis=0)
            o_d, hfin = _hg_scan(bool(d), tabs, q_in, f_raw, i_in, lower_bounds[l, d].reshape(1, HG_KW), h0)
            os_.append(o_d)
            hg_fin.append(jnp.swapaxes(hfin[:n_ctx], -1, -2))
        new_s5_re.append(jnp.stack(s5_fin_re, axis=1))
        new_s5_im.append(jnp.stack(s5_fin_im, axis=1))
        new_hg.append(jnp.stack(hg_fin, axis=1))

        x, h2, scores_t = _merge(
            rowmap, x, ys[0], ys[1], u, os_[0], os_[1], g_out, gate_s5, gate_hg, mod,
            s5_d[l].reshape(1, S5_WIDTH), s5_b_glu[l].reshape(1, S5_WIDTH), hg_norm_w[l].reshape(1, HG_WIDTH),
            norm2_w[l].reshape(1, D_MODEL), s5_w_glu[l].astype(BF16), w_br_s5[l].astype(BF16),
            w_br_hg[l].astype(BF16), w_out[l].astype(BF16), r_hi, r_lo)

        eid, wts, rank, counts = _router(scores_t, router_b.astype(F32))
        dest, blk_expert, nused, row_tok = _dispatch_plan(eid, rank, counts[:, 0])
        n_blocks = blk_expert.shape[0]
        cb = n_blocks // MOE_CHUNKS
        yb = None
        for ch in range(MOE_CHUNKS):
            xb = h2[row_tok[ch * cb * MOE_ROWS:(ch + 1) * cb * MOE_ROWS]]
            yb = _experts(l, ch * cb, n_blocks * MOE_ROWS, blk_expert[ch * cb:(ch + 1) * cb],
                          jnp.clip(nused - ch * cb, 0, cb), xb, moe_w_gate, moe_w_up, moe_w_down, yb)
        resid = yb[dest[0]] * wts[0][:, None] + yb[dest[1]] * wts[1][:, None]
        gate_rows = mod[:, 5 * D_MODEL:6 * D_MODEL]

    y = _final(rowmap, x, resid, gate_rows, norm_f_w.reshape(1, D_MODEL))
    y_prompt = y[:n_ctx_tok].reshape(n_ctx, ctx_len, D_MODEL)
    y_sample = y[n_ctx_tok:].reshape(n_lat, lat_len, D_MODEL)
    new_state_s5_re = jnp.stack(new_s5_re, axis=1).astype(state_s5_re.dtype)
    new_state_s5_im = jnp.stack(new_s5_im, axis=1).astype(state_s5_im.dtype)
    new_state_hg = jnp.stack(new_hg, axis=1).astype(state_hg.dtype)
    return (y_prompt, y_sample, new_state_s5_re, new_state_s5_im, new_state_hg)
```

```python
import functools
import math

import jax
import jax.numpy as jnp
import numpy as np
from jax import lax
from jax.experimental import pallas as pl
from jax.experimental.pallas import tpu as pltpu

F32 = jnp.float32
BF16 = jnp.bfloat16

D_MODEL = 1024
DEPTH = 4
GRID_W = 64
RMS_EPS = 1e-6
S5_WIDTH = 512
S5_GROUP = 16
S5_GROUPS = 32
S5_STATE = 64
HG_HEADS = 4
HG_DK = 128
HG_DV = 128
HG_WIDTH = 512
HG_KW = 512
HG_CHUNK = 64
HG_SUB = 16
N_EXPERTS = 16
N_GROUPS = 4
EXPERTS_PER_GROUP = 4
TOP_K = 2
IN_COLS = 5120

LANES = 128
SUBLANES = 8
VMEM_LIMIT_BYTES = 56 * 1024 * 1024

TOKEN_BLOCK = 256
S5_SEG = TOKEN_BLOCK // SUBLANES
MXU_DEPTH = 256
S5_CHANNELS = MXU_DEPTH
S5_COLBLOCKS = S5_WIDTH // S5_CHANNELS
S5_BLOCK_STATES = (S5_CHANNELS // S5_GROUP) * S5_STATE
S5_PITCH = S5_SEG + 4
_S5_PERM = np.zeros((TOKEN_BLOCK, TOKEN_BLOCK), np.float32)
_S5_PERM[np.arange(TOKEN_BLOCK), (np.arange(TOKEN_BLOCK) % SUBLANES) * S5_SEG + np.arange(TOKEN_BLOCK) // SUBLANES] = 1.0
MOE_ROWS = 256
MOE_CHUNKS = 4


def _block_tables(n_ctx_seq, ctx_len, n_lat_seq, lat_len):
    seq, first, last, modrow = [], [], [], []
    for s in range(n_ctx_seq):
        nb = ctx_len // TOKEN_BLOCK
        for b in range(nb):
            seq.append(s); first.append(int(b == 0)); last.append(int(b == nb - 1)); modrow.append(0)
    for s in range(n_lat_seq):
        nb = lat_len // TOKEN_BLOCK
        for b in range(nb):
            seq.append(n_ctx_seq + s); first.append(int(b == 0)); last.append(int(b == nb - 1))
            modrow.append(1 + s)
    return (np.asarray(seq, np.int32), np.asarray(first, np.int32),
            np.asarray(last, np.int32), np.asarray(modrow, np.int32))


def _cparams(n_axes):
    return pltpu.CompilerParams(dimension_semantics=("arbitrary",) * n_axes,
                                vmem_limit_bytes=VMEM_LIMIT_BYTES)


def _split_bf16(a):
    hi = a.astype(BF16)
    lo = (a - hi.astype(F32)).astype(BF16)
    return hi, lo


def _dot_nt(a, b):
    return lax.dot_general(a, b, (((1,), (1,)), ((), ())), preferred_element_type=F32)


def _adaln_body(c_ref, w_ref, b_ref, o_ref):
    a = c_ref[...]
    a = a * jax.nn.sigmoid(a)
    a_hi, a_lo = _split_bf16(a)
    w_hi, w_lo = _split_bf16(w_ref[0])
    acc = jnp.dot(a_hi, w_hi, preferred_element_type=F32)
    acc += jnp.dot(a_lo, w_hi, preferred_element_type=F32)
    acc += jnp.dot(a_hi, w_lo, preferred_element_type=F32)
    o_ref[0] = acc + b_ref[0]


def _adaln(cvecs, w_ada, b_ada):
    tn = 1536
    n_tiles = (6 * D_MODEL) // tn
    return pl.pallas_call(
        _adaln_body,
        out_shape=jax.ShapeDtypeStruct((DEPTH, SUBLANES, 6 * D_MODEL), F32),
        grid=(DEPTH, n_tiles),
        in_specs=[
            pl.BlockSpec((SUBLANES, D_MODEL), lambda l, n: (0, 0)),
            pl.BlockSpec((1, D_MODEL, tn), lambda l, n: (l, 0, n)),
            pl.BlockSpec((1, 1, tn), lambda l, n: (l, 0, n)),
        ],
        out_specs=pl.BlockSpec((1, SUBLANES, tn), lambda l, n: (l, 0, n)),
        compiler_params=_cparams(2),
        name="adaln",
    )(cvecs, w_ada, b_ada.reshape(DEPTH, 1, 6 * D_MODEL))


_IN_SPLITS = ((0, 512), (512, 1024), (1024, 2048), (2048, 2560), (2560, 3072), (3072, 4096), (4096, 5120))


def _rms_modulate(x, nw, scale, shift):
    ms = jnp.mean(x * x, axis=-1, keepdims=True)
    return (x * lax.rsqrt(ms + RMS_EPS)) * nw * (1.0 + scale) + shift


def _branch_update(r_refs):
    if len(r_refs) == 1:
        return r_refs[0][...]
    ya_ref, yb_ref, w_ref = r_refs
    w = w_ref[...]
    return ya_ref[...] * w[:, 0:1] + yb_ref[...] * w[:, 1:2]


def _inproj_body(n_r, rowmap_ref, x_ref, *refs):
    r_refs, (gate_ref, mod_ref, nw_ref, w_ref, xo_ref, u_ref, q_ref, f_ref, i_ref, g_ref, gs_ref, gh_ref) = refs[:n_r], refs[n_r:]
    row = rowmap_ref[pl.program_id(0)]
    gate = gate_ref[pl.ds(row, 1), :]
    xn = x_ref[...] + gate * _branch_update(r_refs)
    xo_ref[...] = xn
    shift = mod_ref[pl.ds(row, 1), 0:D_MODEL]
    scale = mod_ref[pl.ds(row, 1), D_MODEL:2 * D_MODEL]
    h = _rms_modulate(xn, nw_ref[...], scale, shift).astype(BF16)
    outs = (u_ref, q_ref, f_ref, i_ref, g_ref, gs_ref, gh_ref)
    for ref, (a, b) in zip(outs, _IN_SPLITS):
        ref[...] = jnp.dot(h, w_ref[:, a:b], preferred_element_type=F32).astype(ref.dtype)


def _inproj(rowmap, x, r, gate_rows, mod, nw, w_bf16):
    n_tok = x.shape[0]
    nblk = n_tok // TOKEN_BLOCK
    tok = lambda w: pl.BlockSpec((TOKEN_BLOCK, w), lambda j, rm: (j, 0))
    full = lambda shape: pl.BlockSpec(shape, lambda j, rm: (0,) * len(shape))
    widths = [b - a for a, b in _IN_SPLITS]
    dtypes = [F32, F32, F32, BF16, F32, F32, F32]
    return pl.pallas_call(
        functools.partial(_inproj_body, len(r)),
        out_shape=[jax.ShapeDtypeStruct((n_tok, D_MODEL), F32)]
        + [jax.ShapeDtypeStruct((n_tok, w), dt) for w, dt in zip(widths, dtypes)],
        grid_spec=pltpu.PrefetchScalarGridSpec(
            num_scalar_prefetch=1, grid=(nblk,),
            in_specs=[tok(D_MODEL)] + [tok(a.shape[1]) for a in r]
            + [full((SUBLANES, D_MODEL)), full((SUBLANES, 6 * D_MODEL)), full((1, D_MODEL)), full((D_MODEL, IN_COLS))],
            out_specs=[tok(D_MODEL)] + [tok(w) for w in widths]),
        compiler_params=_cparams(1),
        name="inproj",
    )(rowmap, x, *r, gate_rows, mod, nw, w_bf16)


def _s5_direction(rev, u_ref, perm_ref, bd_ref, cd_ref, lam_ref,
                  y_ref, fin_ref, bu_ref, sb_ref, yp_ref, carry_ref):
    ns = S5_BLOCK_STATES

    up = jnp.dot(perm_ref[...], u_ref[...].astype(BF16), preferred_element_type=F32).astype(BF16)
    bu_ref[...] = jnp.dot(up, bd_ref[0, 0], preferred_element_type=F32)

    lre = lam_ref[0, 0, 0:SUBLANES, :]
    lim = lam_ref[0, 0, SUBLANES:2 * SUBLANES, :]
    order = list(range(S5_SEG))
    if rev:
        order = order[::-1]

    def step(sr, si, k):
        br = bu_ref[pl.ds(SUBLANES * k, SUBLANES), 0:ns]
        bi = bu_ref[pl.ds(SUBLANES * k, SUBLANES), ns:2 * ns]
        return lre * sr - lim * si + br, lre * si + lim * sr + bi

    sr = jnp.zeros((SUBLANES, ns), F32)
    si = jnp.zeros((SUBLANES, ns), F32)
    for k in order:
        sr, si = step(sr, si, k)

    cr = carry_ref[0:1, :]
    ci = carry_ref[1:2, :]
    lkre = lam_ref[0, 0, 2 * SUBLANES:2 * SUBLANES + 1, :]
    lkim = lam_ref[0, 0, 2 * SUBLANES + 1:2 * SUBLANES + 2, :]
    start_r = [None] * SUBLANES
    start_i = [None] * SUBLANES
    seg_order = list(range(SUBLANES))
    if rev:
        seg_order = seg_order[::-1]
    for i in seg_order:
        start_r[i] = cr
        start_i[i] = ci
        er = sr[i:i + 1]
        ei = si[i:i + 1]
        cr, ci = lkre * cr - lkim * ci + er, lkre * ci + lkim * cr + ei
    carry_ref[0:1, :] = cr
    carry_ref[1:2, :] = ci
    fin_ref[0, 0, 0:1, :] = cr
    fin_ref[0, 0, 1:2, :] = ci

    sr = jnp.concatenate(start_r, axis=0)
    si = jnp.concatenate(start_i, axis=0)
    for n in range(0, S5_SEG, 2):
        k0, k1 = order[n], order[n + 1]
        sr0, si0 = step(sr, si, k0)
        sr, si = step(sr0, si0, k1)
        lo_r, hi_r = (sr0, sr) if k0 < k1 else (sr, sr0)
        lo_i, hi_i = (si0, si) if k0 < k1 else (si, si0)
        base = SUBLANES * min(k0, k1)
        sb_ref[pl.ds(base, 2 * SUBLANES), 0:ns] = jnp.concatenate([lo_r, hi_r], axis=0).astype(BF16)
        sb_ref[pl.ds(base, 2 * SUBLANES), ns:2 * ns] = jnp.concatenate([lo_i, hi_i], axis=0).astype(BF16)

    y = jnp.dot(sb_ref[...], cd_ref[0], preferred_element_type=F32)
    for h in range(S5_CHANNELS // LANES):
        for k in range(S5_SEG):
            yp_ref[h, pl.ds(k, SUBLANES, stride=S5_PITCH), :] = y[SUBLANES * k:SUBLANES * (k + 1), h * LANES:(h + 1) * LANES]
        for i in range(SUBLANES):
            y_ref[i * S5_SEG:(i + 1) * S5_SEG, h * LANES:(h + 1) * LANES] = yp_ref[h, pl.ds(i * S5_PITCH, S5_SEG), :]


def _s5_body(seq_ref, first_ref, last_ref,
             uf_ref, ub_ref, perm_ref, bdf_ref, bdb_ref, cd_ref, lamf_ref, lamb_ref, h0f_ref, h0b_ref,
             yf_ref, yb_ref, finf_ref, finb_ref,
             buf_ref, bub_ref, sbf_ref, sbb_ref, ypf_ref, ypb_ref, carryf_ref, carryb_ref):
    p = pl.program_id(1)
    q = pl.num_programs(1) - 1 - p

    @pl.when(first_ref[p] == 1)
    def _():
        carryf_ref[...] = h0f_ref[0, 0, 0]

    @pl.when(last_ref[q] == 1)
    def _():
        carryb_ref[...] = h0b_ref[0, 0, 0]

    _s5_direction(False, uf_ref, perm_ref, bdf_ref, cd_ref, lamf_ref,
                  yf_ref, finf_ref, buf_ref, sbf_ref, ypf_ref, carryf_ref)
    _s5_direction(True, ub_ref, perm_ref, bdb_ref, cd_ref, lamb_ref,
                  yb_ref, finb_ref, bub_ref, sbb_ref, ypb_ref, carryb_ref)


def _s5_scan(tables, u, bd, cd, lam, h0):
    seq, first, last = tables
    n_tok = u.shape[0]
    nblk = n_tok // TOKEN_BLOCK
    n_seq = h0.shape[1]
    ns = S5_BLOCK_STATES
    lam_rows = lam.shape[2]
    grid = (S5_COLBLOCKS, nblk)
    fwd = lambda a, p, sq, fi, la: p
    bwd = lambda a, p, sq, fi, la: nblk - 1 - p
    tok = lambda blk: pl.BlockSpec((TOKEN_BLOCK, S5_CHANNELS), lambda a, p, sq, fi, la: (blk(a, p, sq, fi, la), a))
    par = lambda d, shape: pl.BlockSpec((1, 1) + shape, lambda a, p, sq, fi, la: (d, a, 0, 0))
    st = lambda d, blk: pl.BlockSpec((1, 1, 1, 2, ns), lambda a, p, sq, fi, la: (d, sq[blk(a, p, sq, fi, la)], a, 0, 0))
    fin = lambda blk: pl.BlockSpec((1, 1, 2, ns), lambda a, p, sq, fi, la: (sq[blk(a, p, sq, fi, la)], a, 0, 0))
    fin_shape = jax.ShapeDtypeStruct((n_seq, S5_COLBLOCKS, 2, ns), F32)
    return pl.pallas_call(
        _s5_body,
        out_shape=[jax.ShapeDtypeStruct((n_tok, S5_WIDTH), F32), jax.ShapeDtypeStruct((n_tok, S5_WIDTH), F32),
                   fin_shape, fin_shape],
        grid_spec=pltpu.PrefetchScalarGridSpec(
            num_scalar_prefetch=3, grid=grid,
            in_specs=[
                tok(fwd), tok(bwd),
                pl.BlockSpec((TOKEN_BLOCK, TOKEN_BLOCK), lambda a, p, sq, fi, la: (0, 0)),
                par(0, (S5_CHANNELS, 2 * ns)), par(1, (S5_CHANNELS, 2 * ns)),
                pl.BlockSpec((1, 2 * ns, S5_CHANNELS), lambda a, p, sq, fi, la: (a, 0, 0)),
                par(0, (lam_rows, ns)), par(1, (lam_rows, ns)),
                st(0, fwd), st(1, bwd)],
            out_specs=[tok(fwd), tok(bwd), fin(fwd), fin(bwd)],
            scratch_shapes=[pltpu.VMEM((TOKEN_BLOCK, 2 * ns), F32)] * 2
            + [pltpu.VMEM((TOKEN_BLOCK, 2 * ns), BF16)] * 2
            + [pltpu.VMEM((S5_CHANNELS // LANES, SUBLANES * S5_PITCH, LANES), F32)] * 2
            + [pltpu.VMEM((2, ns), F32)] * 2),
        compiler_params=_cparams(2),
        name="s5_scan",
    )(seq, first, last, u, u, jnp.asarray(_S5_PERM, BF16), bd, bd, cd, lam, lam, h0, h0)


def _s5_params(lam_re, lam_im, log_step, b_re, b_im, c_re, c_im):
    g, p, c = S5_GROUPS, S5_STATE, S5_GROUP
    gl = S5_CHANNELS // c
    lr = jnp.minimum(lam_re, -1e-4)
    li = lam_im
    step = jnp.exp(log_step)[:, None]
    mag = jnp.exp(lr * step)
    ar, ai = mag * jnp.cos(li * step), mag * jnp.sin(li * step)
    magk = jnp.exp(lr * step * S5_SEG)
    akr, aki = magk * jnp.cos(li * step * S5_SEG), magk * jnp.sin(li * step * S5_SEG)
    den = lr * lr + li * li
    cr = ((ar - 1.0) * lr + ai * li) / den
    ci = (ai * lr - (ar - 1.0) * li) / den
    bbr = cr[:, :, None] * b_re - ci[:, :, None] * b_im
    bbi = cr[:, :, None] * b_im + ci[:, :, None] * b_re
    eye = jnp.eye(gl, dtype=F32)

    def bmat(bb):
        bb = bb.reshape(S5_COLBLOCKS, gl, p, c)
        return jnp.einsum('agpc,gh->agchp', bb, eye).reshape(S5_COLBLOCKS, gl * c, gl * p)

    def cmat(cc):
        cc = cc.reshape(S5_COLBLOCKS, gl, c, p)
        return jnp.einsum('agcp,gh->agphc', cc, eye).reshape(S5_COLBLOCKS, gl * p, gl * c)

    bd = jnp.concatenate([bmat(bbr), bmat(bbi)], axis=2).astype(BF16)
    cd = jnp.concatenate([cmat(c_re), -cmat(c_im)], axis=1).astype(BF16)

    def lanes(v, rows):
        v = v.reshape(S5_COLBLOCKS, 1, gl * p)
        return jnp.broadcast_to(v, (S5_COLBLOCKS, rows, gl * p))

    lam = jnp.concatenate([lanes(ar, SUBLANES), lanes(ai, SUBLANES), lanes(akr, 1), lanes(aki, 1)], axis=1)
    return bd, cd, lam


def _hg_body(rev, jmap_ref, seq_ref, first_ref, last_ref,
             q_ref, f_ref, v_ref, lb_ref, h0_ref, o_ref, s_ref):
    p = pl.program_id(0)
    c_len = HG_CHUNK

    @pl.when(first_ref[p] == 1)
    def _():
        s_ref[...] = h0_ref[...]

    sub = HG_SUB
    n_sub = c_len // sub
    ri = lax.broadcasted_iota(jnp.int32, (c_len, c_len), 0)
    ci = lax.broadcasted_iota(jnp.int32, (c_len, c_len), 1)
    mask = (ri <= ci) if rev else (ri >= ci)
    tri = mask.astype(BF16)
    rb, cb = ri // sub, ci // sub
    diag_mask = mask & (rb == cb)
    off_mask = (cb > rb) if rev else (cb < rb)
    lb = lb_ref[...]
    tot_row = 0 if rev else c_len - 1
    n_chunks = TOKEN_BLOCK // c_len
    chunk_order = range(n_chunks - 1, -1, -1) if rev else range(n_chunks)
    for c in chunk_order:
        rows = pl.ds(c * c_len, c_len)
        qin = q_ref[rows, :]
        q = qin * jax.nn.sigmoid(qin) * (HG_DK ** -0.5)
        f = lb + (1.0 - lb) * jax.nn.sigmoid(f_ref[rows, :])
        k = 1.0 - f
        g = jnp.log(f)
        g0 = g.astype(BF16)
        r1 = g - g0.astype(F32)
        g1 = r1.astype(BF16)
        g2 = (r1 - g1.astype(F32)).astype(BF16)
        b = (jnp.dot(tri, g0, preferred_element_type=F32) + jnp.dot(tri, g1, preferred_element_type=F32)
             + jnp.dot(tri, g2, preferred_element_type=F32))
        btot = b[tot_row:tot_row + 1, :]
        bmid = jnp.concatenate(
            [jnp.broadcast_to(b[i * sub + sub // 2:i * sub + sub // 2 + 1, :], (sub, HG_KW)) for i in range(n_sub)], axis=0)
        qd = (q * jnp.exp(b - bmid)).astype(BF16)
        kd = (k * jnp.exp(bmid - b)).astype(BF16)
        qo, ko = {}, {}
        for i in range(n_sub):
            edge = (i + 1) * sub if rev else i * sub - 1
            if 0 <= edge < c_len:
                bref = b[edge:edge + 1, :]
                rws = slice(i * sub, (i + 1) * sub)
                qo[i] = (q[rws] * jnp.exp(b[rws] - bref)).astype(BF16)
                ko[i] = (k * jnp.exp(jnp.minimum(bref - b, 0.0))).astype(BF16)
        qs = (q * jnp.exp(b)).astype(BF16)
        ks = (k * jnp.exp(btot - b)).astype(BF16)
        dec = jnp.exp(btot)
        v = v_ref[rows, :]
        vt = v.astype(F32).T.astype(BF16)
        outs = []
        for h in range(HG_HEADS):
            sl = slice(h * HG_DK, (h + 1) * HG_DK)
            near = _dot_nt(qd[:, sl], kd[:, sl])
            far = jnp.concatenate(
                [_dot_nt(qo[i][:, sl], ko[i][:, sl]) if i in qo else jnp.zeros((sub, c_len), F32)
                 for i in range(n_sub)], axis=0)
            att = jnp.where(diag_mask, near, jnp.where(off_mask, far, 0.0)).astype(BF16)
            s_old = s_ref[0, h]
            o_h = jnp.dot(att, v[:, sl], preferred_element_type=F32) + _dot_nt(qs[:, sl], s_old.astype(BF16))
            s_ref[0, h] = s_old * dec[:, sl] + jnp.dot(vt[sl, :], ks[:, sl], preferred_element_type=F32)
            outs.append(o_h)
        o_ref[rows, :] = jnp.concatenate(outs, axis=1)


def _hg_scan(rev, tables, q, f, v, lb, h0):
    jmap, seq, first, last = tables
    n_tok = q.shape[0]
    nblk = n_tok // TOKEN_BLOCK
    n_seq = h0.shape[0]
    d = 1 if rev else 0
    tok = pl.BlockSpec((TOKEN_BLOCK, HG_WIDTH), lambda p, jm, sq, fi, la: (jm[p], 0))
    st = pl.BlockSpec((1, HG_HEADS, HG_DV, HG_DK), lambda p, jm, sq, fi, la: (sq[p], 0, 0, 0))
    return pl.pallas_call(
        functools.partial(_hg_body, rev),
        out_shape=[jax.ShapeDtypeStruct((n_tok, HG_WIDTH), F32),
                   jax.ShapeDtypeStruct((n_seq, HG_HEADS, HG_DV, HG_DK), F32)],
        grid_spec=pltpu.PrefetchScalarGridSpec(
            num_scalar_prefetch=4, grid=(nblk,),
            in_specs=[tok,
                      pl.BlockSpec((TOKEN_BLOCK, HG_KW), lambda p, jm, sq, fi, la: (jm[p], d)),
                      tok,
                      pl.BlockSpec((1, HG_KW), lambda p, jm, sq, fi, la: (0, 0)),
                      st],
            out_specs=[tok, st]),
        compiler_params=_cparams(1),
        name="hg_bwd" if rev else "hg_fwd",
    )(jmap, seq, first, last, q, f, v, lb, h0)


def _merge_body(rowmap_ref, x_ref, ysf_ref, ysb_ref, u_ref, ohf_ref, ohb_ref, g_ref, gs_ref, gh_ref,
                mod_ref, d_ref, bglu_ref, hnw_ref, n2w_ref, wglu_ref, wbs_ref, wbh_ref, wout_ref,
                rhi_ref, rlo_ref, xm_ref, h2_ref, sc_ref):
    row = rowmap_ref[pl.program_id(0)]
    mrow = lambda n: mod_ref[pl.ds(row, 1), n * D_MODEL:(n + 1) * D_MODEL]
    gate1, shift2, scale2 = mrow(2), mrow(3), mrow(4)

    y = ysf_ref[...] + ysb_ref[...] + d_ref[...] * u_ref[...]
    gl = jax.nn.gelu(y)
    z = jnp.dot(gl.astype(BF16), wglu_ref[...], preferred_element_type=F32) + bglu_ref[...]
    y_s5 = gl * jax.nn.sigmoid(z)

    o = ohf_ref[...] + ohb_ref[...]
    normed = []
    for h in range(HG_HEADS):
        oh = o[:, h * HG_DV:(h + 1) * HG_DV]
        normed.append(oh * lax.rsqrt(jnp.mean(oh * oh, axis=-1, keepdims=True) + RMS_EPS))
    gout = g_ref[...]
    y_hg = jnp.concatenate(normed, axis=1) * hnw_ref[...] * (gout * jax.nn.sigmoid(gout))

    merged = (jax.nn.sigmoid(gs_ref[...]) * jnp.dot(y_s5.astype(BF16), wbs_ref[...], preferred_element_type=F32)
              + jax.nn.sigmoid(gh_ref[...]) * jnp.dot(y_hg.astype(BF16), wbh_ref[...], preferred_element_type=F32))
    xm = x_ref[...] + gate1 * jnp.dot(merged.astype(BF16), wout_ref[...], preferred_element_type=F32)
    xm_ref[...] = xm
    h2 = _rms_modulate(xm, n2w_ref[...], scale2, shift2)
    h2_ref[...] = h2
    h_hi, h_lo = _split_bf16(h2)
    logits = _dot_nt(rhi_ref[...], h_hi) + _dot_nt(rhi_ref[...], h_lo) + _dot_nt(rlo_ref[...], h_hi)
    sc_ref[...] = jax.nn.sigmoid(logits)


def _merge(rowmap, x, ysf, ysb, u, ohf, ohb, g, gs, gh, mod, d, bglu, hnw, n2w,
           wglu, wbs, wbh, wout, rhi, rlo):
    n_tok = x.shape[0]
    nblk = n_tok // TOKEN_BLOCK
    tok = lambda w: pl.BlockSpec((TOKEN_BLOCK, w), lambda j, rm: (j, 0))
    full = lambda shape: pl.BlockSpec(shape, lambda j, rm: (0,) * len(shape))
    return pl.pallas_call(
        _merge_body,
        out_shape=[jax.ShapeDtypeStruct((n_tok, D_MODEL), F32),
                   jax.ShapeDtypeStruct((n_tok, D_MODEL), F32),
                   jax.ShapeDtypeStruct((N_EXPERTS, n_tok), F32)],
        grid_spec=pltpu.PrefetchScalarGridSpec(
            num_scalar_prefetch=1, grid=(nblk,),
            in_specs=[tok(D_MODEL), tok(S5_WIDTH), tok(S5_WIDTH), tok(S5_WIDTH), tok(HG_WIDTH), tok(HG_WIDTH),
                      tok(HG_WIDTH), tok(D_MODEL), tok(D_MODEL),
                      full((SUBLANES, 6 * D_MODEL)), full((1, S5_WIDTH)), full((1, S5_WIDTH)),
                      full((1, HG_WIDTH)), full((1, D_MODEL)),
                      full((S5_WIDTH, S5_WIDTH)), full((S5_WIDTH, D_MODEL)), full((HG_WIDTH, D_MODEL)),
                      full((D_MODEL, D_MODEL)), full((N_EXPERTS, D_MODEL)), full((N_EXPERTS, D_MODEL))],
            out_specs=[tok(D_MODEL), tok(D_MODEL),
                       pl.BlockSpec((N_EXPERTS, TOKEN_BLOCK), lambda j, rm: (0, j))]),
        compiler_params=_cparams(1),
        name="merge",
    )(rowmap, x, ysf, ysb, u, ohf, ohb, g, gs, gh, mod, d, bglu, hnw, n2w, wglu, wbs, wbh, wout, rhi, rlo)


def _experts_body(has_prev, be_ref, nused_ref, x_ref, wg_ref, wu_ref, wd_ref, *rest):
    y_ref, wgb_ref, wub_ref, wdb_ref = rest[1:] if has_prev else rest
    b = pl.program_id(0)
    e = be_ref[b]
    prev = be_ref[jnp.maximum(b - 1, 0)]

    @pl.when((b == 0) | (e != prev))
    def _():
        wgb_ref[...] = wg_ref[0, 0].astype(BF16)
        wub_ref[...] = wu_ref[0, 0].astype(BF16)
        wdb_ref[...] = wd_ref[0, 0].astype(BF16)

    @pl.when(b < nused_ref[0])
    def _():
        x = x_ref[...].astype(BF16)
        gate = jnp.dot(x, wgb_ref[...], preferred_element_type=F32)
        up = jnp.dot(x, wub_ref[...], preferred_element_type=F32)
        hid = (gate * jax.nn.sigmoid(gate) * up).astype(BF16)
        y_ref[...] = jnp.dot(hid, wdb_ref[...], preferred_element_type=F32)

    @pl.when(b >= nused_ref[0])
    def _():
        y_ref[...] = jnp.zeros_like(y_ref)


def _experts(layer, first_blk, n_rows_total, blk_expert, nused, xb, w_gate, w_up, w_down, y_prev):
    nblk = xb.shape[0] // MOE_ROWS
    wspec = pl.BlockSpec((1, 1, D_MODEL, D_MODEL), lambda b, be, nu: (layer, be[b], 0, 0))
    has_prev = y_prev is not None
    in_specs = [pl.BlockSpec((MOE_ROWS, D_MODEL), lambda b, be, nu: (b, 0)), wspec, wspec, wspec]
    args = [blk_expert, nused, xb, w_gate, w_up, w_down]
    if has_prev:
        in_specs.append(pl.BlockSpec(memory_space=pl.ANY))
        args.append(y_prev)
    return pl.pallas_call(
        functools.partial(_experts_body, has_prev),
        out_shape=jax.ShapeDtypeStruct((n_rows_total, D_MODEL), F32),
        grid_spec=pltpu.PrefetchScalarGridSpec(
            num_scalar_prefetch=2, grid=(nblk,),
            in_specs=in_specs,
            out_specs=pl.BlockSpec((MOE_ROWS, D_MODEL), lambda b, be, nu: (first_blk + b, 0)),
            scratch_shapes=[pltpu.VMEM((D_MODEL, D_MODEL), BF16)] * 3),
        input_output_aliases={len(args) - 1: 0} if has_prev else {},
        compiler_params=_cparams(1),
        name="experts",
    )(*args)


def _first_max(vals):
    best, idx = vals[0], jnp.zeros(vals[0].shape, jnp.int32)
    for j in range(1, len(vals)):
        upd = vals[j] > best
        idx = jnp.where(upd, j, idx)
        best = jnp.where(upd, vals[j], best)
    return best, idx


def _pick(idx, vals):
    out = vals[-1]
    for j in range(len(vals) - 2, -1, -1):
        out = jnp.where(idx == j, vals[j], out)
    return out


def _router_body(sc_ref, rb_ref, eid_ref, wts_ref, rank_ref, cnt_ref, base_ref):
    j = pl.program_id(0)

    @pl.when(j == 0)
    def _():
        base_ref[...] = jnp.zeros_like(base_ref)

    s = sc_ref[...]
    sel = s + rb_ref[...]
    srow = [s[e:e + 1, :] for e in range(N_EXPERTS)]
    brow = [sel[e:e + 1, :] for e in range(N_EXPERTS)]
    epg = EXPERTS_PER_GROUP
    gscore = []
    for g in range(N_GROUPS):
        x = brow[g * epg:(g + 1) * epg]
        pairs = [x[a] + x[b] for a in range(epg) for b in range(a + 1, epg)]
        gscore.append(functools.reduce(jnp.maximum, pairs))
    _, bestg = _first_max(gscore)
    y = [_pick(bestg, [brow[g * epg + i] for g in range(N_GROUPS)]) for i in range(epg)]
    sy = [_pick(bestg, [srow[g * epg + i] for g in range(N_GROUPS)]) for i in range(epg)]
    _, i1 = _first_max(y)
    v2 = jnp.full(y[0].shape, -jnp.inf, F32)
    i2 = jnp.zeros(y[0].shape, jnp.int32)
    for i in range(epg):
        upd = (i1 != i) & (y[i] > v2)
        i2 = jnp.where(upd, i, i2)
        v2 = jnp.where(upd, y[i], v2)
    e0 = bestg * epg + i1
    e1 = bestg * epg + i2
    w0 = _pick(i1, sy)
    w1 = _pick(i2, sy)
    wsum = w0 + w1
    eid_ref[0:1, :] = e0
    eid_ref[1:2, :] = e1
    wts_ref[0:1, :] = w0 / wsum
    wts_ref[1:2, :] = w1 / wsum

    eiota = lax.broadcasted_iota(jnp.int32, s.shape, 0)
    m0 = eiota == e0
    m1 = eiota == e1
    member = (m0 | m1).astype(BF16)
    tb = s.shape[1]
    before = (lax.broadcasted_iota(jnp.int32, (tb, tb), 0) < lax.broadcasted_iota(jnp.int32, (tb, tb), 1)).astype(BF16)
    pos = base_ref[...] + jnp.dot(member, before, preferred_element_type=F32)
    rank_ref[0:1, :] = jnp.sum(jnp.where(m0, pos, 0.0), axis=0, keepdims=True).astype(jnp.int32)
    rank_ref[1:2, :] = jnp.sum(jnp.where(m1, pos, 0.0), axis=0, keepdims=True).astype(jnp.int32)
    base = base_ref[...] + jnp.sum(member.astype(F32), axis=1, keepdims=True)
    base_ref[...] = base
    cnt_ref[...] = jnp.broadcast_to(base, cnt_ref.shape).astype(jnp.int32)


def _router(scores_t, router_b):
    n_tok = scores_t.shape[1]
    tok = lambda rows: pl.BlockSpec((rows, TOKEN_BLOCK), lambda j: (0, j))
    return pl.pallas_call(
        _router_body,
        out_shape=[jax.ShapeDtypeStruct((TOP_K, n_tok), jnp.int32), jax.ShapeDtypeStruct((TOP_K, n_tok), F32),
                   jax.ShapeDtypeStruct((TOP_K, n_tok), jnp.int32), jax.ShapeDtypeStruct((N_EXPERTS, LANES), jnp.int32)],
        grid=(n_tok // TOKEN_BLOCK,),
        in_specs=[tok(N_EXPERTS), pl.BlockSpec((N_EXPERTS, 1), lambda j: (0, 0))],
        out_specs=[tok(TOP_K), tok(TOP_K), tok(TOP_K), pl.BlockSpec((N_EXPERTS, LANES), lambda j: (0, 0))],
        scratch_shapes=[pltpu.VMEM((N_EXPERTS, 1), F32)],
        compiler_params=_cparams(1),
        name="router",
    )(scores_t, router_b.reshape(N_EXPERTS, 1))


def _dispatch_plan(eid, rank, counts):
    n_tok = eid.shape[1]
    n_slots = n_tok * TOP_K
    padded = (counts + MOE_ROWS - 1) // MOE_ROWS * MOE_ROWS
    pad_end = jnp.cumsum(padded)
    pad_start = pad_end - padded
    e_ids = jnp.arange(N_EXPERTS, dtype=jnp.int32)
    dest = jnp.sum(jnp.where(eid[..., None] == e_ids, pad_start, 0), axis=-1) + rank
    n_blocks = -(-n_slots // MOE_ROWS) + N_EXPERTS
    n_rows = n_blocks * MOE_ROWS
    blk_start = jnp.arange(n_blocks, dtype=jnp.int32) * MOE_ROWS
    blk_expert = jnp.minimum(jnp.sum(pad_end[None, :] <= blk_start[:, None], axis=-1), N_EXPERTS - 1).astype(jnp.int32)
    nused = (pad_end[-1:] // MOE_ROWS).astype(jnp.int32)
    n_fill = n_rows - n_slots
    gap = padded - counts
    gap_before = jnp.cumsum(gap) - gap
    j = jnp.arange(n_fill, dtype=jnp.int32)
    in_gap = (j[:, None] >= gap_before[None, :]) & (j[:, None] < (gap_before + gap)[None, :])
    fill_pos = jnp.where(jnp.any(in_gap, axis=-1),
                         jnp.sum(jnp.where(in_gap, (pad_start + counts - gap_before)[None, :], 0), axis=-1) + j,
                         pad_end[-1] - jnp.sum(gap) + j)
    keys = jnp.concatenate([dest.reshape(-1), fill_pos])
    vals = jnp.concatenate([jnp.tile(jnp.arange(n_tok, dtype=jnp.int32), TOP_K), j % n_tok])
    _, row_tok = lax.sort_key_val(keys, vals)
    return dest, blk_expert, nused, row_tok


def _final_body(n_r, rowmap_ref, x_ref, *refs):
    r_refs, (gate_ref, nw_ref, o_ref) = refs[:n_r], refs[n_r:]
    row = rowmap_ref[pl.program_id(0)]
    xn = x_ref[...] + gate_ref[pl.ds(row, 1), :] * _branch_update(r_refs)
    ms = jnp.mean(xn * xn, axis=-1, keepdims=True)
    o_ref[...] = xn * lax.rsqrt(ms + RMS_EPS) * nw_ref[...]


def _final(rowmap, x, r, gate_rows, nw):
    n_tok = x.shape[0]
    tok = lambda w: pl.BlockSpec((TOKEN_BLOCK, w), lambda j, rm: (j, 0))
    full = lambda shape: pl.BlockSpec(shape, lambda j, rm: (0,) * len(shape))
    return pl.pallas_call(
        functools.partial(_final_body, len(r)),
        out_shape=jax.ShapeDtypeStruct((n_tok, D_MODEL), F32),
        grid_spec=pltpu.PrefetchScalarGridSpec(
            num_scalar_prefetch=1, grid=(n_tok // TOKEN_BLOCK,),
            in_specs=[tok(D_MODEL)] + [tok(a.shape[1]) for a in r] + [full((SUBLANES, D_MODEL)), full((1, D_MODEL))],
            out_specs=tok(D_MODEL)),
        compiler_params=_cparams(1),
        name="final_norm",
    )(rowmap, x, *r, gate_rows, nw)


def _grid_pos_embed(rows):
    r = jnp.repeat(jnp.arange(rows, dtype=F32), GRID_W)
    col = jnp.tile(jnp.arange(GRID_W, dtype=F32), rows)
    quarter = D_MODEL // 4
    omega = 1.0 / (10000.0 ** (jnp.arange(quarter, dtype=F32) / quarter))

    def axis_embed(pos):
        ang = pos[:, None] * omega[None, :]
        return jnp.concatenate([jnp.sin(ang), jnp.cos(ang)], axis=-1)

    return jnp.concatenate([axis_embed(r), axis_embed(col)], axis=-1)


def kernel(x_prompt, x_sample, state_s5_re, state_s5_im, state_hg, c, c_ctx, w_ada, b_ada, norm1_w, norm2_w, w_in, s5_lam_re, s5_lam_im, s5_log_step, s5_b_re, s5_b_im, s5_c_re, s5_c_im, s5_d, s5_w_glu, s5_b_glu, hg_lb_logits, hg_norm_w, w_br_s5, w_br_hg, w_out, router_w, router_b, moe_w_gate, moe_w_up, moe_w_down, norm_f_w):
    n_ctx, ctx_len, _ = x_prompt.shape
    n_lat, lat_len, _ = x_sample.shape
    n_ctx_tok = n_ctx * ctx_len
    n_seq = n_ctx + n_lat
    assert n_lat + 1 <= SUBLANES and ctx_len % TOKEN_BLOCK == 0 and lat_len % TOKEN_BLOCK == 0

    seq_np, first_np, last_np, modrow_np = _block_tables(n_ctx, ctx_len, n_lat, lat_len)
    nblk = seq_np.shape[0]
    jmap_f = np.arange(nblk, dtype=np.int32)
    jmap_b = jmap_f[::-1].copy()
    tab_f = tuple(jnp.asarray(t) for t in (jmap_f, seq_np, first_np, last_np))
    tab_b = tuple(jnp.asarray(t) for t in (jmap_b, seq_np[::-1].copy(), last_np[::-1].copy(), first_np[::-1].copy()))
    tab_s5 = tuple(jnp.asarray(t) for t in (seq_np, first_np, last_np))
    rowmap = jnp.asarray(modrow_np)

    cvecs = jnp.zeros((SUBLANES, D_MODEL), F32).at[0].set(c_ctx).at[1:1 + n_lat].set(c)
    mod_all = _adaln(cvecs, w_ada, b_ada)

    probs = jax.nn.softmax(hg_lb_logits.astype(F32), axis=0)
    lower_bounds = jnp.cumsum(probs, axis=0) - probs[0]

    pos = _grid_pos_embed(lat_len // GRID_W)
    x = jnp.concatenate([x_prompt.reshape(n_ctx_tok, D_MODEL), x_sample.reshape(n_lat * lat_len, D_MODEL)], axis=0)
    resid = (jnp.concatenate([jnp.zeros((n_ctx_tok, D_MODEL), F32), jnp.tile(pos, (n_lat, 1))], axis=0),)
    gate_rows = jnp.ones((SUBLANES, D_MODEL), F32)

    router_t = router_w.T.astype(F32)
    r_hi, r_lo = _split_bf16(router_t)
    ns = S5_BLOCK_STATES
    zeros_s5 = jnp.zeros((n_ctx, S5_COLBLOCKS, 2, ns), F32)
    zeros_hg = jnp.zeros((n_ctx, HG_HEADS, HG_DV, HG_DK), F32)

    new_s5_re, new_s5_im, new_hg = [], [], []
    for l in range(DEPTH):
        mod = mod_all[l]
        x, u, q_in, f_raw, i_in, g_out, gate_s5, gate_hg = _inproj(
            rowmap, x, resid, gate_rows, mod, norm1_w[l].reshape(1, D_MODEL), w_in[l].astype(BF16))

        par = [_s5_params(s5_lam_re[l, d], s5_lam_im[l, d], s5_log_step[l, d], s5_b_re[l], s5_b_im[l], s5_c_re[l], s5_c_im[l])
               for d in range(2)]
        h0_s5 = jnp.stack([
            jnp.concatenate([zeros_s5, jnp.stack([state_s5_re[:, l, d], state_s5_im[:, l, d]], axis=1).astype(F32).reshape(
                n_lat, 2, S5_COLBLOCKS, ns).swapaxes(1, 2)], axis=0) for d in range(2)])
        y_f, y_b, fin_f, fin_b = _s5_scan(tab_s5, u, jnp.stack([par[0][0], par[1][0]]), par[0][1],
                                          jnp.stack([par[0][2], par[1][2]]), h0_s5)
        ys = [y_f, y_b]
        s5_fin_re = [fin[:n_ctx, :, 0].reshape(n_ctx, S5_GROUPS, S5_STATE) for fin in (fin_f, fin_b)]
        s5_fin_im = [fin[:n_ctx, :, 1].reshape(n_ctx, S5_GROUPS, S5_STATE) for fin in (fin_f, fin_b)]

        os_, hg_fin = [], []
        for d, tabs in enumerate((tab_f, tab_b)):

            h0 = jnp.concatenate([zeros_hg, jnp.swapaxes(state_hg[:, l, d].astype(F32), -1, -2)], axis=0)
            o_d, hfin = _hg_scan(bool(d), tabs, q_in, f_raw, i_in, lower_bounds[l, d].reshape(1, HG_KW), h0)
            os_.append(o_d)
            hg_fin.append(jnp.swapaxes(hfin[:n_ctx], -1, -2))
        new_s5_re.append(jnp.stack(s5_fin_re, axis=1))
        new_s5_im.append(jnp.stack(s5_fin_im, axis=1))
        new_hg.append(jnp.stack(hg_fin, axis=1))

        x, h2, scores_t = _merge(
            rowmap, x, ys[0], ys[1], u, os_[0], os_[1], g_out, gate_s5, gate_hg, mod,
            s5_d[l].reshape(1, S5_WIDTH), s5_b_glu[l].reshape(1, S5_WIDTH), hg_norm_w[l].reshape(1, HG_WIDTH),
            norm2_w[l].reshape(1, D_MODEL), s5_w_glu[l].astype(BF16), w_br_s5[l].astype(BF16),
            w_br_hg[l].astype(BF16), w_out[l].astype(BF16), r_hi, r_lo)

        eid, wts, rank, counts = _router(scores_t, router_b.astype(F32))
        dest, blk_expert, nused, row_tok = _dispatch_plan(eid, rank, counts[:, 0])
        n_blocks = blk_expert.shape[0]
        cb = n_blocks // MOE_CHUNKS
        yb = None
        for ch in range(MOE_CHUNKS):
            xb = h2[row_tok[ch * cb * MOE_ROWS:(ch + 1) * cb * MOE_ROWS]]
            yb = _experts(l, ch * cb, n_blocks * MOE_ROWS, blk_expert[ch * cb:(ch + 1) * cb],
                          jnp.clip(nused - ch * cb, 0, cb), xb, moe_w_gate, moe_w_up, moe_w_down, yb)
        resid = (yb[dest[0]], yb[dest[1]], wts.T)
        gate_rows = mod[:, 5 * D_MODEL:6 * D_MODEL]

    y = _final(rowmap, x, resid, gate_rows, norm_f_w.reshape(1, D_MODEL))
    y_prompt = y[:n_ctx_tok].reshape(n_ctx, ctx_len, D_MODEL)
    y_sample = y[n_ctx_tok:].reshape(n_lat, lat_len, D_MODEL)
    new_state_s5_re = jnp.stack(new_s5_re, axis=1).astype(state_s5_re.dtype)
    new_state_s5_im = jnp.stack(new_s5_im, axis=1).astype(state_s5_im.dtype)
    new_state_hg = jnp.stack(new_hg, axis=1).astype(state_hg.dtype)
    return (y_prompt, y_sample, new_state_s5_re, new_state_s5_im, new_state_hg)
```

```python
import functools
import math

import jax
import jax.numpy as jnp
import numpy as np
from jax import lax
from jax.experimental import pallas as pl
from jax.experimental.pallas import tpu as pltpu

F32 = jnp.float32
BF16 = jnp.bfloat16

D_MODEL = 1024
DEPTH = 4
GRID_W = 64
RMS_EPS = 1e-6
S5_WIDTH = 512
S5_GROUP = 16
S5_GROUPS = 32
S5_STATE = 64
HG_HEADS = 4
HG_DK = 128
HG_DV = 128
HG_WIDTH = 512
HG_KW = 512
HG_CHUNK = 64
HG_SUB = 16
N_EXPERTS = 16
N_GROUPS = 4
EXPERTS_PER_GROUP = 4
TOP_K = 2
IN_COLS = 5120

LANES = 128
SUBLANES = 8
VMEM_LIMIT_BYTES = 56 * 1024 * 1024

TOKEN_BLOCK = 256
DENSE_BLOCK = 512
S5_SEG = TOKEN_BLOCK // SUBLANES
MXU_DEPTH = 256
S5_CHANNELS = MXU_DEPTH
S5_COLBLOCKS = S5_WIDTH // S5_CHANNELS
S5_BLOCK_STATES = (S5_CHANNELS // S5_GROUP) * S5_STATE
S5_PITCH = S5_SEG + 4
_S5_PERM = np.zeros((TOKEN_BLOCK, TOKEN_BLOCK), np.float32)
_S5_PERM[np.arange(TOKEN_BLOCK), (np.arange(TOKEN_BLOCK) % SUBLANES) * S5_SEG + np.arange(TOKEN_BLOCK) // SUBLANES] = 1.0
MOE_ROWS = 256
MOE_CHUNKS = 4


def _block_tables(n_ctx_seq, ctx_len, n_lat_seq, lat_len):
    seq, first, last, modrow = [], [], [], []
    for s in range(n_ctx_seq):
        nb = ctx_len // TOKEN_BLOCK
        for b in range(nb):
            seq.append(s); first.append(int(b == 0)); last.append(int(b == nb - 1)); modrow.append(0)
    for s in range(n_lat_seq):
        nb = lat_len // TOKEN_BLOCK
        for b in range(nb):
            seq.append(n_ctx_seq + s); first.append(int(b == 0)); last.append(int(b == nb - 1))
            modrow.append(1 + s)
    return (np.asarray(seq, np.int32), np.asarray(first, np.int32),
            np.asarray(last, np.int32), np.asarray(modrow, np.int32))


def _cparams(n_axes):
    return pltpu.CompilerParams(dimension_semantics=("arbitrary",) * n_axes,
                                vmem_limit_bytes=VMEM_LIMIT_BYTES)


def _split_bf16(a):
    hi = a.astype(BF16)
    lo = (a - hi.astype(F32)).astype(BF16)
    return hi, lo


def _dot_nt(a, b):
    return lax.dot_general(a, b, (((1,), (1,)), ((), ())), preferred_element_type=F32)


def _adaln_body(c_ref, w_ref, b_ref, o_ref):
    a = c_ref[...]
    a = a * jax.nn.sigmoid(a)
    a_hi, a_lo = _split_bf16(a)
    w_hi, w_lo = _split_bf16(w_ref[0])
    acc = jnp.dot(a_hi, w_hi, preferred_element_type=F32)
    acc += jnp.dot(a_lo, w_hi, preferred_element_type=F32)
    acc += jnp.dot(a_hi, w_lo, preferred_element_type=F32)
    o_ref[0] = acc + b_ref[0]


def _adaln(cvecs, w_ada, b_ada):
    tn = 1536
    n_tiles = (6 * D_MODEL) // tn
    return pl.pallas_call(
        _adaln_body,
        out_shape=jax.ShapeDtypeStruct((DEPTH, SUBLANES, 6 * D_MODEL), F32),
        grid=(DEPTH, n_tiles),
        in_specs=[
            pl.BlockSpec((SUBLANES, D_MODEL), lambda l, n: (0, 0)),
            pl.BlockSpec((1, D_MODEL, tn), lambda l, n: (l, 0, n)),
            pl.BlockSpec((1, 1, tn), lambda l, n: (l, 0, n)),
        ],
        out_specs=pl.BlockSpec((1, SUBLANES, tn), lambda l, n: (l, 0, n)),
        compiler_params=_cparams(2),
        name="adaln",
    )(cvecs, w_ada, b_ada.reshape(DEPTH, 1, 6 * D_MODEL))


_IN_SPLITS = ((0, 512), (512, 1024), (1024, 2048), (2048, 2560), (2560, 3072), (3072, 4096), (4096, 5120))


def _rms_modulate(x, nw, scale, shift):
    ms = jnp.mean(x * x, axis=-1, keepdims=True)
    return (x * lax.rsqrt(ms + RMS_EPS)) * nw * (1.0 + scale) + shift


def _branch_update(r_refs):
    if len(r_refs) == 1:
        return r_refs[0][...]
    ya_ref, yb_ref, w_ref = r_refs
    w = w_ref[...]
    return ya_ref[...] * w[:, 0:1] + yb_ref[...] * w[:, 1:2]


def _inproj_body(n_r, rowmap_ref, x_ref, *refs):
    r_refs, (gate_ref, mod_ref, nw_ref, w_ref, xo_ref, u_ref, q_ref, f_ref, i_ref, g_ref, gs_ref, gh_ref) = refs[:n_r], refs[n_r:]
    row = rowmap_ref[pl.program_id(0)]
    gate = gate_ref[pl.ds(row, 1), :]
    xn = x_ref[...] + gate * _branch_update(r_refs)
    xo_ref[...] = xn
    shift = mod_ref[pl.ds(row, 1), 0:D_MODEL]
    scale = mod_ref[pl.ds(row, 1), D_MODEL:2 * D_MODEL]
    h = _rms_modulate(xn, nw_ref[...], scale, shift).astype(BF16)
    outs = (u_ref, q_ref, f_ref, i_ref, g_ref, gs_ref, gh_ref)
    for ref, (a, b) in zip(outs, _IN_SPLITS):
        ref[...] = jnp.dot(h, w_ref[:, a:b], preferred_element_type=F32).astype(ref.dtype)


def _inproj(rowmap, x, r, gate_rows, mod, nw, w_bf16):
    n_tok = x.shape[0]
    nblk = n_tok // DENSE_BLOCK
    tok = lambda w: pl.BlockSpec((DENSE_BLOCK, w), lambda j, rm: (j, 0))
    full = lambda shape: pl.BlockSpec(shape, lambda j, rm: (0,) * len(shape))
    widths = [b - a for a, b in _IN_SPLITS]
    dtypes = [F32, BF16, F32, BF16, BF16, BF16, BF16]
    return pl.pallas_call(
        functools.partial(_inproj_body, len(r)),
        out_shape=[jax.ShapeDtypeStruct((n_tok, D_MODEL), F32)]
        + [jax.ShapeDtypeStruct((n_tok, w), dt) for w, dt in zip(widths, dtypes)],
        grid_spec=pltpu.PrefetchScalarGridSpec(
            num_scalar_prefetch=1, grid=(nblk,),
            in_specs=[tok(D_MODEL)] + [tok(a.shape[1]) for a in r]
            + [full((SUBLANES, D_MODEL)), full((SUBLANES, 6 * D_MODEL)), full((1, D_MODEL)), full((D_MODEL, IN_COLS))],
            out_specs=[tok(D_MODEL)] + [tok(w) for w in widths]),
        compiler_params=_cparams(1),
        name="inproj",
    )(rowmap, x, *r, gate_rows, mod, nw, w_bf16)


def _s5_direction(rev, u_ref, perm_ref, bd_ref, cd_ref, lam_ref,
                  y_ref, fin_ref, bu_ref, sb_ref, yp_ref, carry_ref):
    ns = S5_BLOCK_STATES

    up = jnp.dot(perm_ref[...], u_ref[...].astype(BF16), preferred_element_type=F32).astype(BF16)
    bu_ref[...] = jnp.dot(up, bd_ref[0, 0], preferred_element_type=F32)

    lre = lam_ref[0, 0, 0:SUBLANES, :]
    lim = lam_ref[0, 0, SUBLANES:2 * SUBLANES, :]
    order = list(range(S5_SEG))
    if rev:
        order = order[::-1]

    def step(sr, si, k):
        br = bu_ref[pl.ds(SUBLANES * k, SUBLANES), 0:ns]
        bi = bu_ref[pl.ds(SUBLANES * k, SUBLANES), ns:2 * ns]
        return lre * sr - lim * si + br, lre * si + lim * sr + bi

    sr = jnp.zeros((SUBLANES, ns), F32)
    si = jnp.zeros((SUBLANES, ns), F32)
    for k in order:
        sr, si = step(sr, si, k)

    cr = carry_ref[0:1, :]
    ci = carry_ref[1:2, :]
    lkre = lam_ref[0, 0, 2 * SUBLANES:2 * SUBLANES + 1, :]
    lkim = lam_ref[0, 0, 2 * SUBLANES + 1:2 * SUBLANES + 2, :]
    start_r = [None] * SUBLANES
    start_i = [None] * SUBLANES
    seg_order = list(range(SUBLANES))
    if rev:
        seg_order = seg_order[::-1]
    for i in seg_order:
        start_r[i] = cr
        start_i[i] = ci
        er = sr[i:i + 1]
        ei = si[i:i + 1]
        cr, ci = lkre * cr - lkim * ci + er, lkre * ci + lkim * cr + ei
    carry_ref[0:1, :] = cr
    carry_ref[1:2, :] = ci
    fin_ref[0, 0, 0:1, :] = cr
    fin_ref[0, 0, 1:2, :] = ci

    sr = jnp.concatenate(start_r, axis=0)
    si = jnp.concatenate(start_i, axis=0)
    for n in range(0, S5_SEG, 2):
        k0, k1 = order[n], order[n + 1]
        sr0, si0 = step(sr, si, k0)
        sr, si = step(sr0, si0, k1)
        lo_r, hi_r = (sr0, sr) if k0 < k1 else (sr, sr0)
        lo_i, hi_i = (si0, si) if k0 < k1 else (si, si0)
        base = SUBLANES * min(k0, k1)
        sb_ref[pl.ds(base, 2 * SUBLANES), 0:ns] = jnp.concatenate([lo_r, hi_r], axis=0).astype(BF16)
        sb_ref[pl.ds(base, 2 * SUBLANES), ns:2 * ns] = jnp.concatenate([lo_i, hi_i], axis=0).astype(BF16)

    y = jnp.dot(sb_ref[...], cd_ref[0], preferred_element_type=F32)
    for h in range(S5_CHANNELS // LANES):
        for k in range(S5_SEG):
            yp_ref[h, pl.ds(k, SUBLANES, stride=S5_PITCH), :] = y[SUBLANES * k:SUBLANES * (k + 1), h * LANES:(h + 1) * LANES]
        for i in range(SUBLANES):
            y_ref[i * S5_SEG:(i + 1) * S5_SEG, h * LANES:(h + 1) * LANES] = yp_ref[h, pl.ds(i * S5_PITCH, S5_SEG), :]


def _s5_body(seq_ref, first_ref, last_ref,
             uf_ref, ub_ref, perm_ref, bdf_ref, bdb_ref, cd_ref, lamf_ref, lamb_ref, h0f_ref, h0b_ref,
             yf_ref, yb_ref, finf_ref, finb_ref,
             buf_ref, bub_ref, sbf_ref, sbb_ref, ypf_ref, ypb_ref, carryf_ref, carryb_ref):
    p = pl.program_id(1)
    q = pl.num_programs(1) - 1 - p

    @pl.when(first_ref[p] == 1)
    def _():
        carryf_ref[...] = h0f_ref[0, 0, 0]

    @pl.when(last_ref[q] == 1)
    def _():
        carryb_ref[...] = h0b_ref[0, 0, 0]

    _s5_direction(False, uf_ref, perm_ref, bdf_ref, cd_ref, lamf_ref,
                  yf_ref, finf_ref, buf_ref, sbf_ref, ypf_ref, carryf_ref)
    _s5_direction(True, ub_ref, perm_ref, bdb_ref, cd_ref, lamb_ref,
                  yb_ref, finb_ref, bub_ref, sbb_ref, ypb_ref, carryb_ref)


def _s5_scan(tables, u, bd, cd, lam, h0):
    seq, first, last = tables
    n_tok = u.shape[0]
    nblk = n_tok // TOKEN_BLOCK
    n_seq = h0.shape[1]
    ns = S5_BLOCK_STATES
    lam_rows = lam.shape[2]
    grid = (S5_COLBLOCKS, nblk)
    fwd = lambda a, p, sq, fi, la: p
    bwd = lambda a, p, sq, fi, la: nblk - 1 - p
    tok = lambda blk: pl.BlockSpec((TOKEN_BLOCK, S5_CHANNELS), lambda a, p, sq, fi, la: (blk(a, p, sq, fi, la), a))
    par = lambda d, shape: pl.BlockSpec((1, 1) + shape, lambda a, p, sq, fi, la: (d, a, 0, 0))
    st = lambda d, blk: pl.BlockSpec((1, 1, 1, 2, ns), lambda a, p, sq, fi, la: (d, sq[blk(a, p, sq, fi, la)], a, 0, 0))
    fin = lambda blk: pl.BlockSpec((1, 1, 2, ns), lambda a, p, sq, fi, la: (sq[blk(a, p, sq, fi, la)], a, 0, 0))
    fin_shape = jax.ShapeDtypeStruct((n_seq, S5_COLBLOCKS, 2, ns), F32)
    return pl.pallas_call(
        _s5_body,
        out_shape=[jax.ShapeDtypeStruct((n_tok, S5_WIDTH), F32), jax.ShapeDtypeStruct((n_tok, S5_WIDTH), F32),
                   fin_shape, fin_shape],
        grid_spec=pltpu.PrefetchScalarGridSpec(
            num_scalar_prefetch=3, grid=grid,
            in_specs=[
                tok(fwd), tok(bwd),
                pl.BlockSpec((TOKEN_BLOCK, TOKEN_BLOCK), lambda a, p, sq, fi, la: (0, 0)),
                par(0, (S5_CHANNELS, 2 * ns)), par(1, (S5_CHANNELS, 2 * ns)),
                pl.BlockSpec((1, 2 * ns, S5_CHANNELS), lambda a, p, sq, fi, la: (a, 0, 0)),
                par(0, (lam_rows, ns)), par(1, (lam_rows, ns)),
                st(0, fwd), st(1, bwd)],
            out_specs=[tok(fwd), tok(bwd), fin(fwd), fin(bwd)],
            scratch_shapes=[pltpu.VMEM((TOKEN_BLOCK, 2 * ns), F32)] * 2
            + [pltpu.VMEM((TOKEN_BLOCK, 2 * ns), BF16)] * 2
            + [pltpu.VMEM((S5_CHANNELS // LANES, SUBLANES * S5_PITCH, LANES), F32)] * 2
            + [pltpu.VMEM((2, ns), F32)] * 2),
        compiler_params=_cparams(2),
        name="s5_scan",
    )(seq, first, last, u, u, jnp.asarray(_S5_PERM, BF16), bd, bd, cd, lam, lam, h0, h0)


def _s5_params(lam_re, lam_im, log_step, b_re, b_im, c_re, c_im):
    g, p, c = S5_GROUPS, S5_STATE, S5_GROUP
    gl = S5_CHANNELS // c
    lr = jnp.minimum(lam_re, -1e-4)
    li = lam_im
    step = jnp.exp(log_step)[:, None]
    mag = jnp.exp(lr * step)
    ar, ai = mag * jnp.cos(li * step), mag * jnp.sin(li * step)
    magk = jnp.exp(lr * step * S5_SEG)
    akr, aki = magk * jnp.cos(li * step * S5_SEG), magk * jnp.sin(li * step * S5_SEG)
    den = lr * lr + li * li
    cr = ((ar - 1.0) * lr + ai * li) / den
    ci = (ai * lr - (ar - 1.0) * li) / den
    bbr = cr[:, :, None] * b_re - ci[:, :, None] * b_im
    bbi = cr[:, :, None] * b_im + ci[:, :, None] * b_re
    eye = jnp.eye(gl, dtype=F32)

    def bmat(bb):
        bb = bb.reshape(S5_COLBLOCKS, gl, p, c)
        return jnp.einsum('agpc,gh->agchp', bb, eye).reshape(S5_COLBLOCKS, gl * c, gl * p)

    def cmat(cc):
        cc = cc.reshape(S5_COLBLOCKS, gl, c, p)
        return jnp.einsum('agcp,gh->agphc', cc, eye).reshape(S5_COLBLOCKS, gl * p, gl * c)

    bd = jnp.concatenate([bmat(bbr), bmat(bbi)], axis=2).astype(BF16)
    cd = jnp.concatenate([cmat(c_re), -cmat(c_im)], axis=1).astype(BF16)

    def lanes(v, rows):
        v = v.reshape(S5_COLBLOCKS, 1, gl * p)
        return jnp.broadcast_to(v, (S5_COLBLOCKS, rows, gl * p))

    lam = jnp.concatenate([lanes(ar, SUBLANES), lanes(ai, SUBLANES), lanes(akr, 1), lanes(aki, 1)], axis=1)
    return bd, cd, lam


def _hg_body(rev, jmap_ref, seq_ref, first_ref, last_ref,
             q_ref, f_ref, v_ref, lb_ref, h0_ref, o_ref, s_ref):
    p = pl.program_id(0)
    c_len = HG_CHUNK

    @pl.when(first_ref[p] == 1)
    def _():
        s_ref[...] = h0_ref[...]

    sub = HG_SUB
    n_sub = c_len // sub
    ri = lax.broadcasted_iota(jnp.int32, (c_len, c_len), 0)
    ci = lax.broadcasted_iota(jnp.int32, (c_len, c_len), 1)
    mask = (ri <= ci) if rev else (ri >= ci)
    tri = mask.astype(BF16)
    rb, cb = ri // sub, ci // sub
    diag_mask = mask & (rb == cb)
    off_mask = (cb > rb) if rev else (cb < rb)
    lb = lb_ref[...]
    tot_row = 0 if rev else c_len - 1
    n_chunks = TOKEN_BLOCK // c_len
    chunk_order = range(n_chunks - 1, -1, -1) if rev else range(n_chunks)
    for c in chunk_order:
        rows = pl.ds(c * c_len, c_len)
        qin = q_ref[rows, :].astype(F32)
        q = qin * jax.nn.sigmoid(qin) * (HG_DK ** -0.5)
        f = lb + (1.0 - lb) * jax.nn.sigmoid(f_ref[rows, :])
        k = 1.0 - f
        g = jnp.log(f)
        g0 = g.astype(BF16)
        r1 = g - g0.astype(F32)
        g1 = r1.astype(BF16)
        g2 = (r1 - g1.astype(F32)).astype(BF16)
        b = (jnp.dot(tri, g0, preferred_element_type=F32) + jnp.dot(tri, g1, preferred_element_type=F32)
             + jnp.dot(tri, g2, preferred_element_type=F32))
        btot = b[tot_row:tot_row + 1, :]
        bmid = jnp.concatenate(
            [jnp.broadcast_to(b[i * sub + sub // 2:i * sub + sub // 2 + 1, :], (sub, HG_KW)) for i in range(n_sub)], axis=0)
        qd = (q * jnp.exp(b - bmid)).astype(BF16)
        kd = (k * jnp.exp(bmid - b)).astype(BF16)
        qo, ko = {}, {}
        for i in range(n_sub):
            edge = (i + 1) * sub if rev else i * sub - 1
            if 0 <= edge < c_len:
                bref = b[edge:edge + 1, :]
                rws = slice(i * sub, (i + 1) * sub)
                qo[i] = (q[rws] * jnp.exp(b[rws] - bref)).astype(BF16)
                ko[i] = (k * jnp.exp(jnp.minimum(bref - b, 0.0))).astype(BF16)
        qs = (q * jnp.exp(b)).astype(BF16)
        ks = (k * jnp.exp(btot - b)).astype(BF16)
        dec = jnp.exp(btot)
        v = v_ref[rows, :]
        vt = v.astype(F32).T.astype(BF16)
        outs = []
        for h in range(HG_HEADS):
            sl = slice(h * HG_DK, (h + 1) * HG_DK)
            near = _dot_nt(qd[:, sl], kd[:, sl])
            far = jnp.concatenate(
                [_dot_nt(qo[i][:, sl], ko[i][:, sl]) if i in qo else jnp.zeros((sub, c_len), F32)
                 for i in range(n_sub)], axis=0)
            att = jnp.where(diag_mask, near, jnp.where(off_mask, far, 0.0)).astype(BF16)
            s_old = s_ref[0, h]
            o_h = jnp.dot(att, v[:, sl], preferred_element_type=F32) + _dot_nt(qs[:, sl], s_old.astype(BF16))
            s_ref[0, h] = s_old * dec[:, sl] + jnp.dot(vt[sl, :], ks[:, sl], preferred_element_type=F32)
            outs.append(o_h)
        o_ref[rows, :] = jnp.concatenate(outs, axis=1)


def _hg_scan(rev, tables, q, f, v, lb, h0):
    jmap, seq, first, last = tables
    n_tok = q.shape[0]
    nblk = n_tok // TOKEN_BLOCK
    n_seq = h0.shape[0]
    d = 1 if rev else 0
    tok = pl.BlockSpec((TOKEN_BLOCK, HG_WIDTH), lambda p, jm, sq, fi, la: (jm[p], 0))
    st = pl.BlockSpec((1, HG_HEADS, HG_DV, HG_DK), lambda p, jm, sq, fi, la: (sq[p], 0, 0, 0))
    return pl.pallas_call(
        functools.partial(_hg_body, rev),
        out_shape=[jax.ShapeDtypeStruct((n_tok, HG_WIDTH), F32),
                   jax.ShapeDtypeStruct((n_seq, HG_HEADS, HG_DV, HG_DK), F32)],
        grid_spec=pltpu.PrefetchScalarGridSpec(
            num_scalar_prefetch=4, grid=(nblk,),
            in_specs=[tok,
                      pl.BlockSpec((TOKEN_BLOCK, HG_KW), lambda p, jm, sq, fi, la: (jm[p], d)),
                      tok,
                      pl.BlockSpec((1, HG_KW), lambda p, jm, sq, fi, la: (0, 0)),
                      st],
            out_specs=[tok, st]),
        compiler_params=_cparams(1),
        name="hg_bwd" if rev else "hg_fwd",
    )(jmap, seq, first, last, q, f, v, lb, h0)


def _merge_body(rowmap_ref, x_ref, ysf_ref, ysb_ref, u_ref, ohf_ref, ohb_ref, g_ref, gs_ref, gh_ref,
                mod_ref, d_ref, bglu_ref, hnw_ref, n2w_ref, wglu_ref, wbs_ref, wbh_ref, wout_ref,
                rhi_ref, rlo_ref, xm_ref, h2_ref, sc_ref):
    row = rowmap_ref[pl.program_id(0)]
    mrow = lambda n: mod_ref[pl.ds(row, 1), n * D_MODEL:(n + 1) * D_MODEL]
    gate1, shift2, scale2 = mrow(2), mrow(3), mrow(4)

    y = ysf_ref[...] + ysb_ref[...] + d_ref[...] * u_ref[...]
    gl = jax.nn.gelu(y)
    z = jnp.dot(gl.astype(BF16), wglu_ref[...], preferred_element_type=F32) + bglu_ref[...]
    y_s5 = gl * jax.nn.sigmoid(z)

    o = ohf_ref[...] + ohb_ref[...]
    normed = []
    for h in range(HG_HEADS):
        oh = o[:, h * HG_DV:(h + 1) * HG_DV]
        normed.append(oh * lax.rsqrt(jnp.mean(oh * oh, axis=-1, keepdims=True) + RMS_EPS))
    gout = g_ref[...].astype(F32)
    y_hg = jnp.concatenate(normed, axis=1) * hnw_ref[...] * (gout * jax.nn.sigmoid(gout))

    merged = (jax.nn.sigmoid(gs_ref[...].astype(F32)) * jnp.dot(y_s5.astype(BF16), wbs_ref[...], preferred_element_type=F32)
              + jax.nn.sigmoid(gh_ref[...].astype(F32)) * jnp.dot(y_hg.astype(BF16), wbh_ref[...], preferred_element_type=F32))
    xm = x_ref[...] + gate1 * jnp.dot(merged.astype(BF16), wout_ref[...], preferred_element_type=F32)
    xm_ref[...] = xm
    h2 = _rms_modulate(xm, n2w_ref[...], scale2, shift2)
    h2_ref[...] = h2
    h_hi, h_lo = _split_bf16(h2)
    logits = _dot_nt(rhi_ref[...], h_hi) + _dot_nt(rhi_ref[...], h_lo) + _dot_nt(rlo_ref[...], h_hi)
    sc_ref[...] = jax.nn.sigmoid(logits)


def _merge(rowmap, x, ysf, ysb, u, ohf, ohb, g, gs, gh, mod, d, bglu, hnw, n2w,
           wglu, wbs, wbh, wout, rhi, rlo):
    n_tok = x.shape[0]
    nblk = n_tok // DENSE_BLOCK
    tok = lambda w: pl.BlockSpec((DENSE_BLOCK, w), lambda j, rm: (j, 0))
    full = lambda shape: pl.BlockSpec(shape, lambda j, rm: (0,) * len(shape))
    return pl.pallas_call(
        _merge_body,
        out_shape=[jax.ShapeDtypeStruct((n_tok, D_MODEL), F32),
                   jax.ShapeDtypeStruct((n_tok, D_MODEL), F32),
                   jax.ShapeDtypeStruct((N_EXPERTS, n_tok), F32)],
        grid_spec=pltpu.PrefetchScalarGridSpec(
            num_scalar_prefetch=1, grid=(nblk,),
            in_specs=[tok(D_MODEL), tok(S5_WIDTH), tok(S5_WIDTH), tok(S5_WIDTH), tok(HG_WIDTH), tok(HG_WIDTH),
                      tok(HG_WIDTH), tok(D_MODEL), tok(D_MODEL),
                      full((SUBLANES, 6 * D_MODEL)), full((1, S5_WIDTH)), full((1, S5_WIDTH)),
                      full((1, HG_WIDTH)), full((1, D_MODEL)),
                      full((S5_WIDTH, S5_WIDTH)), full((S5_WIDTH, D_MODEL)), full((HG_WIDTH, D_MODEL)),
                      full((D_MODEL, D_MODEL)), full((N_EXPERTS, D_MODEL)), full((N_EXPERTS, D_MODEL))],
            out_specs=[tok(D_MODEL), tok(D_MODEL),
                       pl.BlockSpec((N_EXPERTS, DENSE_BLOCK), lambda j, rm: (0, j))]),
        compiler_params=_cparams(1),
        name="merge",
    )(rowmap, x, ysf, ysb, u, ohf, ohb, g, gs, gh, mod, d, bglu, hnw, n2w, wglu, wbs, wbh, wout, rhi, rlo)


def _experts_body(has_prev, be_ref, nused_ref, x_ref, wg_ref, wu_ref, wd_ref, *rest):
    y_ref, wgb_ref, wub_ref, wdb_ref = rest[1:] if has_prev else rest
    b = pl.program_id(0)
    e = be_ref[b]
    prev = be_ref[jnp.maximum(b - 1, 0)]

    @pl.when((b == 0) | (e != prev))
    def _():
        wgb_ref[...] = wg_ref[0, 0].astype(BF16)
        wub_ref[...] = wu_ref[0, 0].astype(BF16)
        wdb_ref[...] = wd_ref[0, 0].astype(BF16)

    @pl.when(b < nused_ref[0])
    def _():
        x = x_ref[...].astype(BF16)
        gate = jnp.dot(x, wgb_ref[...], preferred_element_type=F32)
        up = jnp.dot(x, wub_ref[...], preferred_element_type=F32)
        hid = (gate * jax.nn.sigmoid(gate) * up).astype(BF16)
        y_ref[...] = jnp.dot(hid, wdb_ref[...], preferred_element_type=F32)

    @pl.when(b >= nused_ref[0])
    def _():
        y_ref[...] = jnp.zeros_like(y_ref)


def _experts(layer, first_blk, n_rows_total, blk_expert, nused, xb, w_gate, w_up, w_down, y_prev):
    nblk = xb.shape[0] // MOE_ROWS
    wspec = pl.BlockSpec((1, 1, D_MODEL, D_MODEL), lambda b, be, nu: (layer, be[b], 0, 0))
    has_prev = y_prev is not None
    in_specs = [pl.BlockSpec((MOE_ROWS, D_MODEL), lambda b, be, nu: (b, 0)), wspec, wspec, wspec]
    args = [blk_expert, nused, xb, w_gate, w_up, w_down]
    if has_prev:
        in_specs.append(pl.BlockSpec(memory_space=pl.ANY))
        args.append(y_prev)
    return pl.pallas_call(
        functools.partial(_experts_body, has_prev),
        out_shape=jax.ShapeDtypeStruct((n_rows_total, D_MODEL), F32),
        grid_spec=pltpu.PrefetchScalarGridSpec(
            num_scalar_prefetch=2, grid=(nblk,),
            in_specs=in_specs,
            out_specs=pl.BlockSpec((MOE_ROWS, D_MODEL), lambda b, be, nu: (first_blk + b, 0)),
            scratch_shapes=[pltpu.VMEM((D_MODEL, D_MODEL), BF16)] * 3),
        input_output_aliases={len(args) - 1: 0} if has_prev else {},
        compiler_params=_cparams(1),
        name="experts",
    )(*args)


def _first_max(vals):
    best, idx = vals[0], jnp.zeros(vals[0].shape, jnp.int32)
    for j in range(1, len(vals)):
        upd = vals[j] > best
        idx = jnp.where(upd, j, idx)
        best = jnp.where(upd, vals[j], best)
    return best, idx


def _pick(idx, vals):
    out = vals[-1]
    for j in range(len(vals) - 2, -1, -1):
        out = jnp.where(idx == j, vals[j], out)
    return out


def _router_body(sc_ref, rb_ref, eid_ref, wts_ref, rank_ref, cnt_ref, base_ref):
    j = pl.program_id(0)

    @pl.when(j == 0)
    def _():
        base_ref[...] = jnp.zeros_like(base_ref)

    s = sc_ref[...]
    sel = s + rb_ref[...]
    srow = [s[e:e + 1, :] for e in range(N_EXPERTS)]
    brow = [sel[e:e + 1, :] for e in range(N_EXPERTS)]
    epg = EXPERTS_PER_GROUP
    gscore = []
    for g in range(N_GROUPS):
        x = brow[g * epg:(g + 1) * epg]
        pairs = [x[a] + x[b] for a in range(epg) for b in range(a + 1, epg)]
        gscore.append(functools.reduce(jnp.maximum, pairs))
    _, bestg = _first_max(gscore)
    y = [_pick(bestg, [brow[g * epg + i] for g in range(N_GROUPS)]) for i in range(epg)]
    sy = [_pick(bestg, [srow[g * epg + i] for g in range(N_GROUPS)]) for i in range(epg)]
    _, i1 = _first_max(y)
    v2 = jnp.full(y[0].shape, -jnp.inf, F32)
    i2 = jnp.zeros(y[0].shape, jnp.int32)
    for i in range(epg):
        upd = (i1 != i) & (y[i] > v2)
        i2 = jnp.where(upd, i, i2)
        v2 = jnp.where(upd, y[i], v2)
    e0 = bestg * epg + i1
    e1 = bestg * epg + i2
    w0 = _pick(i1, sy)
    w1 = _pick(i2, sy)
    wsum = w0 + w1
    eid_ref[0:1, :] = e0
    eid_ref[1:2, :] = e1
    wts_ref[0:1, :] = w0 / wsum
    wts_ref[1:2, :] = w1 / wsum

    eiota = lax.broadcasted_iota(jnp.int32, s.shape, 0)
    m0 = eiota == e0
    m1 = eiota == e1
    member = (m0 | m1).astype(BF16)
    tb = s.shape[1]
    before = (lax.broadcasted_iota(jnp.int32, (tb, tb), 0) < lax.broadcasted_iota(jnp.int32, (tb, tb), 1)).astype(BF16)
    pos = base_ref[...] + jnp.dot(member, before, preferred_element_type=F32)
    rank_ref[0:1, :] = jnp.sum(jnp.where(m0, pos, 0.0), axis=0, keepdims=True).astype(jnp.int32)
    rank_ref[1:2, :] = jnp.sum(jnp.where(m1, pos, 0.0), axis=0, keepdims=True).astype(jnp.int32)
    base = base_ref[...] + jnp.sum(member.astype(F32), axis=1, keepdims=True)
    base_ref[...] = base
    cnt_ref[...] = jnp.broadcast_to(base, cnt_ref.shape).astype(jnp.int32)


def _router(scores_t, router_b):
    n_tok = scores_t.shape[1]
    tok = lambda rows: pl.BlockSpec((rows, DENSE_BLOCK), lambda j: (0, j))
    return pl.pallas_call(
        _router_body,
        out_shape=[jax.ShapeDtypeStruct((TOP_K, n_tok), jnp.int32), jax.ShapeDtypeStruct((TOP_K, n_tok), F32),
                   jax.ShapeDtypeStruct((TOP_K, n_tok), jnp.int32), jax.ShapeDtypeStruct((N_EXPERTS, LANES), jnp.int32)],
        grid=(n_tok // DENSE_BLOCK,),
        in_specs=[tok(N_EXPERTS), pl.BlockSpec((N_EXPERTS, 1), lambda j: (0, 0))],
        out_specs=[tok(TOP_K), tok(TOP_K), tok(TOP_K), pl.BlockSpec((N_EXPERTS, LANES), lambda j: (0, 0))],
        scratch_shapes=[pltpu.VMEM((N_EXPERTS, 1), F32)],
        compiler_params=_cparams(1),
        name="router",
    )(scores_t, router_b.reshape(N_EXPERTS, 1))


def _dispatch_plan(eid, rank, counts):
    n_tok = eid.shape[1]
    n_slots = n_tok * TOP_K
    padded = (counts + MOE_ROWS - 1) // MOE_ROWS * MOE_ROWS
    pad_end = jnp.cumsum(padded)
    pad_start = pad_end - padded
    e_ids = jnp.arange(N_EXPERTS, dtype=jnp.int32)
    dest = jnp.sum(jnp.where(eid[..., None] == e_ids, pad_start, 0), axis=-1) + rank
    n_blocks = -(-n_slots // MOE_ROWS) + N_EXPERTS
    n_rows = n_blocks * MOE_ROWS
    blk_start = jnp.arange(n_blocks, dtype=jnp.int32) * MOE_ROWS
    blk_expert = jnp.minimum(jnp.sum(pad_end[None, :] <= blk_start[:, None], axis=-1), N_EXPERTS - 1).astype(jnp.int32)
    nused = (pad_end[-1:] // MOE_ROWS).astype(jnp.int32)
    n_fill = n_rows - n_slots
    gap = padded - counts
    gap_before = jnp.cumsum(gap) - gap
    j = jnp.arange(n_fill, dtype=jnp.int32)
    in_gap = (j[:, None] >= gap_before[None, :]) & (j[:, None] < (gap_before + gap)[None, :])
    fill_pos = jnp.where(jnp.any(in_gap, axis=-1),
                         jnp.sum(jnp.where(in_gap, (pad_start + counts - gap_before)[None, :], 0), axis=-1) + j,
                         pad_end[-1] - jnp.sum(gap) + j)
    keys = jnp.concatenate([dest.reshape(-1), fill_pos])
    vals = jnp.concatenate([jnp.tile(jnp.arange(n_tok, dtype=jnp.int32), TOP_K), j % n_tok])
    _, row_tok = lax.sort_key_val(keys, vals)
    return dest, blk_expert, nused, row_tok


def _final_body(n_r, rowmap_ref, x_ref, *refs):
    r_refs, (gate_ref, nw_ref, o_ref) = refs[:n_r], refs[n_r:]
    row = rowmap_ref[pl.program_id(0)]
    xn = x_ref[...] + gate_ref[pl.ds(row, 1), :] * _branch_update(r_refs)
    ms = jnp.mean(xn * xn, axis=-1, keepdims=True)
    o_ref[...] = xn * lax.rsqrt(ms + RMS_EPS) * nw_ref[...]


def _final(rowmap, x, r, gate_rows, nw):
    n_tok = x.shape[0]
    tok = lambda w: pl.BlockSpec((DENSE_BLOCK, w), lambda j, rm: (j, 0))
    full = lambda shape: pl.BlockSpec(shape, lambda j, rm: (0,) * len(shape))
    return pl.pallas_call(
        functools.partial(_final_body, len(r)),
        out_shape=jax.ShapeDtypeStruct((n_tok, D_MODEL), F32),
        grid_spec=pltpu.PrefetchScalarGridSpec(
            num_scalar_prefetch=1, grid=(n_tok // DENSE_BLOCK,),
            in_specs=[tok(D_MODEL)] + [tok(a.shape[1]) for a in r] + [full((SUBLANES, D_MODEL)), full((1, D_MODEL))],
            out_specs=tok(D_MODEL)),
        compiler_params=_cparams(1),
        name="final_norm",
    )(rowmap, x, *r, gate_rows, nw)


def _grid_pos_embed(rows):
    r = jnp.repeat(jnp.arange(rows, dtype=F32), GRID_W)
    col = jnp.tile(jnp.arange(GRID_W, dtype=F32), rows)
    quarter = D_MODEL // 4
    omega = 1.0 / (10000.0 ** (jnp.arange(quarter, dtype=F32) / quarter))

    def axis_embed(pos):
        ang = pos[:, None] * omega[None, :]
        return jnp.concatenate([jnp.sin(ang), jnp.cos(ang)], axis=-1)

    return jnp.concatenate([axis_embed(r), axis_embed(col)], axis=-1)


def kernel(x_prompt, x_sample, state_s5_re, state_s5_im, state_hg, c, c_ctx, w_ada, b_ada, norm1_w, norm2_w, w_in, s5_lam_re, s5_lam_im, s5_log_step, s5_b_re, s5_b_im, s5_c_re, s5_c_im, s5_d, s5_w_glu, s5_b_glu, hg_lb_logits, hg_norm_w, w_br_s5, w_br_hg, w_out, router_w, router_b, moe_w_gate, moe_w_up, moe_w_down, norm_f_w):
    n_ctx, ctx_len, _ = x_prompt.shape
    n_lat, lat_len, _ = x_sample.shape
    n_ctx_tok = n_ctx * ctx_len
    n_seq = n_ctx + n_lat
    assert n_lat + 1 <= SUBLANES and ctx_len % TOKEN_BLOCK == 0 and lat_len % DENSE_BLOCK == 0
    assert (n_ctx * ctx_len) % DENSE_BLOCK == 0

    seq_np, first_np, last_np, modrow_np = _block_tables(n_ctx, ctx_len, n_lat, lat_len)
    nblk = seq_np.shape[0]
    jmap_f = np.arange(nblk, dtype=np.int32)
    jmap_b = jmap_f[::-1].copy()
    tab_f = tuple(jnp.asarray(t) for t in (jmap_f, seq_np, first_np, last_np))
    tab_b = tuple(jnp.asarray(t) for t in (jmap_b, seq_np[::-1].copy(), last_np[::-1].copy(), first_np[::-1].copy()))
    tab_s5 = tuple(jnp.asarray(t) for t in (seq_np, first_np, last_np))
    rowmap = jnp.asarray(modrow_np.reshape(-1, DENSE_BLOCK // TOKEN_BLOCK)[:, 0])

    cvecs = jnp.zeros((SUBLANES, D_MODEL), F32).at[0].set(c_ctx).at[1:1 + n_lat].set(c)
    mod_all = _adaln(cvecs, w_ada, b_ada)

    probs = jax.nn.softmax(hg_lb_logits.astype(F32), axis=0)
    lower_bounds = jnp.cumsum(probs, axis=0) - probs[0]

    pos = _grid_pos_embed(lat_len // GRID_W)
    x = jnp.concatenate([x_prompt.reshape(n_ctx_tok, D_MODEL), x_sample.reshape(n_lat * lat_len, D_MODEL)], axis=0)
    resid = (jnp.concatenate([jnp.zeros((n_ctx_tok, D_MODEL), F32), jnp.tile(pos, (n_lat, 1))], axis=0),)
    gate_rows = jnp.ones((SUBLANES, D_MODEL), F32)

    router_t = router_w.T.astype(F32)
    r_hi, r_lo = _split_bf16(router_t)
    ns = S5_BLOCK_STATES
    zeros_s5 = jnp.zeros((n_ctx, S5_COLBLOCKS, 2, ns), F32)
    zeros_hg = jnp.zeros((n_ctx, HG_HEADS, HG_DV, HG_DK), F32)

    new_s5_re, new_s5_im, new_hg = [], [], []
    for l in range(DEPTH):
        mod = mod_all[l]
        x, u, q_in, f_raw, i_in, g_out, gate_s5, gate_hg = _inproj(
            rowmap, x, resid, gate_rows, mod, norm1_w[l].reshape(1, D_MODEL), w_in[l].astype(BF16))

        par = [_s5_params(s5_lam_re[l, d], s5_lam_im[l, d], s5_log_step[l, d], s5_b_re[l], s5_b_im[l], s5_c_re[l], s5_c_im[l])
               for d in range(2)]
        h0_s5 = jnp.stack([
            jnp.concatenate([zeros_s5, jnp.stack([state_s5_re[:, l, d], state_s5_im[:, l, d]], axis=1).astype(F32).reshape(
                n_lat, 2, S5_COLBLOCKS, ns).swapaxes(1, 2)], axis=0) for d in range(2)])
        y_f, y_b, fin_f, fin_b = _s5_scan(tab_s5, u, jnp.stack([par[0][0], par[1][0]]), par[0][1],
                                          jnp.stack([par[0][2], par[1][2]]), h0_s5)
        ys = [y_f, y_b]
        s5_fin_re = [fin[:n_ctx, :, 0].reshape(n_ctx, S5_GROUPS, S5_STATE) for fin in (fin_f, fin_b)]
        s5_fin_im = [fin[:n_ctx, :, 1].reshape(n_ctx, S5_GROUPS, S5_STATE) for fin in (fin_f, fin_b)]

        os_, hg_fin = [], []
        for d, tabs in enumerate((tab_f, tab_b)):

            h0 = jnp.concatenate([zeros_hg, jnp.swapaxes(state_hg[:, l, d].astype(F32), -1, -2)], axis=0)
            o_d, hfin = _hg_scan(bool(d), tabs, q_in, f_raw, i_in, lower_bounds[l, d].reshape(1, HG_KW), h0)
            os_.append(o_d)
            hg_fin.append(jnp.swapaxes(hfin[:n_ctx], -1, -2))
        new_s5_re.append(jnp.stack(s5_fin_re, axis=1))
        new_s5_im.append(jnp.stack(s5_fin_im, axis=1))
        new_hg.append(jnp.stack(hg_fin, axis=1))

        x, h2, scores_t = _merge(
            rowmap, x, ys[0], ys[1], u, os_[0], os_[1], g_out, gate_s5, gate_hg, mod,
            s5_d[l].reshape(1, S5_WIDTH), s5_b_glu[l].reshape(1, S5_WIDTH), hg_norm_w[l].reshape(1, HG_WIDTH),
            norm2_w[l].reshape(1, D_MODEL), s5_w_glu[l].astype(BF16), w_br_s5[l].astype(BF16),
            w_br_hg[l].astype(BF16), w_out[l].astype(BF16), r_hi, r_lo)

        eid, wts, rank, counts = _router(scores_t, router_b.astype(F32))
        dest, blk_expert, nused, row_tok = _dispatch_plan(eid, rank, counts[:, 0])
        n_blocks = blk_expert.shape[0]
        cb = n_blocks // MOE_CHUNKS
        yb = None
        for ch in range(MOE_CHUNKS):
            xb = h2[row_tok[ch * cb * MOE_ROWS:(ch + 1) * cb * MOE_ROWS]]
            yb = _experts(l, ch * cb, n_blocks * MOE_ROWS, blk_expert[ch * cb:(ch + 1) * cb],
                          jnp.clip(nused - ch * cb, 0, cb), xb, moe_w_gate, moe_w_up, moe_w_down, yb)
        resid = (yb[dest[0]], yb[dest[1]], wts.T)
        gate_rows = mod[:, 5 * D_MODEL:6 * D_MODEL]

    y = _final(rowmap, x, resid, gate_rows, norm_f_w.reshape(1, D_MODEL))
    y_prompt = y[:n_ctx_tok].reshape(n_ctx, ctx_len, D_MODEL)
    y_sample = y[n_ctx_tok:].reshape(n_lat, lat_len, D_MODEL)
    new_state_s5_re = jnp.stack(new_s5_re, axis=1).astype(state_s5_re.dtype)
    new_state_s5_im = jnp.stack(new_s5_im, axis=1).astype(state_s5_im.dtype)
    new_state_hg = jnp.stack(new_hg, axis=1).astype(state_hg.dtype)
    return (y_prompt, y_sample, new_state_s5_re, new_state_s5_im, new_state_hg)
```

```python
import functools
import math

import jax
import jax.numpy as jnp
import numpy as np
from jax import lax
from jax.experimental import pallas as pl
from jax.experimental.pallas import tpu as pltpu

F32 = jnp.float32
BF16 = jnp.bfloat16

D_MODEL = 1024
DEPTH = 4
GRID_W = 64
RMS_EPS = 1e-6
S5_WIDTH = 512
S5_GROUP = 16
S5_GROUPS = 32
S5_STATE = 64
HG_HEADS = 4
HG_DK = 128
HG_DV = 128
HG_WIDTH = 512
HG_KW = 512
HG_CHUNK = 64
HG_SUB = 16
N_EXPERTS = 16
N_GROUPS = 4
EXPERTS_PER_GROUP = 4
TOP_K = 2
IN_COLS = 5120

LANES = 128
SUBLANES = 8
VMEM_LIMIT_BYTES = 56 * 1024 * 1024

TOKEN_BLOCK = 256
DENSE_BLOCK = 512
S5_SEG = TOKEN_BLOCK // SUBLANES
MXU_DEPTH = 256
S5_CHANNELS = MXU_DEPTH
S5_COLBLOCKS = S5_WIDTH // S5_CHANNELS
S5_BLOCK_STATES = (S5_CHANNELS // S5_GROUP) * S5_STATE
S5_PITCH = S5_SEG + 4
_S5_PERM = np.zeros((TOKEN_BLOCK, TOKEN_BLOCK), np.float32)
_S5_PERM[np.arange(TOKEN_BLOCK), (np.arange(TOKEN_BLOCK) % SUBLANES) * S5_SEG + np.arange(TOKEN_BLOCK) // SUBLANES] = 1.0
MOE_ROWS = 256
MOE_CHUNKS = 4


def _block_tables(n_ctx_seq, ctx_len, n_lat_seq, lat_len):
    seq, first, last, modrow = [], [], [], []
    for s in range(n_ctx_seq):
        nb = ctx_len // TOKEN_BLOCK
        for b in range(nb):
            seq.append(s); first.append(int(b == 0)); last.append(int(b == nb - 1)); modrow.append(0)
    for s in range(n_lat_seq):
        nb = lat_len // TOKEN_BLOCK
        for b in range(nb):
            seq.append(n_ctx_seq + s); first.append(int(b == 0)); last.append(int(b == nb - 1))
            modrow.append(1 + s)
    return (np.asarray(seq, np.int32), np.asarray(first, np.int32),
            np.asarray(last, np.int32), np.asarray(modrow, np.int32))


def _cparams(n_axes):
    return pltpu.CompilerParams(dimension_semantics=("arbitrary",) * n_axes,
                                vmem_limit_bytes=VMEM_LIMIT_BYTES)


def _split_bf16(a):
    hi = a.astype(BF16)
    lo = (a - hi.astype(F32)).astype(BF16)
    return hi, lo


def _dot_nt(a, b):
    return lax.dot_general(a, b, (((1,), (1,)), ((), ())), preferred_element_type=F32)


def _adaln_body(c_ref, w_ref, b_ref, o_ref):
    a = c_ref[...]
    a = a * jax.nn.sigmoid(a)
    a_hi, a_lo = _split_bf16(a)
    w_hi, w_lo = _split_bf16(w_ref[0])
    acc = jnp.dot(a_hi, w_hi, preferred_element_type=F32)
    acc += jnp.dot(a_lo, w_hi, preferred_element_type=F32)
    acc += jnp.dot(a_hi, w_lo, preferred_element_type=F32)
    o_ref[0] = acc + b_ref[0]


def _adaln(cvecs, w_ada, b_ada):
    tn = 1536
    n_tiles = (6 * D_MODEL) // tn
    return pl.pallas_call(
        _adaln_body,
        out_shape=jax.ShapeDtypeStruct((DEPTH, SUBLANES, 6 * D_MODEL), F32),
        grid=(DEPTH, n_tiles),
        in_specs=[
            pl.BlockSpec((SUBLANES, D_MODEL), lambda l, n: (0, 0)),
            pl.BlockSpec((1, D_MODEL, tn), lambda l, n: (l, 0, n)),
            pl.BlockSpec((1, 1, tn), lambda l, n: (l, 0, n)),
        ],
        out_specs=pl.BlockSpec((1, SUBLANES, tn), lambda l, n: (l, 0, n)),
        compiler_params=_cparams(2),
        name="adaln",
    )(cvecs, w_ada, b_ada.reshape(DEPTH, 1, 6 * D_MODEL))


_IN_SPLITS = ((0, 512), (512, 1024), (1024, 2048), (2048, 2560), (2560, 3072), (3072, 4096), (4096, 5120))


def _rms_modulate(x, nw, scale, shift):
    ms = jnp.mean(x * x, axis=-1, keepdims=True)
    return (x * lax.rsqrt(ms + RMS_EPS)) * nw * (1.0 + scale) + shift


def _branch_update(r_refs):
    if len(r_refs) == 1:
        return r_refs[0][...]
    ya_ref, yb_ref, w_ref = r_refs
    w = w_ref[...]
    return ya_ref[...] * w[:, 0:1] + yb_ref[...] * w[:, 1:2]


def _inproj_body(n_x, n_ctx_blk, n_r, rowmap_ref, *refs):
    x_refs, r_refs = refs[:n_x], refs[n_x:n_x + n_r]
    gate_ref, mod_ref, nw_ref, w_ref, xo_ref, u_ref, q_ref, f_ref, i_ref, g_ref, gs_ref, gh_ref = refs[n_x + n_r:]
    j = pl.program_id(0)
    row = rowmap_ref[j]
    gate = gate_ref[pl.ds(row, 1), :]
    x = x_refs[0][...] if n_x == 1 else jnp.where(j < n_ctx_blk, x_refs[0][...], x_refs[1][...])
    xn = x + gate * _branch_update(r_refs)
    xo_ref[...] = xn
    shift = mod_ref[pl.ds(row, 1), 0:D_MODEL]
    scale = mod_ref[pl.ds(row, 1), D_MODEL:2 * D_MODEL]
    h = _rms_modulate(xn, nw_ref[...], scale, shift).astype(BF16)
    outs = (u_ref, q_ref, f_ref, i_ref, g_ref, gs_ref, gh_ref)
    for ref, (a, b) in zip(outs, _IN_SPLITS):
        ref[...] = jnp.dot(h, w_ref[:, a:b], preferred_element_type=F32).astype(ref.dtype)


def _row_specs(arrays_and_maps):
    return [pl.BlockSpec((DENSE_BLOCK, a.shape[1]), lambda j, rm, f=f: (f(j), 0)) for a, f in arrays_and_maps]


def _same_block(j):
    return j


def _inproj(rowmap, x, r, gate_rows, mod, nw, w_bf16):
    xs = x if isinstance(x, tuple) else (x,)
    n_tok = sum(a.shape[0] for a in xs)
    nblk = n_tok // DENSE_BLOCK
    n_ctx_blk = xs[0].shape[0] // DENSE_BLOCK
    x_maps = [_same_block] if len(xs) == 1 else [lambda j: jnp.minimum(j, n_ctx_blk - 1), lambda j: jnp.maximum(j - n_ctx_blk, 0)]
    tok = lambda w: pl.BlockSpec((DENSE_BLOCK, w), lambda j, rm: (j, 0))
    full = lambda shape: pl.BlockSpec(shape, lambda j, rm: (0,) * len(shape))
    widths = [b - a for a, b in _IN_SPLITS]
    dtypes = [F32, BF16, F32, BF16, BF16, BF16, BF16]
    return pl.pallas_call(
        functools.partial(_inproj_body, len(xs), n_ctx_blk, len(r)),
        out_shape=[jax.ShapeDtypeStruct((n_tok, D_MODEL), F32)]
        + [jax.ShapeDtypeStruct((n_tok, w), dt) for w, dt in zip(widths, dtypes)],
        grid_spec=pltpu.PrefetchScalarGridSpec(
            num_scalar_prefetch=1, grid=(nblk,),
            in_specs=_row_specs(list(zip(xs, x_maps))) + _row_specs(r)
            + [full((SUBLANES, D_MODEL)), full((SUBLANES, 6 * D_MODEL)), full((1, D_MODEL)), full((D_MODEL, IN_COLS))],
            out_specs=[tok(D_MODEL)] + [tok(w) for w in widths]),
        compiler_params=_cparams(1),
        name="inproj",
    )(rowmap, *xs, *[a for a, _ in r], gate_rows, mod, nw, w_bf16)


def _s5_direction(rev, u_ref, perm_ref, bd_ref, cd_ref, lam_ref,
                  y_ref, fin_ref, bu_ref, sb_ref, yp_ref, carry_ref):
    ns = S5_BLOCK_STATES

    up = jnp.dot(perm_ref[...], u_ref[...].astype(BF16), preferred_element_type=F32).astype(BF16)
    bu_ref[...] = jnp.dot(up, bd_ref[0, 0], preferred_element_type=F32)

    lre = lam_ref[0, 0, 0:SUBLANES, :]
    lim = lam_ref[0, 0, SUBLANES:2 * SUBLANES, :]
    order = list(range(S5_SEG))
    if rev:
        order = order[::-1]

    def step(sr, si, k):
        br = bu_ref[pl.ds(SUBLANES * k, SUBLANES), 0:ns]
        bi = bu_ref[pl.ds(SUBLANES * k, SUBLANES), ns:2 * ns]
        return lre * sr - lim * si + br, lre * si + lim * sr + bi

    sr = jnp.zeros((SUBLANES, ns), F32)
    si = jnp.zeros((SUBLANES, ns), F32)
    for k in order:
        sr, si = step(sr, si, k)

    cr = carry_ref[0:1, :]
    ci = carry_ref[1:2, :]
    lkre = lam_ref[0, 0, 2 * SUBLANES:2 * SUBLANES + 1, :]
    lkim = lam_ref[0, 0, 2 * SUBLANES + 1:2 * SUBLANES + 2, :]
    start_r = [None] * SUBLANES
    start_i = [None] * SUBLANES
    seg_order = list(range(SUBLANES))
    if rev:
        seg_order = seg_order[::-1]
    for i in seg_order:
        start_r[i] = cr
        start_i[i] = ci
        er = sr[i:i + 1]
        ei = si[i:i + 1]
        cr, ci = lkre * cr - lkim * ci + er, lkre * ci + lkim * cr + ei
    carry_ref[0:1, :] = cr
    carry_ref[1:2, :] = ci
    fin_ref[0, 0, 0:1, :] = cr
    fin_ref[0, 0, 1:2, :] = ci

    sr = jnp.concatenate(start_r, axis=0)
    si = jnp.concatenate(start_i, axis=0)
    for n in range(0, S5_SEG, 2):
        k0, k1 = order[n], order[n + 1]
        sr0, si0 = step(sr, si, k0)
        sr, si = step(sr0, si0, k1)
        lo_r, hi_r = (sr0, sr) if k0 < k1 else (sr, sr0)
        lo_i, hi_i = (si0, si) if k0 < k1 else (si, si0)
        base = SUBLANES * min(k0, k1)
        sb_ref[pl.ds(base, 2 * SUBLANES), 0:ns] = jnp.concatenate([lo_r, hi_r], axis=0).astype(BF16)
        sb_ref[pl.ds(base, 2 * SUBLANES), ns:2 * ns] = jnp.concatenate([lo_i, hi_i], axis=0).astype(BF16)

    y = jnp.dot(sb_ref[...], cd_ref[0], preferred_element_type=F32)
    for h in range(S5_CHANNELS // LANES):
        for k in range(S5_SEG):
            yp_ref[h, pl.ds(k, SUBLANES, stride=S5_PITCH), :] = y[SUBLANES * k:SUBLANES * (k + 1), h * LANES:(h + 1) * LANES]
        for i in range(SUBLANES):
            y_ref[i * S5_SEG:(i + 1) * S5_SEG, h * LANES:(h + 1) * LANES] = yp_ref[h, pl.ds(i * S5_PITCH, S5_SEG), :]


def _s5_body(seq_ref, first_ref, last_ref,
             uf_ref, ub_ref, perm_ref, bdf_ref, bdb_ref, cd_ref, lamf_ref, lamb_ref, h0f_ref, h0b_ref,
             yf_ref, yb_ref, finf_ref, finb_ref,
             buf_ref, bub_ref, sbf_ref, sbb_ref, ypf_ref, ypb_ref, carryf_ref, carryb_ref):
    p = pl.program_id(1)
    q = pl.num_programs(1) - 1 - p

    @pl.when(first_ref[p] == 1)
    def _():
        carryf_ref[...] = h0f_ref[0, 0, 0]

    @pl.when(last_ref[q] == 1)
    def _():
        carryb_ref[...] = h0b_ref[0, 0, 0]

    _s5_direction(False, uf_ref, perm_ref, bdf_ref, cd_ref, lamf_ref,
                  yf_ref, finf_ref, buf_ref, sbf_ref, ypf_ref, carryf_ref)
    _s5_direction(True, ub_ref, perm_ref, bdb_ref, cd_ref, lamb_ref,
                  yb_ref, finb_ref, bub_ref, sbb_ref, ypb_ref, carryb_ref)


def _s5_scan(tables, u, bd, cd, lam, h0):
    seq, first, last = tables
    n_tok = u.shape[0]
    nblk = n_tok // TOKEN_BLOCK
    n_seq = h0.shape[1]
    ns = S5_BLOCK_STATES
    lam_rows = lam.shape[2]
    grid = (S5_COLBLOCKS, nblk)
    fwd = lambda a, p, sq, fi, la: p
    bwd = lambda a, p, sq, fi, la: nblk - 1 - p
    tok = lambda blk: pl.BlockSpec((TOKEN_BLOCK, S5_CHANNELS), lambda a, p, sq, fi, la: (blk(a, p, sq, fi, la), a))
    par = lambda d, shape: pl.BlockSpec((1, 1) + shape, lambda a, p, sq, fi, la: (d, a, 0, 0))
    st = lambda d, blk: pl.BlockSpec((1, 1, 1, 2, ns), lambda a, p, sq, fi, la: (d, sq[blk(a, p, sq, fi, la)], a, 0, 0))
    fin = lambda blk: pl.BlockSpec((1, 1, 2, ns), lambda a, p, sq, fi, la: (sq[blk(a, p, sq, fi, la)], a, 0, 0))
    fin_shape = jax.ShapeDtypeStruct((n_seq, S5_COLBLOCKS, 2, ns), F32)
    return pl.pallas_call(
        _s5_body,
        out_shape=[jax.ShapeDtypeStruct((n_tok, S5_WIDTH), F32), jax.ShapeDtypeStruct((n_tok, S5_WIDTH), F32),
                   fin_shape, fin_shape],
        grid_spec=pltpu.PrefetchScalarGridSpec(
            num_scalar_prefetch=3, grid=grid,
            in_specs=[
                tok(fwd), tok(bwd),
                pl.BlockSpec((TOKEN_BLOCK, TOKEN_BLOCK), lambda a, p, sq, fi, la: (0, 0)),
                par(0, (S5_CHANNELS, 2 * ns)), par(1, (S5_CHANNELS, 2 * ns)),
                pl.BlockSpec((1, 2 * ns, S5_CHANNELS), lambda a, p, sq, fi, la: (a, 0, 0)),
                par(0, (lam_rows, ns)), par(1, (lam_rows, ns)),
                st(0, fwd), st(1, bwd)],
            out_specs=[tok(fwd), tok(bwd), fin(fwd), fin(bwd)],
            scratch_shapes=[pltpu.VMEM((TOKEN_BLOCK, 2 * ns), F32)] * 2
            + [pltpu.VMEM((TOKEN_BLOCK, 2 * ns), BF16)] * 2
            + [pltpu.VMEM((S5_CHANNELS // LANES, SUBLANES * S5_PITCH, LANES), F32)] * 2
            + [pltpu.VMEM((2, ns), F32)] * 2),
        compiler_params=_cparams(2),
        name="s5_scan",
    )(seq, first, last, u, u, jnp.asarray(_S5_PERM, BF16), bd, bd, cd, lam, lam, h0, h0)


def _s5_params(lam_re, lam_im, log_step, b_re, b_im, c_re, c_im):
    g, p, c = S5_GROUPS, S5_STATE, S5_GROUP
    gl = S5_CHANNELS // c
    lr = jnp.minimum(lam_re, -1e-4)
    li = lam_im
    step = jnp.exp(log_step)[:, None]
    mag = jnp.exp(lr * step)
    ar, ai = mag * jnp.cos(li * step), mag * jnp.sin(li * step)
    magk = jnp.exp(lr * step * S5_SEG)
    akr, aki = magk * jnp.cos(li * step * S5_SEG), magk * jnp.sin(li * step * S5_SEG)
    den = lr * lr + li * li
    cr = ((ar - 1.0) * lr + ai * li) / den
    ci = (ai * lr - (ar - 1.0) * li) / den
    bbr = cr[:, :, None] * b_re - ci[:, :, None] * b_im
    bbi = cr[:, :, None] * b_im + ci[:, :, None] * b_re
    eye = jnp.eye(gl, dtype=F32)

    def bmat(bb):
        bb = bb.reshape(S5_COLBLOCKS, gl, p, c)
        return jnp.einsum('agpc,gh->agchp', bb, eye).reshape(S5_COLBLOCKS, gl * c, gl * p)

    def cmat(cc):
        cc = cc.reshape(S5_COLBLOCKS, gl, c, p)
        return jnp.einsum('agcp,gh->agphc', cc, eye).reshape(S5_COLBLOCKS, gl * p, gl * c)

    bd = jnp.concatenate([bmat(bbr), bmat(bbi)], axis=2).astype(BF16)
    cd = jnp.concatenate([cmat(c_re), -cmat(c_im)], axis=1).astype(BF16)

    def lanes(v, rows):
        v = v.reshape(S5_COLBLOCKS, 1, gl * p)
        return jnp.broadcast_to(v, (S5_COLBLOCKS, rows, gl * p))

    lam = jnp.concatenate([lanes(ar, SUBLANES), lanes(ai, SUBLANES), lanes(akr, 1), lanes(aki, 1)], axis=1)
    return bd, cd, lam


def _hg_body(rev, jmap_ref, seq_ref, first_ref, last_ref,
             q_ref, f_ref, v_ref, lb_ref, h0_ref, o_ref, s_ref):
    p = pl.program_id(0)
    c_len = HG_CHUNK

    @pl.when(first_ref[p] == 1)
    def _():
        s_ref[...] = h0_ref[...]

    sub = HG_SUB
    n_sub = c_len // sub
    ri = lax.broadcasted_iota(jnp.int32, (c_len, c_len), 0)
    ci = lax.broadcasted_iota(jnp.int32, (c_len, c_len), 1)
    mask = (ri <= ci) if rev else (ri >= ci)
    tri = mask.astype(BF16)
    rb, cb = ri // sub, ci // sub
    diag_mask = mask & (rb == cb)
    off_mask = (cb > rb) if rev else (cb < rb)
    lb = lb_ref[...]
    tot_row = 0 if rev else c_len - 1
    n_chunks = TOKEN_BLOCK // c_len
    chunk_order = range(n_chunks - 1, -1, -1) if rev else range(n_chunks)
    for c in chunk_order:
        rows = pl.ds(c * c_len, c_len)
        qin = q_ref[rows, :].astype(F32)
        q = qin * jax.nn.sigmoid(qin) * (HG_DK ** -0.5)
        f = lb + (1.0 - lb) * jax.nn.sigmoid(f_ref[rows, :])
        k = 1.0 - f
        g = jnp.log(f)
        g0 = g.astype(BF16)
        r1 = g - g0.astype(F32)
        g1 = r1.astype(BF16)
        g2 = (r1 - g1.astype(F32)).astype(BF16)
        b = (jnp.dot(tri, g0, preferred_element_type=F32) + jnp.dot(tri, g1, preferred_element_type=F32)
             + jnp.dot(tri, g2, preferred_element_type=F32))
        btot = b[tot_row:tot_row + 1, :]
        bmid = jnp.concatenate(
            [jnp.broadcast_to(b[i * sub + sub // 2:i * sub + sub // 2 + 1, :], (sub, HG_KW)) for i in range(n_sub)], axis=0)
        qd = (q * jnp.exp(b - bmid)).astype(BF16)
        kd = (k * jnp.exp(bmid - b)).astype(BF16)
        qo, ko = {}, {}
        for i in range(n_sub):
            edge = (i + 1) * sub if rev else i * sub - 1
            if 0 <= edge < c_len:
                bref = b[edge:edge + 1, :]
                rws = slice(i * sub, (i + 1) * sub)
                qo[i] = (q[rws] * jnp.exp(b[rws] - bref)).astype(BF16)
                ko[i] = (k * jnp.exp(jnp.minimum(bref - b, 0.0))).astype(BF16)
        qs = (q * jnp.exp(b)).astype(BF16)
        ks = (k * jnp.exp(btot - b)).astype(BF16)
        dec = jnp.exp(btot)
        v = v_ref[rows, :]
        vt = v.astype(F32).T.astype(BF16)
        outs = []
        for h in range(HG_HEADS):
            sl = slice(h * HG_DK, (h + 1) * HG_DK)
            near = _dot_nt(qd[:, sl], kd[:, sl])
            far = jnp.concatenate(
                [_dot_nt(qo[i][:, sl], ko[i][:, sl]) if i in qo else jnp.zeros((sub, c_len), F32)
                 for i in range(n_sub)], axis=0)
            att = jnp.where(diag_mask, near, jnp.where(off_mask, far, 0.0)).astype(BF16)
            s_old = s_ref[0, h]
            o_h = jnp.dot(att, v[:, sl], preferred_element_type=F32) + _dot_nt(qs[:, sl], s_old.astype(BF16))
            s_ref[0, h] = s_old * dec[:, sl] + jnp.dot(vt[sl, :], ks[:, sl], preferred_element_type=F32)
            outs.append(o_h)
        o_ref[rows, :] = jnp.concatenate(outs, axis=1)


def _hg_scan(rev, tables, q, f, v, lb, h0):
    jmap, seq, first, last = tables
    n_tok = q.shape[0]
    nblk = n_tok // TOKEN_BLOCK
    n_seq = h0.shape[0]
    d = 1 if rev else 0
    tok = pl.BlockSpec((TOKEN_BLOCK, HG_WIDTH), lambda p, jm, sq, fi, la: (jm[p], 0))
    st = pl.BlockSpec((1, HG_HEADS, HG_DV, HG_DK), lambda p, jm, sq, fi, la: (sq[p], 0, 0, 0))
    return pl.pallas_call(
        functools.partial(_hg_body, rev),
        out_shape=[jax.ShapeDtypeStruct((n_tok, HG_WIDTH), F32),
                   jax.ShapeDtypeStruct((n_seq, HG_HEADS, HG_DV, HG_DK), F32)],
        grid_spec=pltpu.PrefetchScalarGridSpec(
            num_scalar_prefetch=4, grid=(nblk,),
            in_specs=[tok,
                      pl.BlockSpec((TOKEN_BLOCK, HG_KW), lambda p, jm, sq, fi, la: (jm[p], d)),
                      tok,
                      pl.BlockSpec((1, HG_KW), lambda p, jm, sq, fi, la: (0, 0)),
                      st],
            out_specs=[tok, st]),
        compiler_params=_cparams(1),
        name="hg_bwd" if rev else "hg_fwd",
    )(jmap, seq, first, last, q, f, v, lb, h0)


def _merge_body(rowmap_ref, x_ref, ysf_ref, ysb_ref, u_ref, ohf_ref, ohb_ref, g_ref, gs_ref, gh_ref,
                mod_ref, d_ref, bglu_ref, hnw_ref, n2w_ref, wglu_ref, wbs_ref, wbh_ref, wout_ref,
                rhi_ref, rlo_ref, xm_ref, h2_ref, sc_ref):
    row = rowmap_ref[pl.program_id(0)]
    mrow = lambda n: mod_ref[pl.ds(row, 1), n * D_MODEL:(n + 1) * D_MODEL]
    gate1, shift2, scale2 = mrow(2), mrow(3), mrow(4)

    y = ysf_ref[...] + ysb_ref[...] + d_ref[...] * u_ref[...]
    gl = jax.nn.gelu(y)
    z = jnp.dot(gl.astype(BF16), wglu_ref[...], preferred_element_type=F32) + bglu_ref[...]
    y_s5 = gl * jax.nn.sigmoid(z)

    o = ohf_ref[...] + ohb_ref[...]
    normed = []
    for h in range(HG_HEADS):
        oh = o[:, h * HG_DV:(h + 1) * HG_DV]
        normed.append(oh * lax.rsqrt(jnp.mean(oh * oh, axis=-1, keepdims=True) + RMS_EPS))
    gout = g_ref[...].astype(F32)
    y_hg = jnp.concatenate(normed, axis=1) * hnw_ref[...] * (gout * jax.nn.sigmoid(gout))

    merged = (jax.nn.sigmoid(gs_ref[...].astype(F32)) * jnp.dot(y_s5.astype(BF16), wbs_ref[...], preferred_element_type=F32)
              + jax.nn.sigmoid(gh_ref[...].astype(F32)) * jnp.dot(y_hg.astype(BF16), wbh_ref[...], preferred_element_type=F32))
    xm = x_ref[...] + gate1 * jnp.dot(merged.astype(BF16), wout_ref[...], preferred_element_type=F32)
    xm_ref[...] = xm
    h2 = _rms_modulate(xm, n2w_ref[...], scale2, shift2)
    h2_ref[...] = h2
    h_hi, h_lo = _split_bf16(h2)
    logits = _dot_nt(rhi_ref[...], h_hi) + _dot_nt(rhi_ref[...], h_lo) + _dot_nt(rlo_ref[...], h_hi)
    sc_ref[...] = jax.nn.sigmoid(logits)


def _merge(rowmap, x, ysf, ysb, u, ohf, ohb, g, gs, gh, mod, d, bglu, hnw, n2w,
           wglu, wbs, wbh, wout, rhi, rlo):
    n_tok = x.shape[0]
    nblk = n_tok // DENSE_BLOCK
    tok = lambda w: pl.BlockSpec((DENSE_BLOCK, w), lambda j, rm: (j, 0))
    full = lambda shape: pl.BlockSpec(shape, lambda j, rm: (0,) * len(shape))
    return pl.pallas_call(
        _merge_body,
        out_shape=[jax.ShapeDtypeStruct((n_tok, D_MODEL), F32),
                   jax.ShapeDtypeStruct((n_tok, D_MODEL), F32),
                   jax.ShapeDtypeStruct((N_EXPERTS, n_tok), F32)],
        grid_spec=pltpu.PrefetchScalarGridSpec(
            num_scalar_prefetch=1, grid=(nblk,),
            in_specs=[tok(D_MODEL), tok(S5_WIDTH), tok(S5_WIDTH), tok(S5_WIDTH), tok(HG_WIDTH), tok(HG_WIDTH),
                      tok(HG_WIDTH), tok(D_MODEL), tok(D_MODEL),
                      full((SUBLANES, 6 * D_MODEL)), full((1, S5_WIDTH)), full((1, S5_WIDTH)),
                      full((1, HG_WIDTH)), full((1, D_MODEL)),
                      full((S5_WIDTH, S5_WIDTH)), full((S5_WIDTH, D_MODEL)), full((HG_WIDTH, D_MODEL)),
                      full((D_MODEL, D_MODEL)), full((N_EXPERTS, D_MODEL)), full((N_EXPERTS, D_MODEL))],
            out_specs=[tok(D_MODEL), tok(D_MODEL),
                       pl.BlockSpec((N_EXPERTS, DENSE_BLOCK), lambda j, rm: (0, j))]),
        compiler_params=_cparams(1),
        name="merge",
    )(rowmap, x, ysf, ysb, u, ohf, ohb, g, gs, gh, mod, d, bglu, hnw, n2w, wglu, wbs, wbh, wout, rhi, rlo)


def _experts_body(has_prev, be_ref, nused_ref, x_ref, wg_ref, wu_ref, wd_ref, *rest):
    y_ref, wgb_ref, wub_ref, wdb_ref = rest[1:] if has_prev else rest
    b = pl.program_id(0)
    e = be_ref[b]
    prev = be_ref[jnp.maximum(b - 1, 0)]

    @pl.when((b == 0) | (e != prev))
    def _():
        wgb_ref[...] = wg_ref[0, 0].astype(BF16)
        wub_ref[...] = wu_ref[0, 0].astype(BF16)
        wdb_ref[...] = wd_ref[0, 0].astype(BF16)

    @pl.when(b < nused_ref[0])
    def _():
        x = x_ref[...].astype(BF16)
        gate = jnp.dot(x, wgb_ref[...], preferred_element_type=F32)
        up = jnp.dot(x, wub_ref[...], preferred_element_type=F32)
        hid = (gate * jax.nn.sigmoid(gate) * up).astype(BF16)
        y_ref[...] = jnp.dot(hid, wdb_ref[...], preferred_element_type=F32)

    @pl.when(b >= nused_ref[0])
    def _():
        y_ref[...] = jnp.zeros_like(y_ref)


def _experts(layer, first_blk, n_rows_total, blk_expert, nused, xb, w_gate, w_up, w_down, y_prev):
    nblk = xb.shape[0] // MOE_ROWS
    wspec = pl.BlockSpec((1, 1, D_MODEL, D_MODEL), lambda b, be, nu: (layer, be[b], 0, 0))
    has_prev = y_prev is not None
    in_specs = [pl.BlockSpec((MOE_ROWS, D_MODEL), lambda b, be, nu: (b, 0)), wspec, wspec, wspec]
    args = [blk_expert, nused, xb, w_gate, w_up, w_down]
    if has_prev:
        in_specs.append(pl.BlockSpec(memory_space=pl.ANY))
        args.append(y_prev)
    return pl.pallas_call(
        functools.partial(_experts_body, has_prev),
        out_shape=jax.ShapeDtypeStruct((n_rows_total, D_MODEL), F32),
        grid_spec=pltpu.PrefetchScalarGridSpec(
            num_scalar_prefetch=2, grid=(nblk,),
            in_specs=in_specs,
            out_specs=pl.BlockSpec((MOE_ROWS, D_MODEL), lambda b, be, nu: (first_blk + b, 0)),
            scratch_shapes=[pltpu.VMEM((D_MODEL, D_MODEL), BF16)] * 3),
        input_output_aliases={len(args) - 1: 0} if has_prev else {},
        compiler_params=_cparams(1),
        name="experts",
    )(*args)


def _first_max(vals):
    best, idx = vals[0], jnp.zeros(vals[0].shape, jnp.int32)
    for j in range(1, len(vals)):
        upd = vals[j] > best
        idx = jnp.where(upd, j, idx)
        best = jnp.where(upd, vals[j], best)
    return best, idx


def _pick(idx, vals):
    out = vals[-1]
    for j in range(len(vals) - 2, -1, -1):
        out = jnp.where(idx == j, vals[j], out)
    return out


def _router_body(sc_ref, rb_ref, eid_ref, wts_ref, rank_ref, cnt_ref, base_ref):
    j = pl.program_id(0)

    @pl.when(j == 0)
    def _():
        base_ref[...] = jnp.zeros_like(base_ref)

    s = sc_ref[...]
    sel = s + rb_ref[...]
    srow = [s[e:e + 1, :] for e in range(N_EXPERTS)]
    brow = [sel[e:e + 1, :] for e in range(N_EXPERTS)]
    epg = EXPERTS_PER_GROUP
    gscore = []
    for g in range(N_GROUPS):
        x = brow[g * epg:(g + 1) * epg]
        pairs = [x[a] + x[b] for a in range(epg) for b in range(a + 1, epg)]
        gscore.append(functools.reduce(jnp.maximum, pairs))
    _, bestg = _first_max(gscore)
    y = [_pick(bestg, [brow[g * epg + i] for g in range(N_GROUPS)]) for i in range(epg)]
    sy = [_pick(bestg, [srow[g * epg + i] for g in range(N_GROUPS)]) for i in range(epg)]
    _, i1 = _first_max(y)
    v2 = jnp.full(y[0].shape, -jnp.inf, F32)
    i2 = jnp.zeros(y[0].shape, jnp.int32)
    for i in range(epg):
        upd = (i1 != i) & (y[i] > v2)
        i2 = jnp.where(upd, i, i2)
        v2 = jnp.where(upd, y[i], v2)
    e0 = bestg * epg + i1
    e1 = bestg * epg + i2
    w0 = _pick(i1, sy)
    w1 = _pick(i2, sy)
    wsum = w0 + w1
    eid_ref[0:1, :] = e0
    eid_ref[1:2, :] = e1
    wts_ref[0:1, :] = w0 / wsum
    wts_ref[1:2, :] = w1 / wsum

    eiota = lax.broadcasted_iota(jnp.int32, s.shape, 0)
    m0 = eiota == e0
    m1 = eiota == e1
    member = (m0 | m1).astype(BF16)
    tb = s.shape[1]
    before = (lax.broadcasted_iota(jnp.int32, (tb, tb), 0) < lax.broadcasted_iota(jnp.int32, (tb, tb), 1)).astype(BF16)
    pos = base_ref[...] + jnp.dot(member, before, preferred_element_type=F32)
    rank_ref[0:1, :] = jnp.sum(jnp.where(m0, pos, 0.0), axis=0, keepdims=True).astype(jnp.int32)
    rank_ref[1:2, :] = jnp.sum(jnp.where(m1, pos, 0.0), axis=0, keepdims=True).astype(jnp.int32)
    base = base_ref[...] + jnp.sum(member.astype(F32), axis=1, keepdims=True)
    base_ref[...] = base
    cnt_ref[...] = jnp.broadcast_to(base, cnt_ref.shape).astype(jnp.int32)


def _router(scores_t, router_b):
    n_tok = scores_t.shape[1]
    tok = lambda rows: pl.BlockSpec((rows, DENSE_BLOCK), lambda j: (0, j))
    return pl.pallas_call(
        _router_body,
        out_shape=[jax.ShapeDtypeStruct((TOP_K, n_tok), jnp.int32), jax.ShapeDtypeStruct((TOP_K, n_tok), F32),
                   jax.ShapeDtypeStruct((TOP_K, n_tok), jnp.int32), jax.ShapeDtypeStruct((N_EXPERTS, LANES), jnp.int32)],
        grid=(n_tok // DENSE_BLOCK,),
        in_specs=[tok(N_EXPERTS), pl.BlockSpec((N_EXPERTS, 1), lambda j: (0, 0))],
        out_specs=[tok(TOP_K), tok(TOP_K), tok(TOP_K), pl.BlockSpec((N_EXPERTS, LANES), lambda j: (0, 0))],
        scratch_shapes=[pltpu.VMEM((N_EXPERTS, 1), F32)],
        compiler_params=_cparams(1),
        name="router",
    )(scores_t, router_b.reshape(N_EXPERTS, 1))


def _dispatch_plan(eid, rank, counts):
    n_tok = eid.shape[1]
    n_slots = n_tok * TOP_K
    padded = (counts + MOE_ROWS - 1) // MOE_ROWS * MOE_ROWS
    pad_end = jnp.cumsum(padded)
    pad_start = pad_end - padded
    e_ids = jnp.arange(N_EXPERTS, dtype=jnp.int32)
    dest = jnp.sum(jnp.where(eid[..., None] == e_ids, pad_start, 0), axis=-1) + rank
    n_blocks = -(-n_slots // MOE_ROWS) + N_EXPERTS
    n_rows = n_blocks * MOE_ROWS
    blk_start = jnp.arange(n_blocks, dtype=jnp.int32) * MOE_ROWS
    blk_expert = jnp.minimum(jnp.sum(pad_end[None, :] <= blk_start[:, None], axis=-1), N_EXPERTS - 1).astype(jnp.int32)
    nused = (pad_end[-1:] // MOE_ROWS).astype(jnp.int32)
    n_fill = n_rows - n_slots
    gap = padded - counts
    gap_before = jnp.cumsum(gap) - gap
    j = jnp.arange(n_fill, dtype=jnp.int32)
    in_gap = (j[:, None] >= gap_before[None, :]) & (j[:, None] < (gap_before + gap)[None, :])
    fill_pos = jnp.where(jnp.any(in_gap, axis=-1),
                         jnp.sum(jnp.where(in_gap, (pad_start + counts - gap_before)[None, :], 0), axis=-1) + j,
                         pad_end[-1] - jnp.sum(gap) + j)
    keys = jnp.concatenate([dest.reshape(-1), fill_pos])
    vals = jnp.concatenate([jnp.tile(jnp.arange(n_tok, dtype=jnp.int32), TOP_K), j % n_tok])
    _, row_tok = lax.sort_key_val(keys, vals)
    return dest, blk_expert, nused, row_tok


def _final_body(n_ctx_blk, n_r, rowmap_ref, x_ref, *refs):
    r_refs, (gate_ref, nw_ref, oc_ref, ol_ref) = refs[:n_r], refs[n_r:]
    j = pl.program_id(0)
    row = rowmap_ref[j]
    xn = x_ref[...] + gate_ref[pl.ds(row, 1), :] * _branch_update(r_refs)
    ms = jnp.mean(xn * xn, axis=-1, keepdims=True)
    y = xn * lax.rsqrt(ms + RMS_EPS) * nw_ref[...]

    @pl.when(j < n_ctx_blk)
    def _():
        oc_ref[...] = y

    @pl.when(j >= n_ctx_blk)
    def _():
        ol_ref[...] = y


def _final(rowmap, x, r, gate_rows, nw, n_ctx_tok):
    n_tok = x.shape[0]
    n_ctx_blk = n_ctx_tok // DENSE_BLOCK
    tok = lambda w: pl.BlockSpec((DENSE_BLOCK, w), lambda j, rm: (j, 0))
    full = lambda shape: pl.BlockSpec(shape, lambda j, rm: (0,) * len(shape))
    return pl.pallas_call(
        functools.partial(_final_body, n_ctx_blk, len(r)),
        out_shape=[jax.ShapeDtypeStruct((n_ctx_tok, D_MODEL), F32), jax.ShapeDtypeStruct((n_tok - n_ctx_tok, D_MODEL), F32)],
        grid_spec=pltpu.PrefetchScalarGridSpec(
            num_scalar_prefetch=1, grid=(n_tok // DENSE_BLOCK,),
            in_specs=[tok(D_MODEL)] + _row_specs(r) + [full((SUBLANES, D_MODEL)), full((1, D_MODEL))],
            out_specs=[pl.BlockSpec((DENSE_BLOCK, D_MODEL), lambda j, rm: (jnp.minimum(j, n_ctx_blk - 1), 0)),
                       pl.BlockSpec((DENSE_BLOCK, D_MODEL), lambda j, rm: (jnp.maximum(j - n_ctx_blk, 0), 0))]),
        compiler_params=_cparams(1),
        name="final_norm",
    )(rowmap, x, *[a for a, _ in r], gate_rows, nw)


def _grid_pos_embed(rows):
    r = jnp.repeat(jnp.arange(rows, dtype=F32), GRID_W)
    col = jnp.tile(jnp.arange(GRID_W, dtype=F32), rows)
    quarter = D_MODEL // 4
    omega = 1.0 / (10000.0 ** (jnp.arange(quarter, dtype=F32) / quarter))

    def axis_embed(pos):
        ang = pos[:, None] * omega[None, :]
        return jnp.concatenate([jnp.sin(ang), jnp.cos(ang)], axis=-1)

    return jnp.concatenate([axis_embed(r), axis_embed(col)], axis=-1)


def kernel(x_prompt, x_sample, state_s5_re, state_s5_im, state_hg, c, c_ctx, w_ada, b_ada, norm1_w, norm2_w, w_in, s5_lam_re, s5_lam_im, s5_log_step, s5_b_re, s5_b_im, s5_c_re, s5_c_im, s5_d, s5_w_glu, s5_b_glu, hg_lb_logits, hg_norm_w, w_br_s5, w_br_hg, w_out, router_w, router_b, moe_w_gate, moe_w_up, moe_w_down, norm_f_w):
    n_ctx, ctx_len, _ = x_prompt.shape
    n_lat, lat_len, _ = x_sample.shape
    n_ctx_tok = n_ctx * ctx_len
    n_seq = n_ctx + n_lat
    assert n_lat + 1 <= SUBLANES and ctx_len % TOKEN_BLOCK == 0 and lat_len % DENSE_BLOCK == 0
    assert (n_ctx * ctx_len) % DENSE_BLOCK == 0

    seq_np, first_np, last_np, modrow_np = _block_tables(n_ctx, ctx_len, n_lat, lat_len)
    nblk = seq_np.shape[0]
    jmap_f = np.arange(nblk, dtype=np.int32)
    jmap_b = jmap_f[::-1].copy()
    tab_f = tuple(jnp.asarray(t) for t in (jmap_f, seq_np, first_np, last_np))
    tab_b = tuple(jnp.asarray(t) for t in (jmap_b, seq_np[::-1].copy(), last_np[::-1].copy(), first_np[::-1].copy()))
    tab_s5 = tuple(jnp.asarray(t) for t in (seq_np, first_np, last_np))
    rowmap = jnp.asarray(modrow_np.reshape(-1, DENSE_BLOCK // TOKEN_BLOCK)[:, 0])

    cvecs = jnp.zeros((SUBLANES, D_MODEL), F32).at[0].set(c_ctx).at[1:1 + n_lat].set(c)
    mod_all = _adaln(cvecs, w_ada, b_ada)

    probs = jax.nn.softmax(hg_lb_logits.astype(F32), axis=0)
    lower_bounds = jnp.cumsum(probs, axis=0) - probs[0]

    pos = _grid_pos_embed(lat_len // GRID_W)
    x = (x_prompt.reshape(n_ctx_tok, D_MODEL), x_sample.reshape(n_lat * lat_len, D_MODEL))
    n_ctx_blk = n_ctx_tok // DENSE_BLOCK
    n_dense_blk = (n_ctx_tok + n_lat * lat_len) // DENSE_BLOCK
    pos_blocks = lat_len // DENSE_BLOCK
    resid = ((pos, lambda j: jnp.maximum(j - n_ctx_blk, 0) % pos_blocks),)
    gate_rows = jnp.ones((SUBLANES, D_MODEL), F32).at[0].set(0.0)

    router_t = router_w.T.astype(F32)
    r_hi, r_lo = _split_bf16(router_t)
    ns = S5_BLOCK_STATES
    zeros_s5 = jnp.zeros((n_ctx, S5_COLBLOCKS, 2, ns), F32)
    zeros_hg = jnp.zeros((n_ctx, HG_HEADS, HG_DV, HG_DK), F32)

    new_s5_re, new_s5_im, new_hg = [], [], []
    for l in range(DEPTH):
        mod = mod_all[l]
        x, u, q_in, f_raw, i_in, g_out, gate_s5, gate_hg = _inproj(
            rowmap, x, resid, gate_rows, mod, norm1_w[l].reshape(1, D_MODEL), w_in[l].astype(BF16))

        par = [_s5_params(s5_lam_re[l, d], s5_lam_im[l, d], s5_log_step[l, d], s5_b_re[l], s5_b_im[l], s5_c_re[l], s5_c_im[l])
               for d in range(2)]
        h0_s5 = jnp.stack([
            jnp.concatenate([zeros_s5, jnp.stack([state_s5_re[:, l, d], state_s5_im[:, l, d]], axis=1).astype(F32).reshape(
                n_lat, 2, S5_COLBLOCKS, ns).swapaxes(1, 2)], axis=0) for d in range(2)])
        y_f, y_b, fin_f, fin_b = _s5_scan(tab_s5, u, jnp.stack([par[0][0], par[1][0]]), par[0][1],
                                          jnp.stack([par[0][2], par[1][2]]), h0_s5)
        ys = [y_f, y_b]
        s5_fin_re = [fin[:n_ctx, :, 0].reshape(n_ctx, S5_GROUPS, S5_STATE) for fin in (fin_f, fin_b)]
        s5_fin_im = [fin[:n_ctx, :, 1].reshape(n_ctx, S5_GROUPS, S5_STATE) for fin in (fin_f, fin_b)]

        os_, hg_fin = [], []
        for d, tabs in enumerate((tab_f, tab_b)):

            h0 = jnp.concatenate([zeros_hg, jnp.swapaxes(state_hg[:, l, d].astype(F32), -1, -2)], axis=0)
            o_d, hfin = _hg_scan(bool(d), tabs, q_in, f_raw, i_in, lower_bounds[l, d].reshape(1, HG_KW), h0)
            os_.append(o_d)
            hg_fin.append(jnp.swapaxes(hfin[:n_ctx], -1, -2))
        new_s5_re.append(jnp.stack(s5_fin_re, axis=1))
        new_s5_im.append(jnp.stack(s5_fin_im, axis=1))
        new_hg.append(jnp.stack(hg_fin, axis=1))

        x, h2, scores_t = _merge(
            rowmap, x, ys[0], ys[1], u, os_[0], os_[1], g_out, gate_s5, gate_hg, mod,
            s5_d[l].reshape(1, S5_WIDTH), s5_b_glu[l].reshape(1, S5_WIDTH), hg_norm_w[l].reshape(1, HG_WIDTH),
            norm2_w[l].reshape(1, D_MODEL), s5_w_glu[l].astype(BF16), w_br_s5[l].astype(BF16),
            w_br_hg[l].astype(BF16), w_out[l].astype(BF16), r_hi, r_lo)

        eid, wts, rank, counts = _router(scores_t, router_b.astype(F32))
        dest, blk_expert, nused, row_tok = _dispatch_plan(eid, rank, counts[:, 0])
        n_blocks = blk_expert.shape[0]
        cb = n_blocks // MOE_CHUNKS
        yb = None
        for ch in range(MOE_CHUNKS):
            xb = h2[row_tok[ch * cb * MOE_ROWS:(ch + 1) * cb * MOE_ROWS]]
            yb = _experts(l, ch * cb, n_blocks * MOE_ROWS, blk_expert[ch * cb:(ch + 1) * cb],
                          jnp.clip(nused - ch * cb, 0, cb), xb, moe_w_gate, moe_w_up, moe_w_down, yb)
        rows = yb[dest.reshape(-1)]
        resid = ((rows, _same_block), (rows, lambda j: j + n_dense_blk), (wts.T, _same_block))
        gate_rows = mod[:, 5 * D_MODEL:6 * D_MODEL]

    y_ctx, y_lat = _final(rowmap, x, resid, gate_rows, norm_f_w.reshape(1, D_MODEL), n_ctx_tok)
    y_prompt = y_ctx.reshape(n_ctx, ctx_len, D_MODEL)
    y_sample = y_lat.reshape(n_lat, lat_len, D_MODEL)
    new_state_s5_re = jnp.stack(new_s5_re, axis=1).astype(state_s5_re.dtype)
    new_state_s5_im = jnp.stack(new_s5_im, axis=1).astype(state_s5_im.dtype)
    new_state_hg = jnp.stack(new_hg, axis=1).astype(state_hg.dtype)
    return (y_prompt, y_sample, new_state_s5_re, new_state_s5_im, new_state_hg)
```

```python
import functools
import math

import jax
import jax.numpy as jnp
import numpy as np
from jax import lax
from jax.experimental import pallas as pl
from jax.experimental.pallas import tpu as pltpu

F32 = jnp.float32
BF16 = jnp.bfloat16

D_MODEL = 1024
DEPTH = 4
GRID_W = 64
RMS_EPS = 1e-6
S5_WIDTH = 512
S5_GROUP = 16
S5_GROUPS = 32
S5_STATE = 64
HG_HEADS = 4
HG_DK = 128
HG_DV = 128
HG_WIDTH = 512
HG_KW = 512
HG_CHUNK = 64
HG_SUB = 16
N_EXPERTS = 16
N_GROUPS = 4
EXPERTS_PER_GROUP = 4
TOP_K = 2
IN_COLS = 5120

LANES = 128
SUBLANES = 8
VMEM_LIMIT_BYTES = 56 * 1024 * 1024

TOKEN_BLOCK = 256
DENSE_BLOCK = 512
S5_SEG = TOKEN_BLOCK // SUBLANES
MXU_DEPTH = 256
S5_CHANNELS = MXU_DEPTH
S5_COLBLOCKS = S5_WIDTH // S5_CHANNELS
S5_BLOCK_STATES = (S5_CHANNELS // S5_GROUP) * S5_STATE
S5_PITCH = S5_SEG + 4
_S5_PERM = np.zeros((TOKEN_BLOCK, TOKEN_BLOCK), np.float32)
_S5_PERM[np.arange(TOKEN_BLOCK), (np.arange(TOKEN_BLOCK) % SUBLANES) * S5_SEG + np.arange(TOKEN_BLOCK) // SUBLANES] = 1.0
MOE_ROWS = 256
MOE_CHUNKS = 4


def _block_tables(n_ctx_seq, ctx_len, n_lat_seq, lat_len):
    seq, first, last, modrow = [], [], [], []
    for s in range(n_ctx_seq):
        nb = ctx_len // TOKEN_BLOCK
        for b in range(nb):
            seq.append(s); first.append(int(b == 0)); last.append(int(b == nb - 1)); modrow.append(0)
    for s in range(n_lat_seq):
        nb = lat_len // TOKEN_BLOCK
        for b in range(nb):
            seq.append(n_ctx_seq + s); first.append(int(b == 0)); last.append(int(b == nb - 1))
            modrow.append(1 + s)
    return (np.asarray(seq, np.int32), np.asarray(first, np.int32),
            np.asarray(last, np.int32), np.asarray(modrow, np.int32))


def _cparams(n_axes):
    return pltpu.CompilerParams(dimension_semantics=("arbitrary",) * n_axes,
                                vmem_limit_bytes=VMEM_LIMIT_BYTES)


def _split_bf16(a):
    hi = a.astype(BF16)
    lo = (a - hi.astype(F32)).astype(BF16)
    return hi, lo


def _dot_nt(a, b):
    return lax.dot_general(a, b, (((1,), (1,)), ((), ())), preferred_element_type=F32)


def _adaln_body(c_ref, w_ref, b_ref, o_ref):
    a = c_ref[...]
    a = a * jax.nn.sigmoid(a)
    a_hi, a_lo = _split_bf16(a)
    w_hi, w_lo = _split_bf16(w_ref[0])
    acc = jnp.dot(a_hi, w_hi, preferred_element_type=F32)
    acc += jnp.dot(a_lo, w_hi, preferred_element_type=F32)
    acc += jnp.dot(a_hi, w_lo, preferred_element_type=F32)
    o_ref[0] = acc + b_ref[0]


def _adaln(cvecs, w_ada, b_ada):
    tn = 1536
    n_tiles = (6 * D_MODEL) // tn
    return pl.pallas_call(
        _adaln_body,
        out_shape=jax.ShapeDtypeStruct((DEPTH, SUBLANES, 6 * D_MODEL), F32),
        grid=(DEPTH, n_tiles),
        in_specs=[
            pl.BlockSpec((SUBLANES, D_MODEL), lambda l, n: (0, 0)),
            pl.BlockSpec((1, D_MODEL, tn), lambda l, n: (l, 0, n)),
            pl.BlockSpec((1, 1, tn), lambda l, n: (l, 0, n)),
        ],
        out_specs=pl.BlockSpec((1, SUBLANES, tn), lambda l, n: (l, 0, n)),
        compiler_params=_cparams(2),
        name="adaln",
    )(cvecs, w_ada, b_ada.reshape(DEPTH, 1, 6 * D_MODEL))


_IN_SPLITS = ((0, 512), (512, 1024), (1024, 2048), (2048, 2560), (2560, 3072), (3072, 4096), (4096, 5120))


def _rms_modulate(x, nw, scale, shift):
    ms = jnp.mean(x * x, axis=-1, keepdims=True)
    return (x * lax.rsqrt(ms + RMS_EPS)) * nw * (1.0 + scale) + shift


def _branch_update(r_refs):
    if len(r_refs) == 1:
        return r_refs[0][...]
    ya_ref, yb_ref, w_ref = r_refs
    w = w_ref[...]
    return ya_ref[...] * w[:, 0:1] + yb_ref[...] * w[:, 1:2]


def _inproj_body(n_x, n_ctx_blk, n_r, rowmap_ref, *refs):
    x_refs, r_refs = refs[:n_x], refs[n_x:n_x + n_r]
    gate_ref, mod_ref, nw_ref, w_ref, xo_ref, u_ref, q_ref, f_ref, i_ref, g_ref, gs_ref, gh_ref = refs[n_x + n_r:]
    j = pl.program_id(0)
    row = rowmap_ref[j]
    gate = gate_ref[pl.ds(row, 1), :]
    x = x_refs[0][...] if n_x == 1 else jnp.where(j < n_ctx_blk, x_refs[0][...], x_refs[1][...])
    xn = x + gate * _branch_update(r_refs)
    xo_ref[...] = xn
    shift = mod_ref[pl.ds(row, 1), 0:D_MODEL]
    scale = mod_ref[pl.ds(row, 1), D_MODEL:2 * D_MODEL]
    h = _rms_modulate(xn, nw_ref[...], scale, shift).astype(BF16)
    outs = (u_ref, q_ref, f_ref, i_ref, g_ref, gs_ref, gh_ref)
    for ref, (a, b) in zip(outs, _IN_SPLITS):
        ref[...] = jnp.dot(h, w_ref[:, a:b], preferred_element_type=F32).astype(ref.dtype)


def _row_specs(arrays_and_maps):
    return [pl.BlockSpec((DENSE_BLOCK, a.shape[1]), lambda j, rm, f=f: (f(j), 0)) for a, f in arrays_and_maps]


def _same_block(j):
    return j


def _inproj(rowmap, x, r, gate_rows, mod, nw, w_bf16):
    xs = x if isinstance(x, tuple) else (x,)
    n_tok = sum(a.shape[0] for a in xs)
    nblk = n_tok // DENSE_BLOCK
    n_ctx_blk = xs[0].shape[0] // DENSE_BLOCK
    x_maps = [_same_block] if len(xs) == 1 else [lambda j: jnp.minimum(j, n_ctx_blk - 1), lambda j: jnp.maximum(j - n_ctx_blk, 0)]
    tok = lambda w: pl.BlockSpec((DENSE_BLOCK, w), lambda j, rm: (j, 0))
    full = lambda shape: pl.BlockSpec(shape, lambda j, rm: (0,) * len(shape))
    widths = [b - a for a, b in _IN_SPLITS]
    dtypes = [F32, BF16, F32, BF16, BF16, BF16, BF16]
    return pl.pallas_call(
        functools.partial(_inproj_body, len(xs), n_ctx_blk, len(r)),
        out_shape=[jax.ShapeDtypeStruct((n_tok, D_MODEL), F32)]
        + [jax.ShapeDtypeStruct((n_tok, w), dt) for w, dt in zip(widths, dtypes)],
        grid_spec=pltpu.PrefetchScalarGridSpec(
            num_scalar_prefetch=1, grid=(nblk,),
            in_specs=_row_specs(list(zip(xs, x_maps))) + _row_specs(r)
            + [full((SUBLANES, D_MODEL)), full((SUBLANES, 6 * D_MODEL)), full((1, D_MODEL)), full((D_MODEL, IN_COLS))],
            out_specs=[tok(D_MODEL)] + [tok(w) for w in widths]),
        compiler_params=_cparams(1),
        name="inproj",
    )(rowmap, *xs, *[a for a, _ in r], gate_rows, mod, nw, w_bf16)


def _s5_direction(rev, u_ref, perm_ref, bd_ref, cd_ref, lam_ref,
                  y_ref, fin_ref, bu_ref, sb_ref, yp_ref, carry_ref):
    ns = S5_BLOCK_STATES

    up = jnp.dot(perm_ref[...], u_ref[...].astype(BF16), preferred_element_type=F32).astype(BF16)
    bu_ref[...] = jnp.dot(up, bd_ref[0, 0], preferred_element_type=F32)

    lre = lam_ref[0, 0, 0:SUBLANES, :]
    lim = lam_ref[0, 0, SUBLANES:2 * SUBLANES, :]
    order = list(range(S5_SEG))
    if rev:
        order = order[::-1]

    def step(sr, si, k):
        br = bu_ref[pl.ds(SUBLANES * k, SUBLANES), 0:ns]
        bi = bu_ref[pl.ds(SUBLANES * k, SUBLANES), ns:2 * ns]
        return lre * sr - lim * si + br, lre * si + lim * sr + bi

    sr = jnp.zeros((SUBLANES, ns), F32)
    si = jnp.zeros((SUBLANES, ns), F32)
    for k in order:
        sr, si = step(sr, si, k)

    cr = carry_ref[0:1, :]
    ci = carry_ref[1:2, :]
    lkre = lam_ref[0, 0, 2 * SUBLANES:2 * SUBLANES + 1, :]
    lkim = lam_ref[0, 0, 2 * SUBLANES + 1:2 * SUBLANES + 2, :]
    start_r = [None] * SUBLANES
    start_i = [None] * SUBLANES
    seg_order = list(range(SUBLANES))
    if rev:
        seg_order = seg_order[::-1]
    for i in seg_order:
        start_r[i] = cr
        start_i[i] = ci
        er = sr[i:i + 1]
        ei = si[i:i + 1]
        cr, ci = lkre * cr - lkim * ci + er, lkre * ci + lkim * cr + ei
    carry_ref[0:1, :] = cr
    carry_ref[1:2, :] = ci
    fin_ref[0, 0, 0:1, :] = cr
    fin_ref[0, 0, 1:2, :] = ci

    sr = jnp.concatenate(start_r, axis=0)
    si = jnp.concatenate(start_i, axis=0)
    for n in range(0, S5_SEG, 2):
        k0, k1 = order[n], order[n + 1]
        sr0, si0 = step(sr, si, k0)
        sr, si = step(sr0, si0, k1)
        lo_r, hi_r = (sr0, sr) if k0 < k1 else (sr, sr0)
        lo_i, hi_i = (si0, si) if k0 < k1 else (si, si0)
        base = SUBLANES * min(k0, k1)
        sb_ref[pl.ds(base, 2 * SUBLANES), 0:ns] = jnp.concatenate([lo_r, hi_r], axis=0).astype(BF16)
        sb_ref[pl.ds(base, 2 * SUBLANES), ns:2 * ns] = jnp.concatenate([lo_i, hi_i], axis=0).astype(BF16)

    y = jnp.dot(sb_ref[...], cd_ref[0], preferred_element_type=F32)
    for h in range(S5_CHANNELS // LANES):
        for k in range(S5_SEG):
            yp_ref[h, pl.ds(k, SUBLANES, stride=S5_PITCH), :] = y[SUBLANES * k:SUBLANES * (k + 1), h * LANES:(h + 1) * LANES]
        for i in range(SUBLANES):
            y_ref[i * S5_SEG:(i + 1) * S5_SEG, h * LANES:(h + 1) * LANES] = yp_ref[h, pl.ds(i * S5_PITCH, S5_SEG), :]


def _s5_body(seq_ref, first_ref, last_ref,
             uf_ref, ub_ref, perm_ref, bdf_ref, bdb_ref, cd_ref, lamf_ref, lamb_ref, h0f_ref, h0b_ref,
             yf_ref, yb_ref, finf_ref, finb_ref,
             buf_ref, bub_ref, sbf_ref, sbb_ref, ypf_ref, ypb_ref, carryf_ref, carryb_ref):
    p = pl.program_id(1)
    q = pl.num_programs(1) - 1 - p

    @pl.when(first_ref[p] == 1)
    def _():
        carryf_ref[...] = h0f_ref[0, 0, 0]

    @pl.when(last_ref[q] == 1)
    def _():
        carryb_ref[...] = h0b_ref[0, 0, 0]

    _s5_direction(False, uf_ref, perm_ref, bdf_ref, cd_ref, lamf_ref,
                  yf_ref, finf_ref, buf_ref, sbf_ref, ypf_ref, carryf_ref)
    _s5_direction(True, ub_ref, perm_ref, bdb_ref, cd_ref, lamb_ref,
                  yb_ref, finb_ref, bub_ref, sbb_ref, ypb_ref, carryb_ref)


def _s5_scan(tables, u, bd, cd, lam, h0):
    seq, first, last = tables
    n_tok = u.shape[0]
    nblk = n_tok // TOKEN_BLOCK
    n_seq = h0.shape[1]
    ns = S5_BLOCK_STATES
    lam_rows = lam.shape[2]
    grid = (S5_COLBLOCKS, nblk)
    fwd = lambda a, p, sq, fi, la: p
    bwd = lambda a, p, sq, fi, la: nblk - 1 - p
    tok = lambda blk: pl.BlockSpec((TOKEN_BLOCK, S5_CHANNELS), lambda a, p, sq, fi, la: (blk(a, p, sq, fi, la), a))
    par = lambda d, shape: pl.BlockSpec((1, 1) + shape, lambda a, p, sq, fi, la: (d, a, 0, 0))
    st = lambda d, blk: pl.BlockSpec((1, 1, 1, 2, ns), lambda a, p, sq, fi, la: (d, sq[blk(a, p, sq, fi, la)], a, 0, 0))
    fin = lambda blk: pl.BlockSpec((1, 1, 2, ns), lambda a, p, sq, fi, la: (sq[blk(a, p, sq, fi, la)], a, 0, 0))
    fin_shape = jax.ShapeDtypeStruct((n_seq, S5_COLBLOCKS, 2, ns), F32)
    return pl.pallas_call(
        _s5_body,
        out_shape=[jax.ShapeDtypeStruct((n_tok, S5_WIDTH), F32), jax.ShapeDtypeStruct((n_tok, S5_WIDTH), F32),
                   fin_shape, fin_shape],
        grid_spec=pltpu.PrefetchScalarGridSpec(
            num_scalar_prefetch=3, grid=grid,
            in_specs=[
                tok(fwd), tok(bwd),
                pl.BlockSpec((TOKEN_BLOCK, TOKEN_BLOCK), lambda a, p, sq, fi, la: (0, 0)),
                par(0, (S5_CHANNELS, 2 * ns)), par(1, (S5_CHANNELS, 2 * ns)),
                pl.BlockSpec((1, 2 * ns, S5_CHANNELS), lambda a, p, sq, fi, la: (a, 0, 0)),
                par(0, (lam_rows, ns)), par(1, (lam_rows, ns)),
                st(0, fwd), st(1, bwd)],
            out_specs=[tok(fwd), tok(bwd), fin(fwd), fin(bwd)],
            scratch_shapes=[pltpu.VMEM((TOKEN_BLOCK, 2 * ns), F32)] * 2
            + [pltpu.VMEM((TOKEN_BLOCK, 2 * ns), BF16)] * 2
            + [pltpu.VMEM((S5_CHANNELS // LANES, SUBLANES * S5_PITCH, LANES), F32)] * 2
            + [pltpu.VMEM((2, ns), F32)] * 2),
        compiler_params=_cparams(2),
        name="s5_scan",
    )(seq, first, last, u, u, jnp.asarray(_S5_PERM, BF16), bd, bd, cd, lam, lam, h0, h0)


def _s5_params(lam_re, lam_im, log_step, b_re, b_im, c_re, c_im):
    g, p, c = S5_GROUPS, S5_STATE, S5_GROUP
    gl = S5_CHANNELS // c
    lr = jnp.minimum(lam_re, -1e-4)
    li = lam_im
    step = jnp.exp(log_step)[:, None]
    mag = jnp.exp(lr * step)
    ar, ai = mag * jnp.cos(li * step), mag * jnp.sin(li * step)
    magk = jnp.exp(lr * step * S5_SEG)
    akr, aki = magk * jnp.cos(li * step * S5_SEG), magk * jnp.sin(li * step * S5_SEG)
    den = lr * lr + li * li
    cr = ((ar - 1.0) * lr + ai * li) / den
    ci = (ai * lr - (ar - 1.0) * li) / den
    bbr = cr[:, :, None] * b_re - ci[:, :, None] * b_im
    bbi = cr[:, :, None] * b_im + ci[:, :, None] * b_re
    eye = jnp.eye(gl, dtype=F32)

    def bmat(bb):
        bb = bb.reshape(S5_COLBLOCKS, gl, p, c)
        return jnp.einsum('agpc,gh->agchp', bb, eye).reshape(S5_COLBLOCKS, gl * c, gl * p)

    def cmat(cc):
        cc = cc.reshape(S5_COLBLOCKS, gl, c, p)
        return jnp.einsum('agcp,gh->agphc', cc, eye).reshape(S5_COLBLOCKS, gl * p, gl * c)

    bd = jnp.concatenate([bmat(bbr), bmat(bbi)], axis=2).astype(BF16)
    cd = jnp.concatenate([cmat(c_re), -cmat(c_im)], axis=1).astype(BF16)

    def lanes(v, rows):
        v = v.reshape(S5_COLBLOCKS, 1, gl * p)
        return jnp.broadcast_to(v, (S5_COLBLOCKS, rows, gl * p))

    lam = jnp.concatenate([lanes(ar, SUBLANES), lanes(ai, SUBLANES), lanes(akr, 1), lanes(aki, 1)], axis=1)
    return bd, cd, lam


def _hg_body(rev, jmap_ref, seq_ref, first_ref, last_ref,
             q_ref, f_ref, v_ref, lb_ref, h0_ref, o_ref, s_ref):
    p = pl.program_id(0)
    c_len = HG_CHUNK

    @pl.when(first_ref[p] == 1)
    def _():
        s_ref[...] = h0_ref[...]

    sub = HG_SUB
    n_sub = c_len // sub
    ri = lax.broadcasted_iota(jnp.int32, (c_len, c_len), 0)
    ci = lax.broadcasted_iota(jnp.int32, (c_len, c_len), 1)
    tri = ((ri <= ci) if rev else (ri >= ci)).astype(BF16)
    hc = HG_HEADS * c_len
    rr = lax.broadcasted_iota(jnp.int32, (hc, hc), 0)
    cc = lax.broadcasted_iota(jnp.int32, (hc, hc), 1)
    same_head = (rr // c_len) == (cc // c_len)
    rt, ct = rr % c_len, cc % c_len
    causal = (rt <= ct) if rev else (rt >= ct)
    rb, cb = rt // sub, ct // sub
    diag_mask = same_head & causal & (rb == cb)
    off_mask = same_head & ((cb > rb) if rev else (cb < rb))

    def stack(x):
        return jnp.concatenate([x[:, h * HG_DK:(h + 1) * HG_DK] for h in range(HG_HEADS)], axis=0)

    lb = lb_ref[...]
    tot_row = 0 if rev else c_len - 1
    n_chunks = TOKEN_BLOCK // c_len
    chunk_order = range(n_chunks - 1, -1, -1) if rev else range(n_chunks)
    for c in chunk_order:
        rows = pl.ds(c * c_len, c_len)
        qin = q_ref[rows, :].astype(F32)
        q = qin * jax.nn.sigmoid(qin) * (HG_DK ** -0.5)
        f = lb + (1.0 - lb) * jax.nn.sigmoid(f_ref[rows, :])
        k = 1.0 - f
        g = jnp.log(f)
        g0 = g.astype(BF16)
        r1 = g - g0.astype(F32)
        g1 = r1.astype(BF16)
        g2 = (r1 - g1.astype(F32)).astype(BF16)
        b = (jnp.dot(tri, g0, preferred_element_type=F32) + jnp.dot(tri, g1, preferred_element_type=F32)
             + jnp.dot(tri, g2, preferred_element_type=F32))
        btot = b[tot_row:tot_row + 1, :]
        bmid = jnp.concatenate(
            [jnp.broadcast_to(b[i * sub + sub // 2:i * sub + sub // 2 + 1, :], (sub, HG_KW)) for i in range(n_sub)], axis=0)
        qd = (q * jnp.exp(b - bmid)).astype(BF16)
        kd = (k * jnp.exp(bmid - b)).astype(BF16)
        qo, ko = {}, {}
        for i in range(n_sub):
            edge = (i + 1) * sub if rev else i * sub - 1
            if 0 <= edge < c_len:
                bref = b[edge:edge + 1, :]
                rws = slice(i * sub, (i + 1) * sub)
                qo[i] = (q[rws] * jnp.exp(b[rws] - bref)).astype(BF16)
                ko[i] = (k * jnp.exp(jnp.minimum(bref - b, 0.0))).astype(BF16)
        qs = (q * jnp.exp(b)).astype(BF16)
        ks = (k * jnp.exp(btot - b)).astype(BF16)
        dec = jnp.exp(btot)
        v = v_ref[rows, :]
        vt = v.astype(F32).T.astype(BF16)
        near = _dot_nt(stack(qd), stack(kd))
        pieces = {i: _dot_nt(stack(qo[i]), stack(ko[i])) for i in qo}
        far = jnp.concatenate(
            [pieces[i][h * sub:(h + 1) * sub] if i in pieces else jnp.zeros((sub, hc), F32)
             for h in range(HG_HEADS) for i in range(n_sub)], axis=0)
        att = jnp.where(diag_mask, near, jnp.where(off_mask, far, 0.0)).astype(BF16)
        o_all = jnp.dot(att, stack(v), preferred_element_type=F32)
        outs = []
        for h in range(HG_HEADS):
            sl = slice(h * HG_DK, (h + 1) * HG_DK)
            s_old = s_ref[0, h]
            outs.append(o_all[h * c_len:(h + 1) * c_len] + _dot_nt(qs[:, sl], s_old.astype(BF16)))
            s_ref[0, h] = s_old * dec[:, sl] + jnp.dot(vt[sl, :], ks[:, sl], preferred_element_type=F32)
        o_ref[rows, :] = jnp.concatenate(outs, axis=1)


def _hg_scan(rev, tables, q, f, v, lb, h0):
    jmap, seq, first, last = tables
    n_tok = q.shape[0]
    nblk = n_tok // TOKEN_BLOCK
    n_seq = h0.shape[0]
    d = 1 if rev else 0
    tok = pl.BlockSpec((TOKEN_BLOCK, HG_WIDTH), lambda p, jm, sq, fi, la: (jm[p], 0))
    st = pl.BlockSpec((1, HG_HEADS, HG_DV, HG_DK), lambda p, jm, sq, fi, la: (sq[p], 0, 0, 0))
    return pl.pallas_call(
        functools.partial(_hg_body, rev),
        out_shape=[jax.ShapeDtypeStruct((n_tok, HG_WIDTH), F32),
                   jax.ShapeDtypeStruct((n_seq, HG_HEADS, HG_DV, HG_DK), F32)],
        grid_spec=pltpu.PrefetchScalarGridSpec(
            num_scalar_prefetch=4, grid=(nblk,),
            in_specs=[tok,
                      pl.BlockSpec((TOKEN_BLOCK, HG_KW), lambda p, jm, sq, fi, la: (jm[p], d)),
                      tok,
                      pl.BlockSpec((1, HG_KW), lambda p, jm, sq, fi, la: (0, 0)),
                      st],
            out_specs=[tok, st]),
        compiler_params=_cparams(1),
        name="hg_bwd" if rev else "hg_fwd",
    )(jmap, seq, first, last, q, f, v, lb, h0)


def _merge_body(rowmap_ref, x_ref, ysf_ref, ysb_ref, u_ref, ohf_ref, ohb_ref, g_ref, gs_ref, gh_ref,
                mod_ref, d_ref, bglu_ref, hnw_ref, n2w_ref, wglu_ref, wbs_ref, wbh_ref, wout_ref,
                rhi_ref, rlo_ref, xm_ref, h2_ref, sc_ref):
    row = rowmap_ref[pl.program_id(0)]
    mrow = lambda n: mod_ref[pl.ds(row, 1), n * D_MODEL:(n + 1) * D_MODEL]
    gate1, shift2, scale2 = mrow(2), mrow(3), mrow(4)

    y = ysf_ref[...] + ysb_ref[...] + d_ref[...] * u_ref[...]
    gl = jax.nn.gelu(y)
    z = jnp.dot(gl.astype(BF16), wglu_ref[...], preferred_element_type=F32) + bglu_ref[...]
    y_s5 = gl * jax.nn.sigmoid(z)

    o = ohf_ref[...] + ohb_ref[...]
    normed = []
    for h in range(HG_HEADS):
        oh = o[:, h * HG_DV:(h + 1) * HG_DV]
        normed.append(oh * lax.rsqrt(jnp.mean(oh * oh, axis=-1, keepdims=True) + RMS_EPS))
    gout = g_ref[...].astype(F32)
    y_hg = jnp.concatenate(normed, axis=1) * hnw_ref[...] * (gout * jax.nn.sigmoid(gout))

    merged = (jax.nn.sigmoid(gs_ref[...].astype(F32)) * jnp.dot(y_s5.astype(BF16), wbs_ref[...], preferred_element_type=F32)
              + jax.nn.sigmoid(gh_ref[...].astype(F32)) * jnp.dot(y_hg.astype(BF16), wbh_ref[...], preferred_element_type=F32))
    xm = x_ref[...] + gate1 * jnp.dot(merged.astype(BF16), wout_ref[...], preferred_element_type=F32)
    xm_ref[...] = xm
    h2 = _rms_modulate(xm, n2w_ref[...], scale2, shift2)
    h2_ref[...] = h2
    h_hi, h_lo = _split_bf16(h2)
    logits = _dot_nt(rhi_ref[...], h_hi) + _dot_nt(rhi_ref[...], h_lo) + _dot_nt(rlo_ref[...], h_hi)
    sc_ref[...] = jax.nn.sigmoid(logits)


def _merge(rowmap, x, ysf, ysb, u, ohf, ohb, g, gs, gh, mod, d, bglu, hnw, n2w,
           wglu, wbs, wbh, wout, rhi, rlo):
    n_tok = x.shape[0]
    nblk = n_tok // DENSE_BLOCK
    tok = lambda w: pl.BlockSpec((DENSE_BLOCK, w), lambda j, rm: (j, 0))
    full = lambda shape: pl.BlockSpec(shape, lambda j, rm: (0,) * len(shape))
    return pl.pallas_call(
        _merge_body,
        out_shape=[jax.ShapeDtypeStruct((n_tok, D_MODEL), F32),
                   jax.ShapeDtypeStruct((n_tok, D_MODEL), F32),
                   jax.ShapeDtypeStruct((N_EXPERTS, n_tok), F32)],
        grid_spec=pltpu.PrefetchScalarGridSpec(
            num_scalar_prefetch=1, grid=(nblk,),
            in_specs=[tok(D_MODEL), tok(S5_WIDTH), tok(S5_WIDTH), tok(S5_WIDTH), tok(HG_WIDTH), tok(HG_WIDTH),
                      tok(HG_WIDTH), tok(D_MODEL), tok(D_MODEL),
                      full((SUBLANES, 6 * D_MODEL)), full((1, S5_WIDTH)), full((1, S5_WIDTH)),
                      full((1, HG_WIDTH)), full((1, D_MODEL)),
                      full((S5_WIDTH, S5_WIDTH)), full((S5_WIDTH, D_MODEL)), full((HG_WIDTH, D_MODEL)),
                      full((D_MODEL, D_MODEL)), full((N_EXPERTS, D_MODEL)), full((N_EXPERTS, D_MODEL))],
            out_specs=[tok(D_MODEL), tok(D_MODEL),
                       pl.BlockSpec((N_EXPERTS, DENSE_BLOCK), lambda j, rm: (0, j))]),
        compiler_params=_cparams(1),
        name="merge",
    )(rowmap, x, ysf, ysb, u, ohf, ohb, g, gs, gh, mod, d, bglu, hnw, n2w, wglu, wbs, wbh, wout, rhi, rlo)


def _experts_body(has_prev, be_ref, nused_ref, x_ref, wg_ref, wu_ref, wd_ref, *rest):
    y_ref, wgb_ref, wub_ref, wdb_ref = rest[1:] if has_prev else rest
    b = pl.program_id(0)
    e = be_ref[b]
    prev = be_ref[jnp.maximum(b - 1, 0)]

    @pl.when((b == 0) | (e != prev))
    def _():
        wgb_ref[...] = wg_ref[0, 0].astype(BF16)
        wub_ref[...] = wu_ref[0, 0].astype(BF16)
        wdb_ref[...] = wd_ref[0, 0].astype(BF16)

    @pl.when(b < nused_ref[0])
    def _():
        x = x_ref[...].astype(BF16)
        gate = jnp.dot(x, wgb_ref[...], preferred_element_type=F32)
        up = jnp.dot(x, wub_ref[...], preferred_element_type=F32)
        hid = (gate * jax.nn.sigmoid(gate) * up).astype(BF16)
        y_ref[...] = jnp.dot(hid, wdb_ref[...], preferred_element_type=F32)

    @pl.when(b >= nused_ref[0])
    def _():
        y_ref[...] = jnp.zeros_like(y_ref)


def _experts(layer, first_blk, n_rows_total, blk_expert, nused, xb, w_gate, w_up, w_down, y_prev):
    nblk = xb.shape[0] // MOE_ROWS
    wspec = pl.BlockSpec((1, 1, D_MODEL, D_MODEL), lambda b, be, nu: (layer, be[b], 0, 0))
    has_prev = y_prev is not None
    in_specs = [pl.BlockSpec((MOE_ROWS, D_MODEL), lambda b, be, nu: (b, 0)), wspec, wspec, wspec]
    args = [blk_expert, nused, xb, w_gate, w_up, w_down]
    if has_prev:
        in_specs.append(pl.BlockSpec(memory_space=pl.ANY))
        args.append(y_prev)
    return pl.pallas_call(
        functools.partial(_experts_body, has_prev),
        out_shape=jax.ShapeDtypeStruct((n_rows_total, D_MODEL), F32),
        grid_spec=pltpu.PrefetchScalarGridSpec(
            num_scalar_prefetch=2, grid=(nblk,),
            in_specs=in_specs,
            out_specs=pl.BlockSpec((MOE_ROWS, D_MODEL), lambda b, be, nu: (first_blk + b, 0)),
            scratch_shapes=[pltpu.VMEM((D_MODEL, D_MODEL), BF16)] * 3),
        input_output_aliases={len(args) - 1: 0} if has_prev else {},
        compiler_params=_cparams(1),
        name="experts",
    )(*args)


def _first_max(vals):
    best, idx = vals[0], jnp.zeros(vals[0].shape, jnp.int32)
    for j in range(1, len(vals)):
        upd = vals[j] > best
        idx = jnp.where(upd, j, idx)
        best = jnp.where(upd, vals[j], best)
    return best, idx


def _pick(idx, vals):
    out = vals[-1]
    for j in range(len(vals) - 2, -1, -1):
        out = jnp.where(idx == j, vals[j], out)
    return out


def _router_body(sc_ref, rb_ref, eid_ref, wts_ref, rank_ref, cnt_ref, base_ref):
    j = pl.program_id(0)

    @pl.when(j == 0)
    def _():
        base_ref[...] = jnp.zeros_like(base_ref)

    s = sc_ref[...]
    sel = s + rb_ref[...]
    srow = [s[e:e + 1, :] for e in range(N_EXPERTS)]
    brow = [sel[e:e + 1, :] for e in range(N_EXPERTS)]
    epg = EXPERTS_PER_GROUP
    gscore = []
    for g in range(N_GROUPS):
        x = brow[g * epg:(g + 1) * epg]
        pairs = [x[a] + x[b] for a in range(epg) for b in range(a + 1, epg)]
        gscore.append(functools.reduce(jnp.maximum, pairs))
    _, bestg = _first_max(gscore)
    y = [_pick(bestg, [brow[g * epg + i] for g in range(N_GROUPS)]) for i in range(epg)]
    sy = [_pick(bestg, [srow[g * epg + i] for g in range(N_GROUPS)]) for i in range(epg)]
    _, i1 = _first_max(y)
    v2 = jnp.full(y[0].shape, -jnp.inf, F32)
    i2 = jnp.zeros(y[0].shape, jnp.int32)
    for i in range(epg):
        upd = (i1 != i) & (y[i] > v2)
        i2 = jnp.where(upd, i, i2)
        v2 = jnp.where(upd, y[i], v2)
    e0 = bestg * epg + i1
    e1 = bestg * epg + i2
    w0 = _pick(i1, sy)
    w1 = _pick(i2, sy)
    wsum = w0 + w1
    eid_ref[0:1, :] = e0
    eid_ref[1:2, :] = e1
    wts_ref[0:1, :] = w0 / wsum
    wts_ref[1:2, :] = w1 / wsum

    eiota = lax.broadcasted_iota(jnp.int32, s.shape, 0)
    m0 = eiota == e0
    m1 = eiota == e1
    member = (m0 | m1).astype(BF16)
    tb = s.shape[1]
    before = (lax.broadcasted_iota(jnp.int32, (tb, tb), 0) < lax.broadcasted_iota(jnp.int32, (tb, tb), 1)).astype(BF16)
    pos = base_ref[...] + jnp.dot(member, before, preferred_element_type=F32)
    rank_ref[0:1, :] = jnp.sum(jnp.where(m0, pos, 0.0), axis=0, keepdims=True).astype(jnp.int32)
    rank_ref[1:2, :] = jnp.sum(jnp.where(m1, pos, 0.0), axis=0, keepdims=True).astype(jnp.int32)
    base = base_ref[...] + jnp.sum(member.astype(F32), axis=1, keepdims=True)
    base_ref[...] = base
    cnt_ref[...] = jnp.broadcast_to(base, cnt_ref.shape).astype(jnp.int32)


def _router(scores_t, router_b):
    n_tok = scores_t.shape[1]
    tok = lambda rows: pl.BlockSpec((rows, DENSE_BLOCK), lambda j: (0, j))
    return pl.pallas_call(
        _router_body,
        out_shape=[jax.ShapeDtypeStruct((TOP_K, n_tok), jnp.int32), jax.ShapeDtypeStruct((TOP_K, n_tok), F32),
                   jax.ShapeDtypeStruct((TOP_K, n_tok), jnp.int32), jax.ShapeDtypeStruct((N_EXPERTS, LANES), jnp.int32)],
        grid=(n_tok // DENSE_BLOCK,),
        in_specs=[tok(N_EXPERTS), pl.BlockSpec((N_EXPERTS, 1), lambda j: (0, 0))],
        out_specs=[tok(TOP_K), tok(TOP_K), tok(TOP_K), pl.BlockSpec((N_EXPERTS, LANES), lambda j: (0, 0))],
        scratch_shapes=[pltpu.VMEM((N_EXPERTS, 1), F32)],
        compiler_params=_cparams(1),
        name="router",
    )(scores_t, router_b.reshape(N_EXPERTS, 1))


def _dispatch_plan(eid, rank, counts):
    n_tok = eid.shape[1]
    n_slots = n_tok * TOP_K
    padded = (counts + MOE_ROWS - 1) // MOE_ROWS * MOE_ROWS
    pad_end = jnp.cumsum(padded)
    pad_start = pad_end - padded
    e_ids = jnp.arange(N_EXPERTS, dtype=jnp.int32)
    dest = jnp.sum(jnp.where(eid[..., None] == e_ids, pad_start, 0), axis=-1) + rank
    n_blocks = -(-n_slots // MOE_ROWS) + N_EXPERTS
    n_rows = n_blocks * MOE_ROWS
    blk_start = jnp.arange(n_blocks, dtype=jnp.int32) * MOE_ROWS
    blk_expert = jnp.minimum(jnp.sum(pad_end[None, :] <= blk_start[:, None], axis=-1), N_EXPERTS - 1).astype(jnp.int32)
    nused = (pad_end[-1:] // MOE_ROWS).astype(jnp.int32)
    n_fill = n_rows - n_slots
    gap = padded - counts
    gap_before = jnp.cumsum(gap) - gap
    j = jnp.arange(n_fill, dtype=jnp.int32)
    in_gap = (j[:, None] >= gap_before[None, :]) & (j[:, None] < (gap_before + gap)[None, :])
    fill_pos = jnp.where(jnp.any(in_gap, axis=-1),
                         jnp.sum(jnp.where(in_gap, (pad_start + counts - gap_before)[None, :], 0), axis=-1) + j,
                         pad_end[-1] - jnp.sum(gap) + j)
    keys = jnp.concatenate([dest.reshape(-1), fill_pos])
    vals = jnp.concatenate([jnp.tile(jnp.arange(n_tok, dtype=jnp.int32), TOP_K), j % n_tok])
    _, row_tok = lax.sort_key_val(keys, vals)
    return dest, blk_expert, nused, row_tok


def _final_body(n_ctx_blk, n_r, rowmap_ref, x_ref, *refs):
    r_refs, (gate_ref, nw_ref, oc_ref, ol_ref) = refs[:n_r], refs[n_r:]
    j = pl.program_id(0)
    row = rowmap_ref[j]
    xn = x_ref[...] + gate_ref[pl.ds(row, 1), :] * _branch_update(r_refs)
    ms = jnp.mean(xn * xn, axis=-1, keepdims=True)
    y = xn * lax.rsqrt(ms + RMS_EPS) * nw_ref[...]

    @pl.when(j < n_ctx_blk)
    def _():
        oc_ref[...] = y

    @pl.when(j >= n_ctx_blk)
    def _():
        ol_ref[...] = y


def _final(rowmap, x, r, gate_rows, nw, n_ctx_tok):
    n_tok = x.shape[0]
    n_ctx_blk = n_ctx_tok // DENSE_BLOCK
    tok = lambda w: pl.BlockSpec((DENSE_BLOCK, w), lambda j, rm: (j, 0))
    full = lambda shape: pl.BlockSpec(shape, lambda j, rm: (0,) * len(shape))
    return pl.pallas_call(
        functools.partial(_final_body, n_ctx_blk, len(r)),
        out_shape=[jax.ShapeDtypeStruct((n_ctx_tok, D_MODEL), F32), jax.ShapeDtypeStruct((n_tok - n_ctx_tok, D_MODEL), F32)],
        grid_spec=pltpu.PrefetchScalarGridSpec(
            num_scalar_prefetch=1, grid=(n_tok // DENSE_BLOCK,),
            in_specs=[tok(D_MODEL)] + _row_specs(r) + [full((SUBLANES, D_MODEL)), full((1, D_MODEL))],
            out_specs=[pl.BlockSpec((DENSE_BLOCK, D_MODEL), lambda j, rm: (jnp.minimum(j, n_ctx_blk - 1), 0)),
                       pl.BlockSpec((DENSE_BLOCK, D_MODEL), lambda j, rm: (jnp.maximum(j - n_ctx_blk, 0), 0))]),
        compiler_params=_cparams(1),
        name="final_norm",
    )(rowmap, x, *[a for a, _ in r], gate_rows, nw)


def _grid_pos_embed(rows):
    r = jnp.repeat(jnp.arange(rows, dtype=F32), GRID_W)
    col = jnp.tile(jnp.arange(GRID_W, dtype=F32), rows)
    quarter = D_MODEL // 4
    omega = 1.0 / (10000.0 ** (jnp.arange(quarter, dtype=F32) / quarter))

    def axis_embed(pos):
        ang = pos[:, None] * omega[None, :]
        return jnp.concatenate([jnp.sin(ang), jnp.cos(ang)], axis=-1)

    return jnp.concatenate([axis_embed(r), axis_embed(col)], axis=-1)


def kernel(x_prompt, x_sample, state_s5_re, state_s5_im, state_hg, c, c_ctx, w_ada, b_ada, norm1_w, norm2_w, w_in, s5_lam_re, s5_lam_im, s5_log_step, s5_b_re, s5_b_im, s5_c_re, s5_c_im, s5_d, s5_w_glu, s5_b_glu, hg_lb_logits, hg_norm_w, w_br_s5, w_br_hg, w_out, router_w, router_b, moe_w_gate, moe_w_up, moe_w_down, norm_f_w):
    n_ctx, ctx_len, _ = x_prompt.shape
    n_lat, lat_len, _ = x_sample.shape
    n_ctx_tok = n_ctx * ctx_len
    n_seq = n_ctx + n_lat
    assert n_lat + 1 <= SUBLANES and ctx_len % TOKEN_BLOCK == 0 and lat_len % DENSE_BLOCK == 0
    assert (n_ctx * ctx_len) % DENSE_BLOCK == 0

    seq_np, first_np, last_np, modrow_np = _block_tables(n_ctx, ctx_len, n_lat, lat_len)
    nblk = seq_np.shape[0]
    jmap_f = np.arange(nblk, dtype=np.int32)
    jmap_b = jmap_f[::-1].copy()
    tab_f = tuple(jnp.asarray(t) for t in (jmap_f, seq_np, first_np, last_np))
    tab_b = tuple(jnp.asarray(t) for t in (jmap_b, seq_np[::-1].copy(), last_np[::-1].copy(), first_np[::-1].copy()))
    tab_s5 = tuple(jnp.asarray(t) for t in (seq_np, first_np, last_np))
    rowmap = jnp.asarray(modrow_np.reshape(-1, DENSE_BLOCK // TOKEN_BLOCK)[:, 0])

    cvecs = jnp.zeros((SUBLANES, D_MODEL), F32).at[0].set(c_ctx).at[1:1 + n_lat].set(c)
    mod_all = _adaln(cvecs, w_ada, b_ada)

    probs = jax.nn.softmax(hg_lb_logits.astype(F32), axis=0)
    lower_bounds = jnp.cumsum(probs, axis=0) - probs[0]

    pos = _grid_pos_embed(lat_len // GRID_W)
    x = (x_prompt.reshape(n_ctx_tok, D_MODEL), x_sample.reshape(n_lat * lat_len, D_MODEL))
    n_ctx_blk = n_ctx_tok // DENSE_BLOCK
    n_dense_blk = (n_ctx_tok + n_lat * lat_len) // DENSE_BLOCK
    pos_blocks = lat_len // DENSE_BLOCK
    resid = ((pos, lambda j: jnp.maximum(j - n_ctx_blk, 0) % pos_blocks),)
    gate_rows = jnp.ones((SUBLANES, D_MODEL), F32).at[0].set(0.0)

    router_t = router_w.T.astype(F32)
    r_hi, r_lo = _split_bf16(router_t)
    ns = S5_BLOCK_STATES
    zeros_s5 = jnp.zeros((n_ctx, S5_COLBLOCKS, 2, ns), F32)
    zeros_hg = jnp.zeros((n_ctx, HG_HEADS, HG_DV, HG_DK), F32)

    new_s5_re, new_s5_im, new_hg = [], [], []
    for l in range(DEPTH):
        mod = mod_all[l]
        x, u, q_in, f_raw, i_in, g_out, gate_s5, gate_hg = _inproj(
            rowmap, x, resid, gate_rows, mod, norm1_w[l].reshape(1, D_MODEL), w_in[l].astype(BF16))

        par = [_s5_params(s5_lam_re[l, d], s5_lam_im[l, d], s5_log_step[l, d], s5_b_re[l], s5_b_im[l], s5_c_re[l], s5_c_im[l])
               for d in range(2)]
        h0_s5 = jnp.stack([
            jnp.concatenate([zeros_s5, jnp.stack([state_s5_re[:, l, d], state_s5_im[:, l, d]], axis=1).astype(F32).reshape(
                n_lat, 2, S5_COLBLOCKS, ns).swapaxes(1, 2)], axis=0) for d in range(2)])
        y_f, y_b, fin_f, fin_b = _s5_scan(tab_s5, u, jnp.stack([par[0][0], par[1][0]]), par[0][1],
                                          jnp.stack([par[0][2], par[1][2]]), h0_s5)
        ys = [y_f, y_b]
        s5_fin_re = [fin[:n_ctx, :, 0].reshape(n_ctx, S5_GROUPS, S5_STATE) for fin in (fin_f, fin_b)]
        s5_fin_im = [fin[:n_ctx, :, 1].reshape(n_ctx, S5_GROUPS, S5_STATE) for fin in (fin_f, fin_b)]

        os_, hg_fin = [], []
        for d, tabs in enumerate((tab_f, tab_b)):

            h0 = jnp.concatenate([zeros_hg, jnp.swapaxes(state_hg[:, l, d].astype(F32), -1, -2)], axis=0)
            o_d, hfin = _hg_scan(bool(d), tabs, q_in, f_raw, i_in, lower_bounds[l, d].reshape(1, HG_KW), h0)
            os_.append(o_d)
            hg_fin.append(jnp.swapaxes(hfin[:n_ctx], -1, -2))
        new_s5_re.append(jnp.stack(s5_fin_re, axis=1))
        new_s5_im.append(jnp.stack(s5_fin_im, axis=1))
        new_hg.append(jnp.stack(hg_fin, axis=1))

        x, h2, scores_t = _merge(
            rowmap, x, ys[0], ys[1], u, os_[0], os_[1], g_out, gate_s5, gate_hg, mod,
            s5_d[l].reshape(1, S5_WIDTH), s5_b_glu[l].reshape(1, S5_WIDTH), hg_norm_w[l].reshape(1, HG_WIDTH),
            norm2_w[l].reshape(1, D_MODEL), s5_w_glu[l].astype(BF16), w_br_s5[l].astype(BF16),
            w_br_hg[l].astype(BF16), w_out[l].astype(BF16), r_hi, r_lo)

        eid, wts, rank, counts = _router(scores_t, router_b.astype(F32))
        dest, blk_expert, nused, row_tok = _dispatch_plan(eid, rank, counts[:, 0])
        n_blocks = blk_expert.shape[0]
        cb = n_blocks // MOE_CHUNKS
        yb = None
        for ch in range(MOE_CHUNKS):
            xb = h2[row_tok[ch * cb * MOE_ROWS:(ch + 1) * cb * MOE_ROWS]]
            yb = _experts(l, ch * cb, n_blocks * MOE_ROWS, blk_expert[ch * cb:(ch + 1) * cb],
                          jnp.clip(nused - ch * cb, 0, cb), xb, moe_w_gate, moe_w_up, moe_w_down, yb)
        rows = yb[dest.reshape(-1)]
        resid = ((rows, _same_block), (rows, lambda j: j + n_dense_blk), (wts.T, _same_block))
        gate_rows = mod[:, 5 * D_MODEL:6 * D_MODEL]

    y_ctx, y_lat = _final(rowmap, x, resid, gate_rows, norm_f_w.reshape(1, D_MODEL), n_ctx_tok)
    y_prompt = y_ctx.reshape(n_ctx, ctx_len, D_MODEL)
    y_sample = y_lat.reshape(n_lat, lat_len, D_MODEL)
    new_state_s5_re = jnp.stack(new_s5_re, axis=1).astype(state_s5_re.dtype)
    new_state_s5_im = jnp.stack(new_s5_im, axis=1).astype(state_s5_im.dtype)
    new_state_hg = jnp.stack(new_hg, axis=1).astype(state_hg.dtype)
    return (y_prompt, y_sample, new_state_s5_re, new_state_s5_im, new_state_hg)
```

```python
import functools
import math

import jax
import jax.numpy as jnp
import numpy as np
from jax import lax
from jax.experimental import pallas as pl
from jax.experimental.pallas import tpu as pltpu

F32 = jnp.float32
BF16 = jnp.bfloat16

D_MODEL = 1024
DEPTH = 4
GRID_W = 64
RMS_EPS = 1e-6
S5_WIDTH = 512
S5_GROUP = 16
S5_GROUPS = 32
S5_STATE = 64
HG_HEADS = 4
HG_DK = 128
HG_DV = 128
HG_WIDTH = 512
HG_KW = 512
HG_CHUNK = 64
HG_SUB = 16
N_EXPERTS = 16
N_GROUPS = 4
EXPERTS_PER_GROUP = 4
TOP_K = 2
IN_COLS = 5120

LANES = 128
SUBLANES = 8
VMEM_LIMIT_BYTES = 56 * 1024 * 1024

TOKEN_BLOCK = 256
DENSE_BLOCK = 512
S5_SEG = TOKEN_BLOCK // SUBLANES
MXU_DEPTH = 256
S5_CHANNELS = MXU_DEPTH
S5_COLBLOCKS = S5_WIDTH // S5_CHANNELS
S5_BLOCK_STATES = (S5_CHANNELS // S5_GROUP) * S5_STATE
S5_PITCH = S5_SEG + 4
_S5_PERM = np.zeros((TOKEN_BLOCK, TOKEN_BLOCK), np.float32)
_S5_PERM[np.arange(TOKEN_BLOCK), (np.arange(TOKEN_BLOCK) % SUBLANES) * S5_SEG + np.arange(TOKEN_BLOCK) // SUBLANES] = 1.0
MOE_ROWS = 256
MOE_CHUNKS = 4


def _block_tables(n_ctx_seq, ctx_len, n_lat_seq, lat_len):
    seq, first, last, modrow = [], [], [], []
    for s in range(n_ctx_seq):
        nb = ctx_len // TOKEN_BLOCK
        for b in range(nb):
            seq.append(s); first.append(int(b == 0)); last.append(int(b == nb - 1)); modrow.append(0)
    for s in range(n_lat_seq):
        nb = lat_len // TOKEN_BLOCK
        for b in range(nb):
            seq.append(n_ctx_seq + s); first.append(int(b == 0)); last.append(int(b == nb - 1))
            modrow.append(1 + s)
    return (np.asarray(seq, np.int32), np.asarray(first, np.int32),
            np.asarray(last, np.int32), np.asarray(modrow, np.int32))


def _cparams(n_axes):
    return pltpu.CompilerParams(dimension_semantics=("arbitrary",) * n_axes,
                                vmem_limit_bytes=VMEM_LIMIT_BYTES)


def _split_bf16(a):
    hi = a.astype(BF16)
    lo = (a - hi.astype(F32)).astype(BF16)
    return hi, lo


def _dot_nt(a, b):
    return lax.dot_general(a, b, (((1,), (1,)), ((), ())), preferred_element_type=F32)


def _adaln_body(c_ref, w_ref, b_ref, o_ref):
    a = c_ref[...]
    a = a * jax.nn.sigmoid(a)
    a_hi, a_lo = _split_bf16(a)
    w_hi, w_lo = _split_bf16(w_ref[0])
    acc = jnp.dot(a_hi, w_hi, preferred_element_type=F32)
    acc += jnp.dot(a_lo, w_hi, preferred_element_type=F32)
    acc += jnp.dot(a_hi, w_lo, preferred_element_type=F32)
    o_ref[0] = acc + b_ref[0]


def _adaln(cvecs, w_ada, b_ada):
    tn = 1536
    n_tiles = (6 * D_MODEL) // tn
    return pl.pallas_call(
        _adaln_body,
        out_shape=jax.ShapeDtypeStruct((DEPTH, SUBLANES, 6 * D_MODEL), F32),
        grid=(DEPTH, n_tiles),
        in_specs=[
            pl.BlockSpec((SUBLANES, D_MODEL), lambda l, n: (0, 0)),
            pl.BlockSpec((1, D_MODEL, tn), lambda l, n: (l, 0, n)),
            pl.BlockSpec((1, 1, tn), lambda l, n: (l, 0, n)),
        ],
        out_specs=pl.BlockSpec((1, SUBLANES, tn), lambda l, n: (l, 0, n)),
        compiler_params=_cparams(2),
        name="adaln",
    )(cvecs, w_ada, b_ada.reshape(DEPTH, 1, 6 * D_MODEL))


_IN_SPLITS = ((0, 512), (512, 1024), (1024, 2048), (2048, 2560), (2560, 3072), (3072, 4096), (4096, 5120))


def _rms_modulate(x, nw, scale, shift):
    ms = jnp.mean(x * x, axis=-1, keepdims=True)
    return (x * lax.rsqrt(ms + RMS_EPS)) * nw * (1.0 + scale) + shift


def _branch_update(r_refs):
    if len(r_refs) == 1:
        return r_refs[0][...]
    ya_ref, yb_ref, w_ref = r_refs
    w = w_ref[...]
    return ya_ref[...] * w[:, 0:1] + yb_ref[...] * w[:, 1:2]


def _inproj_body(n_x, n_ctx_blk, n_r, rowmap_ref, *refs):
    x_refs, r_refs = refs[:n_x], refs[n_x:n_x + n_r]
    gate_ref, mod_ref, nw_ref, w_ref, xo_ref, u_ref, q_ref, f_ref, i_ref, g_ref, gs_ref, gh_ref = refs[n_x + n_r:]
    j = pl.program_id(0)
    row = rowmap_ref[j]
    gate = gate_ref[pl.ds(row, 1), :]
    x = x_refs[0][...] if n_x == 1 else jnp.where(j < n_ctx_blk, x_refs[0][...], x_refs[1][...])
    xn = x + gate * _branch_update(r_refs)
    xo_ref[...] = xn
    shift = mod_ref[pl.ds(row, 1), 0:D_MODEL]
    scale = mod_ref[pl.ds(row, 1), D_MODEL:2 * D_MODEL]
    h = _rms_modulate(xn, nw_ref[...], scale, shift).astype(BF16)
    outs = (u_ref, q_ref, f_ref, i_ref, g_ref, gs_ref, gh_ref)
    for ref, (a, b) in zip(outs, _IN_SPLITS):
        ref[...] = jnp.dot(h, w_ref[:, a:b], preferred_element_type=F32).astype(ref.dtype)


def _row_specs(arrays_and_maps):
    return [pl.BlockSpec((DENSE_BLOCK, a.shape[1]), lambda j, rm, f=f: (f(j), 0)) for a, f in arrays_and_maps]


def _same_block(j):
    return j


def _inproj(rowmap, x, r, gate_rows, mod, nw, w_bf16):
    xs = x if isinstance(x, tuple) else (x,)
    n_tok = sum(a.shape[0] for a in xs)
    nblk = n_tok // DENSE_BLOCK
    n_ctx_blk = xs[0].shape[0] // DENSE_BLOCK
    x_maps = [_same_block] if len(xs) == 1 else [lambda j: jnp.minimum(j, n_ctx_blk - 1), lambda j: jnp.maximum(j - n_ctx_blk, 0)]
    tok = lambda w: pl.BlockSpec((DENSE_BLOCK, w), lambda j, rm: (j, 0))
    full = lambda shape: pl.BlockSpec(shape, lambda j, rm: (0,) * len(shape))
    widths = [b - a for a, b in _IN_SPLITS]
    dtypes = [F32, BF16, F32, BF16, BF16, BF16, BF16]
    return pl.pallas_call(
        functools.partial(_inproj_body, len(xs), n_ctx_blk, len(r)),
        out_shape=[jax.ShapeDtypeStruct((n_tok, D_MODEL), F32)]
        + [jax.ShapeDtypeStruct((n_tok, w), dt) for w, dt in zip(widths, dtypes)],
        grid_spec=pltpu.PrefetchScalarGridSpec(
            num_scalar_prefetch=1, grid=(nblk,),
            in_specs=_row_specs(list(zip(xs, x_maps))) + _row_specs(r)
            + [full((SUBLANES, D_MODEL)), full((SUBLANES, 6 * D_MODEL)), full((1, D_MODEL)), full((D_MODEL, IN_COLS))],
            out_specs=[tok(D_MODEL)] + [tok(w) for w in widths]),
        compiler_params=_cparams(1),
        name="inproj",
    )(rowmap, *xs, *[a for a, _ in r], gate_rows, mod, nw, w_bf16)


def _s5_direction(rev, u_ref, perm_ref, bd_ref, cd_ref, lam_ref,
                  y_ref, fin_ref, bu_ref, sb_ref, yp_ref, carry_ref):
    ns = S5_BLOCK_STATES

    up = jnp.dot(perm_ref[...], u_ref[...].astype(BF16), preferred_element_type=F32).astype(BF16)
    bu_ref[...] = jnp.dot(up, bd_ref[0, 0], preferred_element_type=F32)

    lre = lam_ref[0, 0, 0:SUBLANES, :]
    lim = lam_ref[0, 0, SUBLANES:2 * SUBLANES, :]
    order = list(range(S5_SEG))
    if rev:
        order = order[::-1]

    def step(sr, si, k):
        br = bu_ref[pl.ds(SUBLANES * k, SUBLANES), 0:ns]
        bi = bu_ref[pl.ds(SUBLANES * k, SUBLANES), ns:2 * ns]
        return lre * sr - lim * si + br, lre * si + lim * sr + bi

    sr = jnp.zeros((SUBLANES, ns), F32)
    si = jnp.zeros((SUBLANES, ns), F32)
    for k in order:
        sr, si = step(sr, si, k)

    cr = carry_ref[0:1, :]
    ci = carry_ref[1:2, :]
    lkre = lam_ref[0, 0, 2 * SUBLANES:2 * SUBLANES + 1, :]
    lkim = lam_ref[0, 0, 2 * SUBLANES + 1:2 * SUBLANES + 2, :]
    start_r = [None] * SUBLANES
    start_i = [None] * SUBLANES
    seg_order = list(range(SUBLANES))
    if rev:
        seg_order = seg_order[::-1]
    for i in seg_order:
        start_r[i] = cr
        start_i[i] = ci
        er = sr[i:i + 1]
        ei = si[i:i + 1]
        cr, ci = lkre * cr - lkim * ci + er, lkre * ci + lkim * cr + ei
    carry_ref[0:1, :] = cr
    carry_ref[1:2, :] = ci
    fin_ref[0, 0, 0:1, :] = cr
    fin_ref[0, 0, 1:2, :] = ci

    sr = jnp.concatenate(start_r, axis=0)
    si = jnp.concatenate(start_i, axis=0)
    for n in range(0, S5_SEG, 2):
        k0, k1 = order[n], order[n + 1]
        sr0, si0 = step(sr, si, k0)
        sr, si = step(sr0, si0, k1)
        lo_r, hi_r = (sr0, sr) if k0 < k1 else (sr, sr0)
        lo_i, hi_i = (si0, si) if k0 < k1 else (si, si0)
        base = SUBLANES * min(k0, k1)
        sb_ref[pl.ds(base, 2 * SUBLANES), 0:ns] = jnp.concatenate([lo_r, hi_r], axis=0).astype(BF16)
        sb_ref[pl.ds(base, 2 * SUBLANES), ns:2 * ns] = jnp.concatenate([lo_i, hi_i], axis=0).astype(BF16)

    y = jnp.dot(sb_ref[...], cd_ref[0], preferred_element_type=F32)
    for h in range(S5_CHANNELS // LANES):
        for k in range(S5_SEG):
            yp_ref[h, pl.ds(k, SUBLANES, stride=S5_PITCH), :] = y[SUBLANES * k:SUBLANES * (k + 1), h * LANES:(h + 1) * LANES]
        for i in range(SUBLANES):
            y_ref[i * S5_SEG:(i + 1) * S5_SEG, h * LANES:(h + 1) * LANES] = yp_ref[h, pl.ds(i * S5_PITCH, S5_SEG), :]


def _s5_body(seq_ref, first_ref, last_ref,
             uf_ref, ub_ref, perm_ref, bdf_ref, bdb_ref, cd_ref, lamf_ref, lamb_ref, h0f_ref, h0b_ref,
             yf_ref, yb_ref, finf_ref, finb_ref,
             buf_ref, bub_ref, sbf_ref, sbb_ref, ypf_ref, ypb_ref, carryf_ref, carryb_ref):
    p = pl.program_id(1)
    q = pl.num_programs(1) - 1 - p

    @pl.when(first_ref[p] == 1)
    def _():
        carryf_ref[...] = h0f_ref[0, 0, 0]

    @pl.when(last_ref[q] == 1)
    def _():
        carryb_ref[...] = h0b_ref[0, 0, 0]

    _s5_direction(False, uf_ref, perm_ref, bdf_ref, cd_ref, lamf_ref,
                  yf_ref, finf_ref, buf_ref, sbf_ref, ypf_ref, carryf_ref)
    _s5_direction(True, ub_ref, perm_ref, bdb_ref, cd_ref, lamb_ref,
                  yb_ref, finb_ref, bub_ref, sbb_ref, ypb_ref, carryb_ref)


def _s5_scan(tables, u, bd, cd, lam, h0):
    seq, first, last = tables
    n_tok = u.shape[0]
    nblk = n_tok // TOKEN_BLOCK
    n_seq = h0.shape[1]
    ns = S5_BLOCK_STATES
    lam_rows = lam.shape[2]
    grid = (S5_COLBLOCKS, nblk)
    fwd = lambda a, p, sq, fi, la: p
    bwd = lambda a, p, sq, fi, la: nblk - 1 - p
    tok = lambda blk: pl.BlockSpec((TOKEN_BLOCK, S5_CHANNELS), lambda a, p, sq, fi, la: (blk(a, p, sq, fi, la), a))
    par = lambda d, shape: pl.BlockSpec((1, 1) + shape, lambda a, p, sq, fi, la: (d, a, 0, 0))
    st = lambda d, blk: pl.BlockSpec((1, 1, 1, 2, ns), lambda a, p, sq, fi, la: (d, sq[blk(a, p, sq, fi, la)], a, 0, 0))
    fin = lambda blk: pl.BlockSpec((1, 1, 2, ns), lambda a, p, sq, fi, la: (sq[blk(a, p, sq, fi, la)], a, 0, 0))
    fin_shape = jax.ShapeDtypeStruct((n_seq, S5_COLBLOCKS, 2, ns), F32)
    return pl.pallas_call(
        _s5_body,
        out_shape=[jax.ShapeDtypeStruct((n_tok, S5_WIDTH), F32), jax.ShapeDtypeStruct((n_tok, S5_WIDTH), F32),
                   fin_shape, fin_shape],
        grid_spec=pltpu.PrefetchScalarGridSpec(
            num_scalar_prefetch=3, grid=grid,
            in_specs=[
                tok(fwd), tok(bwd),
                pl.BlockSpec((TOKEN_BLOCK, TOKEN_BLOCK), lambda a, p, sq, fi, la: (0, 0)),
                par(0, (S5_CHANNELS, 2 * ns)), par(1, (S5_CHANNELS, 2 * ns)),
                pl.BlockSpec((1, 2 * ns, S5_CHANNELS), lambda a, p, sq, fi, la: (a, 0, 0)),
                par(0, (lam_rows, ns)), par(1, (lam_rows, ns)),
                st(0, fwd), st(1, bwd)],
            out_specs=[tok(fwd), tok(bwd), fin(fwd), fin(bwd)],
            scratch_shapes=[pltpu.VMEM((TOKEN_BLOCK, 2 * ns), F32)] * 2
            + [pltpu.VMEM((TOKEN_BLOCK, 2 * ns), BF16)] * 2
            + [pltpu.VMEM((S5_CHANNELS // LANES, SUBLANES * S5_PITCH, LANES), F32)] * 2
            + [pltpu.VMEM((2, ns), F32)] * 2),
        compiler_params=_cparams(2),
        name="s5_scan",
    )(seq, first, last, u, u, jnp.asarray(_S5_PERM, BF16), bd, bd, cd, lam, lam, h0, h0)


def _s5_params(lam_re, lam_im, log_step, b_re, b_im, c_re, c_im):
    g, p, c = S5_GROUPS, S5_STATE, S5_GROUP
    gl = S5_CHANNELS // c
    lr = jnp.minimum(lam_re, -1e-4)
    li = lam_im
    step = jnp.exp(log_step)[:, None]
    mag = jnp.exp(lr * step)
    ar, ai = mag * jnp.cos(li * step), mag * jnp.sin(li * step)
    magk = jnp.exp(lr * step * S5_SEG)
    akr, aki = magk * jnp.cos(li * step * S5_SEG), magk * jnp.sin(li * step * S5_SEG)
    den = lr * lr + li * li
    cr = ((ar - 1.0) * lr + ai * li) / den
    ci = (ai * lr - (ar - 1.0) * li) / den
    bbr = cr[:, :, None] * b_re - ci[:, :, None] * b_im
    bbi = cr[:, :, None] * b_im + ci[:, :, None] * b_re
    eye = jnp.eye(gl, dtype=F32)

    def bmat(bb):
        bb = bb.reshape(S5_COLBLOCKS, gl, p, c)
        return jnp.einsum('agpc,gh->agchp', bb, eye).reshape(S5_COLBLOCKS, gl * c, gl * p)

    def cmat(cc):
        cc = cc.reshape(S5_COLBLOCKS, gl, c, p)
        return jnp.einsum('agcp,gh->agphc', cc, eye).reshape(S5_COLBLOCKS, gl * p, gl * c)

    bd = jnp.concatenate([bmat(bbr), bmat(bbi)], axis=2).astype(BF16)
    cd = jnp.concatenate([cmat(c_re), -cmat(c_im)], axis=1).astype(BF16)

    def lanes(v, rows):
        v = v.reshape(S5_COLBLOCKS, 1, gl * p)
        return jnp.broadcast_to(v, (S5_COLBLOCKS, rows, gl * p))

    lam = jnp.concatenate([lanes(ar, SUBLANES), lanes(ai, SUBLANES), lanes(akr, 1), lanes(aki, 1)], axis=1)
    return bd, cd, lam


def _hg_body(rev, jmap_ref, seq_ref, first_ref, last_ref,
             q_ref, f_ref, v_ref, lb_ref, h0_ref, o_ref, s_ref):
    p = pl.program_id(0)
    c_len = HG_CHUNK

    @pl.when(first_ref[p] == 1)
    def _():
        s_ref[...] = h0_ref[...]

    sub = HG_SUB
    n_sub = c_len // sub
    ri = lax.broadcasted_iota(jnp.int32, (c_len, c_len), 0)
    ci = lax.broadcasted_iota(jnp.int32, (c_len, c_len), 1)
    mask = (ri <= ci) if rev else (ri >= ci)
    tri = mask.astype(BF16)
    rb, cb = ri // sub, ci // sub
    diag_mask = mask & (rb == cb)
    off_mask = (cb > rb) if rev else (cb < rb)
    lb = lb_ref[...]
    tot_row = 0 if rev else c_len - 1
    n_chunks = TOKEN_BLOCK // c_len
    chunk_order = range(n_chunks - 1, -1, -1) if rev else range(n_chunks)
    for c in chunk_order:
        rows = pl.ds(c * c_len, c_len)
        qin = q_ref[rows, :].astype(F32)
        q = qin * jax.nn.sigmoid(qin) * (HG_DK ** -0.5)
        f = lb + (1.0 - lb) * jax.nn.sigmoid(f_ref[rows, :])
        k = 1.0 - f
        g = jnp.log(f)
        g0 = g.astype(BF16)
        r1 = g - g0.astype(F32)
        g1 = r1.astype(BF16)
        g2 = (r1 - g1.astype(F32)).astype(BF16)
        b = (jnp.dot(tri, g0, preferred_element_type=F32) + jnp.dot(tri, g1, preferred_element_type=F32)
             + jnp.dot(tri, g2, preferred_element_type=F32))
        btot = b[tot_row:tot_row + 1, :]
        bmid = jnp.concatenate(
            [jnp.broadcast_to(b[i * sub + sub // 2:i * sub + sub // 2 + 1, :], (sub, HG_KW)) for i in range(n_sub)], axis=0)
        qd = (q * jnp.exp(b - bmid)).astype(BF16)
        kd = (k * jnp.exp(bmid - b)).astype(BF16)
        qo, ko = {}, {}
        for i in range(n_sub):
            edge = (i + 1) * sub if rev else i * sub - 1
            if 0 <= edge < c_len:
                bref = b[edge:edge + 1, :]
                rws = slice(i * sub, (i + 1) * sub)
                qo[i] = (q[rws] * jnp.exp(b[rws] - bref)).astype(BF16)
                ko[i] = (k * jnp.exp(jnp.minimum(bref - b, 0.0))).astype(BF16)
        qs = (q * jnp.exp(b)).astype(BF16)
        ks = (k * jnp.exp(btot - b)).astype(BF16)
        dec = jnp.exp(btot)
        v = v_ref[rows, :]
        vt = v.astype(F32).T.astype(BF16)
        outs = []
        for h in range(HG_HEADS):
            sl = slice(h * HG_DK, (h + 1) * HG_DK)
            near = _dot_nt(qd[:, sl], kd[:, sl])
            far = jnp.concatenate(
                [_dot_nt(qo[i][:, sl], ko[i][:, sl]) if i in qo else jnp.zeros((sub, c_len), F32)
                 for i in range(n_sub)], axis=0)
            att = jnp.where(diag_mask, near, jnp.where(off_mask, far, 0.0)).astype(BF16)
            s_old = s_ref[0, h]
            o_h = jnp.dot(att, v[:, sl], preferred_element_type=F32) + _dot_nt(qs[:, sl], s_old.astype(BF16))
            s_ref[0, h] = s_old * dec[:, sl] + jnp.dot(vt[sl, :], ks[:, sl], preferred_element_type=F32)
            outs.append(o_h)
        o_ref[rows, :] = jnp.concatenate(outs, axis=1)


def _hg_scan(rev, tables, q, f, v, lb, h0):
    jmap, seq, first, last = tables
    n_tok = q.shape[0]
    nblk = n_tok // TOKEN_BLOCK
    n_seq = h0.shape[0]
    d = 1 if rev else 0
    tok = pl.BlockSpec((TOKEN_BLOCK, HG_WIDTH), lambda p, jm, sq, fi, la: (jm[p], 0))
    st = pl.BlockSpec((1, HG_HEADS, HG_DV, HG_DK), lambda p, jm, sq, fi, la: (sq[p], 0, 0, 0))
    return pl.pallas_call(
        functools.partial(_hg_body, rev),
        out_shape=[jax.ShapeDtypeStruct((n_tok, HG_WIDTH), F32),
                   jax.ShapeDtypeStruct((n_seq, HG_HEADS, HG_DV, HG_DK), F32)],
        grid_spec=pltpu.PrefetchScalarGridSpec(
            num_scalar_prefetch=4, grid=(nblk,),
            in_specs=[tok,
                      pl.BlockSpec((TOKEN_BLOCK, HG_KW), lambda p, jm, sq, fi, la: (jm[p], d)),
                      tok,
                      pl.BlockSpec((1, HG_KW), lambda p, jm, sq, fi, la: (0, 0)),
                      st],
            out_specs=[tok, st]),
        compiler_params=_cparams(1),
        name="hg_bwd" if rev else "hg_fwd",
    )(jmap, seq, first, last, q, f, v, lb, h0)


def _merge_body(rowmap_ref, x_ref, ysf_ref, ysb_ref, u_ref, ohf_ref, ohb_ref, g_ref, gs_ref, gh_ref,
                mod_ref, d_ref, bglu_ref, hnw_ref, n2w_ref, wglu_ref, wbs_ref, wbh_ref, wout_ref,
                rhi_ref, rlo_ref, xm_ref, h2_ref, sc_ref):
    row = rowmap_ref[pl.program_id(0)]
    mrow = lambda n: mod_ref[pl.ds(row, 1), n * D_MODEL:(n + 1) * D_MODEL]
    gate1, shift2, scale2 = mrow(2), mrow(3), mrow(4)

    y = ysf_ref[...] + ysb_ref[...] + d_ref[...] * u_ref[...]
    gl = jax.nn.gelu(y)
    z = jnp.dot(gl.astype(BF16), wglu_ref[...], preferred_element_type=F32) + bglu_ref[...]
    y_s5 = gl * jax.nn.sigmoid(z)

    o = ohf_ref[...] + ohb_ref[...]
    normed = []
    for h in range(HG_HEADS):
        oh = o[:, h * HG_DV:(h + 1) * HG_DV]
        normed.append(oh * lax.rsqrt(jnp.mean(oh * oh, axis=-1, keepdims=True) + RMS_EPS))
    gout = g_ref[...].astype(F32)
    y_hg = jnp.concatenate(normed, axis=1) * hnw_ref[...] * (gout * jax.nn.sigmoid(gout))

    merged = (jax.nn.sigmoid(gs_ref[...].astype(F32)) * jnp.dot(y_s5.astype(BF16), wbs_ref[...], preferred_element_type=F32)
              + jax.nn.sigmoid(gh_ref[...].astype(F32)) * jnp.dot(y_hg.astype(BF16), wbh_ref[...], preferred_element_type=F32))
    xm = x_ref[...] + gate1 * jnp.dot(merged.astype(BF16), wout_ref[...], preferred_element_type=F32)
    xm_ref[...] = xm
    h2 = _rms_modulate(xm, n2w_ref[...], scale2, shift2)
    h2_ref[...] = h2
    h_hi, h_lo = _split_bf16(h2)
    logits = _dot_nt(rhi_ref[...], h_hi) + _dot_nt(rhi_ref[...], h_lo) + _dot_nt(rlo_ref[...], h_hi)
    sc_ref[...] = jax.nn.sigmoid(logits)


def _merge(rowmap, x, ysf, ysb, u, ohf, ohb, g, gs, gh, mod, d, bglu, hnw, n2w,
           wglu, wbs, wbh, wout, rhi, rlo):
    n_tok = x.shape[0]
    nblk = n_tok // DENSE_BLOCK
    tok = lambda w: pl.BlockSpec((DENSE_BLOCK, w), lambda j, rm: (j, 0))
    full = lambda shape: pl.BlockSpec(shape, lambda j, rm: (0,) * len(shape))
    return pl.pallas_call(
        _merge_body,
        out_shape=[jax.ShapeDtypeStruct((n_tok, D_MODEL), F32),
                   jax.ShapeDtypeStruct((n_tok, D_MODEL), F32),
                   jax.ShapeDtypeStruct((N_EXPERTS, n_tok), F32)],
        grid_spec=pltpu.PrefetchScalarGridSpec(
            num_scalar_prefetch=1, grid=(nblk,),
            in_specs=[tok(D_MODEL), tok(S5_WIDTH), tok(S5_WIDTH), tok(S5_WIDTH), tok(HG_WIDTH), tok(HG_WIDTH),
                      tok(HG_WIDTH), tok(D_MODEL), tok(D_MODEL),
                      full((SUBLANES, 6 * D_MODEL)), full((1, S5_WIDTH)), full((1, S5_WIDTH)),
                      full((1, HG_WIDTH)), full((1, D_MODEL)),
                      full((S5_WIDTH, S5_WIDTH)), full((S5_WIDTH, D_MODEL)), full((HG_WIDTH, D_MODEL)),
                      full((D_MODEL, D_MODEL)), full((N_EXPERTS, D_MODEL)), full((N_EXPERTS, D_MODEL))],
            out_specs=[tok(D_MODEL), tok(D_MODEL),
                       pl.BlockSpec((N_EXPERTS, DENSE_BLOCK), lambda j, rm: (0, j))]),
        compiler_params=_cparams(1),
        name="merge",
    )(rowmap, x, ysf, ysb, u, ohf, ohb, g, gs, gh, mod, d, bglu, hnw, n2w, wglu, wbs, wbh, wout, rhi, rlo)


def _experts_body(has_prev, be_ref, nused_ref, x_ref, wg_ref, wu_ref, wd_ref, *rest):
    y_ref, wgb_ref, wub_ref, wdb_ref = rest[1:] if has_prev else rest
    b = pl.program_id(0)
    e = be_ref[b]
    prev = be_ref[jnp.maximum(b - 1, 0)]

    @pl.when((b == 0) | (e != prev))
    def _():
        wgb_ref[...] = wg_ref[0, 0].astype(BF16)
        wub_ref[...] = wu_ref[0, 0].astype(BF16)
        wdb_ref[...] = wd_ref[0, 0].astype(BF16)

    @pl.when(b < nused_ref[0])
    def _():
        x = x_ref[...].astype(BF16)
        gate = jnp.dot(x, wgb_ref[...], preferred_element_type=F32)
        up = jnp.dot(x, wub_ref[...], preferred_element_type=F32)
        hid = (gate * jax.nn.sigmoid(gate) * up).astype(BF16)
        y_ref[...] = jnp.dot(hid, wdb_ref[...], preferred_element_type=F32)

    @pl.when(b >= nused_ref[0])
    def _():
        y_ref[...] = jnp.zeros_like(y_ref)


def _experts(layer, first_blk, n_rows_total, blk_expert, nused, xb, w_gate, w_up, w_down, y_prev):
    nblk = xb.shape[0] // MOE_ROWS
    wspec = pl.BlockSpec((1, 1, D_MODEL, D_MODEL), lambda b, be, nu: (layer, be[b], 0, 0))
    has_prev = y_prev is not None
    in_specs = [pl.BlockSpec((MOE_ROWS, D_MODEL), lambda b, be, nu: (b, 0)), wspec, wspec, wspec]
    args = [blk_expert, nused, xb, w_gate, w_up, w_down]
    if has_prev:
        in_specs.append(pl.BlockSpec(memory_space=pl.ANY))
        args.append(y_prev)
    return pl.pallas_call(
        functools.partial(_experts_body, has_prev),
        out_shape=jax.ShapeDtypeStruct((n_rows_total, D_MODEL), F32),
        grid_spec=pltpu.PrefetchScalarGridSpec(
            num_scalar_prefetch=2, grid=(nblk,),
            in_specs=in_specs,
            out_specs=pl.BlockSpec((MOE_ROWS, D_MODEL), lambda b, be, nu: (first_blk + b, 0)),
            scratch_shapes=[pltpu.VMEM((D_MODEL, D_MODEL), BF16)] * 3),
        input_output_aliases={len(args) - 1: 0} if has_prev else {},
        compiler_params=_cparams(1),
        name="experts",
    )(*args)


def _first_max(vals):
    best, idx = vals[0], jnp.zeros(vals[0].shape, jnp.int32)
    for j in range(1, len(vals)):
        upd = vals[j] > best
        idx = jnp.where(upd, j, idx)
        best = jnp.where(upd, vals[j], best)
    return best, idx


def _pick(idx, vals):
    out = vals[-1]
    for j in range(len(vals) - 2, -1, -1):
        out = jnp.where(idx == j, vals[j], out)
    return out


def _router_body(sc_ref, rb_ref, eid_ref, wts_ref, rank_ref, cnt_ref, base_ref):
    j = pl.program_id(0)

    @pl.when(j == 0)
    def _():
        base_ref[...] = jnp.zeros_like(base_ref)

    s = sc_ref[...]
    sel = s + rb_ref[...]
    srow = [s[e:e + 1, :] for e in range(N_EXPERTS)]
    brow = [sel[e:e + 1, :] for e in range(N_EXPERTS)]
    epg = EXPERTS_PER_GROUP
    gscore = []
    for g in range(N_GROUPS):
        x = brow[g * epg:(g + 1) * epg]
        pairs = [x[a] + x[b] for a in range(epg) for b in range(a + 1, epg)]
        gscore.append(functools.reduce(jnp.maximum, pairs))
    _, bestg = _first_max(gscore)
    y = [_pick(bestg, [brow[g * epg + i] for g in range(N_GROUPS)]) for i in range(epg)]
    sy = [_pick(bestg, [srow[g * epg + i] for g in range(N_GROUPS)]) for i in range(epg)]
    _, i1 = _first_max(y)
    v2 = jnp.full(y[0].shape, -jnp.inf, F32)
    i2 = jnp.zeros(y[0].shape, jnp.int32)
    for i in range(epg):
        upd = (i1 != i) & (y[i] > v2)
        i2 = jnp.where(upd, i, i2)
        v2 = jnp.where(upd, y[i], v2)
    e0 = bestg * epg + i1
    e1 = bestg * epg + i2
    w0 = _pick(i1, sy)
    w1 = _pick(i2, sy)
    wsum = w0 + w1
    eid_ref[0:1, :] = e0
    eid_ref[1:2, :] = e1
    wts_ref[0:1, :] = w0 / wsum
    wts_ref[1:2, :] = w1 / wsum

    eiota = lax.broadcasted_iota(jnp.int32, s.shape, 0)
    m0 = eiota == e0
    m1 = eiota == e1
    member = (m0 | m1).astype(BF16)
    tb = s.shape[1]
    before = (lax.broadcasted_iota(jnp.int32, (tb, tb), 0) < lax.broadcasted_iota(jnp.int32, (tb, tb), 1)).astype(BF16)
    pos = base_ref[...] + jnp.dot(member, before, preferred_element_type=F32)
    rank_ref[0:1, :] = jnp.sum(jnp.where(m0, pos, 0.0), axis=0, keepdims=True).astype(jnp.int32)
    rank_ref[1:2, :] = jnp.sum(jnp.where(m1, pos, 0.0), axis=0, keepdims=True).astype(jnp.int32)
    base = base_ref[...] + jnp.sum(member.astype(F32), axis=1, keepdims=True)
    base_ref[...] = base
    cnt_ref[...] = jnp.broadcast_to(base, cnt_ref.shape).astype(jnp.int32)


def _router(scores_t, router_b):
    n_tok = scores_t.shape[1]
    tok = lambda rows: pl.BlockSpec((rows, DENSE_BLOCK), lambda j: (0, j))
    return pl.pallas_call(
        _router_body,
        out_shape=[jax.ShapeDtypeStruct((TOP_K, n_tok), jnp.int32), jax.ShapeDtypeStruct((TOP_K, n_tok), F32),
                   jax.ShapeDtypeStruct((TOP_K, n_tok), jnp.int32), jax.ShapeDtypeStruct((N_EXPERTS, LANES), jnp.int32)],
        grid=(n_tok // DENSE_BLOCK,),
        in_specs=[tok(N_EXPERTS), pl.BlockSpec((N_EXPERTS, 1), lambda j: (0, 0))],
        out_specs=[tok(TOP_K), tok(TOP_K), tok(TOP_K), pl.BlockSpec((N_EXPERTS, LANES), lambda j: (0, 0))],
        scratch_shapes=[pltpu.VMEM((N_EXPERTS, 1), F32)],
        compiler_params=_cparams(1),
        name="router",
    )(scores_t, router_b.reshape(N_EXPERTS, 1))


def _dispatch_plan(eid, rank, counts):
    n_tok = eid.shape[1]
    n_slots = n_tok * TOP_K
    padded = (counts + MOE_ROWS - 1) // MOE_ROWS * MOE_ROWS
    pad_end = jnp.cumsum(padded)
    pad_start = pad_end - padded
    e_ids = jnp.arange(N_EXPERTS, dtype=jnp.int32)
    dest = jnp.sum(jnp.where(eid[..., None] == e_ids, pad_start, 0), axis=-1) + rank
    n_blocks = -(-n_slots // MOE_ROWS) + N_EXPERTS
    n_rows = n_blocks * MOE_ROWS
    blk_start = jnp.arange(n_blocks, dtype=jnp.int32) * MOE_ROWS
    blk_expert = jnp.minimum(jnp.sum(pad_end[None, :] <= blk_start[:, None], axis=-1), N_EXPERTS - 1).astype(jnp.int32)
    nused = (pad_end[-1:] // MOE_ROWS).astype(jnp.int32)
    n_fill = n_rows - n_slots
    gap = padded - counts
    gap_before = jnp.cumsum(gap) - gap
    j = jnp.arange(n_fill, dtype=jnp.int32)
    in_gap = (j[:, None] >= gap_before[None, :]) & (j[:, None] < (gap_before + gap)[None, :])
    fill_pos = jnp.where(jnp.any(in_gap, axis=-1),
                         jnp.sum(jnp.where(in_gap, (pad_start + counts - gap_before)[None, :], 0), axis=-1) + j,
                         pad_end[-1] - jnp.sum(gap) + j)
    keys = jnp.concatenate([dest.reshape(-1), fill_pos])
    vals = jnp.concatenate([jnp.tile(jnp.arange(n_tok, dtype=jnp.int32), TOP_K), j % n_tok])
    _, row_tok = lax.sort_key_val(keys, vals)
    return dest, blk_expert, nused, row_tok


def _final_body(n_ctx_blk, n_r, rowmap_ref, x_ref, *refs):
    r_refs, (gate_ref, nw_ref, oc_ref, ol_ref) = refs[:n_r], refs[n_r:]
    j = pl.program_id(0)
    row = rowmap_ref[j]
    xn = x_ref[...] + gate_ref[pl.ds(row, 1), :] * _branch_update(r_refs)
    ms = jnp.mean(xn * xn, axis=-1, keepdims=True)
    y = xn * lax.rsqrt(ms + RMS_EPS) * nw_ref[...]

    @pl.when(j < n_ctx_blk)
    def _():
        oc_ref[...] = y

    @pl.when(j >= n_ctx_blk)
    def _():
        ol_ref[...] = y


def _final(rowmap, x, r, gate_rows, nw, n_ctx_tok):
    n_tok = x.shape[0]
    n_ctx_blk = n_ctx_tok // DENSE_BLOCK
    tok = lambda w: pl.BlockSpec((DENSE_BLOCK, w), lambda j, rm: (j, 0))
    full = lambda shape: pl.BlockSpec(shape, lambda j, rm: (0,) * len(shape))
    return pl.pallas_call(
        functools.partial(_final_body, n_ctx_blk, len(r)),
        out_shape=[jax.ShapeDtypeStruct((n_ctx_tok, D_MODEL), F32), jax.ShapeDtypeStruct((n_tok - n_ctx_tok, D_MODEL), F32)],
        grid_spec=pltpu.PrefetchScalarGridSpec(
            num_scalar_prefetch=1, grid=(n_tok // DENSE_BLOCK,),
            in_specs=[tok(D_MODEL)] + _row_specs(r) + [full((SUBLANES, D_MODEL)), full((1, D_MODEL))],
            out_specs=[pl.BlockSpec((DENSE_BLOCK, D_MODEL), lambda j, rm: (jnp.minimum(j, n_ctx_blk - 1), 0)),
                       pl.BlockSpec((DENSE_BLOCK, D_MODEL), lambda j, rm: (jnp.maximum(j - n_ctx_blk, 0), 0))]),
        compiler_params=_cparams(1),
        name="final_norm",
    )(rowmap, x, *[a for a, _ in r], gate_rows, nw)


def _grid_pos_embed(rows):
    r = jnp.repeat(jnp.arange(rows, dtype=F32), GRID_W)
    col = jnp.tile(jnp.arange(GRID_W, dtype=F32), rows)
    quarter = D_MODEL // 4
    omega = 1.0 / (10000.0 ** (jnp.arange(quarter, dtype=F32) / quarter))

    def axis_embed(pos):
        ang = pos[:, None] * omega[None, :]
        return jnp.concatenate([jnp.sin(ang), jnp.cos(ang)], axis=-1)

    return jnp.concatenate([axis_embed(r), axis_embed(col)], axis=-1)


def kernel(x_prompt, x_sample, state_s5_re, state_s5_im, state_hg, c, c_ctx, w_ada, b_ada, norm1_w, norm2_w, w_in, s5_lam_re, s5_lam_im, s5_log_step, s5_b_re, s5_b_im, s5_c_re, s5_c_im, s5_d, s5_w_glu, s5_b_glu, hg_lb_logits, hg_norm_w, w_br_s5, w_br_hg, w_out, router_w, router_b, moe_w_gate, moe_w_up, moe_w_down, norm_f_w):
    n_ctx, ctx_len, _ = x_prompt.shape
    n_lat, lat_len, _ = x_sample.shape
    n_ctx_tok = n_ctx * ctx_len
    n_seq = n_ctx + n_lat
    assert n_lat + 1 <= SUBLANES and ctx_len % TOKEN_BLOCK == 0 and lat_len % DENSE_BLOCK == 0
    assert (n_ctx * ctx_len) % DENSE_BLOCK == 0

    seq_np, first_np, last_np, modrow_np = _block_tables(n_ctx, ctx_len, n_lat, lat_len)
    nblk = seq_np.shape[0]
    jmap_f = np.arange(nblk, dtype=np.int32)
    jmap_b = jmap_f[::-1].copy()
    tab_f = tuple(jnp.asarray(t) for t in (jmap_f, seq_np, first_np, last_np))
    tab_b = tuple(jnp.asarray(t) for t in (jmap_b, seq_np[::-1].copy(), last_np[::-1].copy(), first_np[::-1].copy()))
    tab_s5 = tuple(jnp.asarray(t) for t in (seq_np, first_np, last_np))
    rowmap = jnp.asarray(modrow_np.reshape(-1, DENSE_BLOCK // TOKEN_BLOCK)[:, 0])

    cvecs = jnp.zeros((SUBLANES, D_MODEL), F32).at[0].set(c_ctx).at[1:1 + n_lat].set(c)
    mod_all = _adaln(cvecs, w_ada, b_ada)

    probs = jax.nn.softmax(hg_lb_logits.astype(F32), axis=0)
    lower_bounds = jnp.cumsum(probs, axis=0) - probs[0]

    pos = _grid_pos_embed(lat_len // GRID_W)
    x = (x_prompt.reshape(n_ctx_tok, D_MODEL), x_sample.reshape(n_lat * lat_len, D_MODEL))
    n_ctx_blk = n_ctx_tok // DENSE_BLOCK
    n_dense_blk = (n_ctx_tok + n_lat * lat_len) // DENSE_BLOCK
    pos_blocks = lat_len // DENSE_BLOCK
    resid = ((pos, lambda j: jnp.maximum(j - n_ctx_blk, 0) % pos_blocks),)
    gate_rows = jnp.ones((SUBLANES, D_MODEL), F32).at[0].set(0.0)

    router_t = router_w.T.astype(F32)
    r_hi, r_lo = _split_bf16(router_t)
    ns = S5_BLOCK_STATES
    zeros_s5 = jnp.zeros((n_ctx, S5_COLBLOCKS, 2, ns), F32)
    zeros_hg = jnp.zeros((n_ctx, HG_HEADS, HG_DV, HG_DK), F32)

    new_s5_re, new_s5_im, new_hg = [], [], []
    for l in range(DEPTH):
        mod = mod_all[l]
        x, u, q_in, f_raw, i_in, g_out, gate_s5, gate_hg = _inproj(
            rowmap, x, resid, gate_rows, mod, norm1_w[l].reshape(1, D_MODEL), w_in[l].astype(BF16))

        par = [_s5_params(s5_lam_re[l, d], s5_lam_im[l, d], s5_log_step[l, d], s5_b_re[l], s5_b_im[l], s5_c_re[l], s5_c_im[l])
               for d in range(2)]
        h0_s5 = jnp.stack([
            jnp.concatenate([zeros_s5, jnp.stack([state_s5_re[:, l, d], state_s5_im[:, l, d]], axis=1).astype(F32).reshape(
                n_lat, 2, S5_COLBLOCKS, ns).swapaxes(1, 2)], axis=0) for d in range(2)])
        y_f, y_b, fin_f, fin_b = _s5_scan(tab_s5, u, jnp.stack([par[0][0], par[1][0]]), par[0][1],
                                          jnp.stack([par[0][2], par[1][2]]), h0_s5)
        ys = [y_f, y_b]
        s5_fin_re = [fin[:n_ctx, :, 0].reshape(n_ctx, S5_GROUPS, S5_STATE) for fin in (fin_f, fin_b)]
        s5_fin_im = [fin[:n_ctx, :, 1].reshape(n_ctx, S5_GROUPS, S5_STATE) for fin in (fin_f, fin_b)]

        os_, hg_fin = [], []
        for d, tabs in enumerate((tab_f, tab_b)):

            h0 = jnp.concatenate([zeros_hg, jnp.swapaxes(state_hg[:, l, d].astype(F32), -1, -2)], axis=0)
            o_d, hfin = _hg_scan(bool(d), tabs, q_in, f_raw, i_in, lower_bounds[l, d].reshape(1, HG_KW), h0)
            os_.append(o_d)
            hg_fin.append(jnp.swapaxes(hfin[:n_ctx], -1, -2))
        new_s5_re.append(jnp.stack(s5_fin_re, axis=1))
        new_s5_im.append(jnp.stack(s5_fin_im, axis=1))
        new_hg.append(jnp.stack(hg_fin, axis=1))

        x, h2, scores_t = _merge(
            rowmap, x, ys[0], ys[1], u, os_[0], os_[1], g_out, gate_s5, gate_hg, mod,
            s5_d[l].reshape(1, S5_WIDTH), s5_b_glu[l].reshape(1, S5_WIDTH), hg_norm_w[l].reshape(1, HG_WIDTH),
            norm2_w[l].reshape(1, D_MODEL), s5_w_glu[l].astype(BF16), w_br_s5[l].astype(BF16),
            w_br_hg[l].astype(BF16), w_out[l].astype(BF16), r_hi, r_lo)

        eid, wts, rank, counts = _router(scores_t, router_b.astype(F32))
        dest, blk_expert, nused, row_tok = _dispatch_plan(eid, rank, counts[:, 0])
        n_blocks = blk_expert.shape[0]
        cb = n_blocks // MOE_CHUNKS
        yb = None
        for ch in range(MOE_CHUNKS):
            xb = h2.at[row_tok[ch * cb * MOE_ROWS:(ch + 1) * cb * MOE_ROWS]].get(mode='promise_in_bounds')
            yb = _experts(l, ch * cb, n_blocks * MOE_ROWS, blk_expert[ch * cb:(ch + 1) * cb],
                          jnp.clip(nused - ch * cb, 0, cb), xb, moe_w_gate, moe_w_up, moe_w_down, yb)
        rows = yb.at[dest.reshape(-1)].get(mode='promise_in_bounds')
        resid = ((rows, _same_block), (rows, lambda j: j + n_dense_blk), (wts.T, _same_block))
        gate_rows = mod[:, 5 * D_MODEL:6 * D_MODEL]

    y_ctx, y_lat = _final(rowmap, x, resid, gate_rows, norm_f_w.reshape(1, D_MODEL), n_ctx_tok)
    y_prompt = y_ctx.reshape(n_ctx, ctx_len, D_MODEL)
    y_sample = y_lat.reshape(n_lat, lat_len, D_MODEL)
    new_state_s5_re = jnp.stack(new_s5_re, axis=1).astype(state_s5_re.dtype)
    new_state_s5_im = jnp.stack(new_s5_im, axis=1).astype(state_s5_im.dtype)
    new_state_hg = jnp.stack(new_hg, axis=1).astype(state_hg.dtype)
    return (y_prompt, y_sample, new_state_s5_re, new_state_s5_im, new_state_hg)
```

```python
import functools
import math

import jax
import jax.numpy as jnp
import numpy as np
from jax import lax
from jax.experimental import pallas as pl
from jax.experimental.pallas import tpu as pltpu

F32 = jnp.float32
BF16 = jnp.bfloat16

D_MODEL = 1024
DEPTH = 4
GRID_W = 64
RMS_EPS = 1e-6
S5_WIDTH = 512
S5_GROUP = 16
S5_GROUPS = 32
S5_STATE = 64
HG_HEADS = 4
HG_DK = 128
HG_DV = 128
HG_WIDTH = 512
HG_KW = 512
HG_CHUNK = 64
HG_SUB = 16
N_EXPERTS = 16
N_GROUPS = 4
EXPERTS_PER_GROUP = 4
TOP_K = 2
IN_COLS = 5120

LANES = 128
SUBLANES = 8
VMEM_LIMIT_BYTES = 56 * 1024 * 1024

TOKEN_BLOCK = 256
DENSE_BLOCK = 512
S5_SEG = TOKEN_BLOCK // SUBLANES
MXU_DEPTH = 256
S5_CHANNELS = MXU_DEPTH
S5_COLBLOCKS = S5_WIDTH // S5_CHANNELS
S5_BLOCK_STATES = (S5_CHANNELS // S5_GROUP) * S5_STATE
S5_PITCH = S5_SEG + 4
_S5_PERM = np.zeros((TOKEN_BLOCK, TOKEN_BLOCK), np.float32)
_S5_PERM[np.arange(TOKEN_BLOCK), (np.arange(TOKEN_BLOCK) % SUBLANES) * S5_SEG + np.arange(TOKEN_BLOCK) // SUBLANES] = 1.0
MOE_ROWS = 256
FF_TILE = 512
MOE_CHUNKS = 4


def _block_tables(n_ctx_seq, ctx_len, n_lat_seq, lat_len):
    seq, first, last, modrow = [], [], [], []
    for s in range(n_ctx_seq):
        nb = ctx_len // TOKEN_BLOCK
        for b in range(nb):
            seq.append(s); first.append(int(b == 0)); last.append(int(b == nb - 1)); modrow.append(0)
    for s in range(n_lat_seq):
        nb = lat_len // TOKEN_BLOCK
        for b in range(nb):
            seq.append(n_ctx_seq + s); first.append(int(b == 0)); last.append(int(b == nb - 1))
            modrow.append(1 + s)
    return (np.asarray(seq, np.int32), np.asarray(first, np.int32),
            np.asarray(last, np.int32), np.asarray(modrow, np.int32))


def _cparams(n_axes):
    return pltpu.CompilerParams(dimension_semantics=("arbitrary",) * n_axes,
                                vmem_limit_bytes=VMEM_LIMIT_BYTES)


def _split_bf16(a):
    hi = a.astype(BF16)
    lo = (a - hi.astype(F32)).astype(BF16)
    return hi, lo


def _dot_nt(a, b):
    return lax.dot_general(a, b, (((1,), (1,)), ((), ())), preferred_element_type=F32)


def _adaln_body(c_ref, w_ref, b_ref, o_ref):
    a = c_ref[...]
    a = a * jax.nn.sigmoid(a)
    a_hi, a_lo = _split_bf16(a)
    w_hi, w_lo = _split_bf16(w_ref[0])
    acc = jnp.dot(a_hi, w_hi, preferred_element_type=F32)
    acc += jnp.dot(a_lo, w_hi, preferred_element_type=F32)
    acc += jnp.dot(a_hi, w_lo, preferred_element_type=F32)
    o_ref[0] = acc + b_ref[0]


def _adaln(cvecs, w_ada, b_ada):
    tn = 1536
    n_tiles = (6 * D_MODEL) // tn
    return pl.pallas_call(
        _adaln_body,
        out_shape=jax.ShapeDtypeStruct((DEPTH, SUBLANES, 6 * D_MODEL), F32),
        grid=(DEPTH, n_tiles),
        in_specs=[
            pl.BlockSpec((SUBLANES, D_MODEL), lambda l, n: (0, 0)),
            pl.BlockSpec((1, D_MODEL, tn), lambda l, n: (l, 0, n)),
            pl.BlockSpec((1, 1, tn), lambda l, n: (l, 0, n)),
        ],
        out_specs=pl.BlockSpec((1, SUBLANES, tn), lambda l, n: (l, 0, n)),
        compiler_params=_cparams(2),
        name="adaln",
    )(cvecs, w_ada, b_ada.reshape(DEPTH, 1, 6 * D_MODEL))


_IN_SPLITS = ((0, 512), (512, 1024), (1024, 2048), (2048, 2560), (2560, 3072), (3072, 4096), (4096, 5120))


def _rms_modulate(x, nw, scale, shift):
    ms = jnp.mean(x * x, axis=-1, keepdims=True)
    return (x * lax.rsqrt(ms + RMS_EPS)) * nw * (1.0 + scale) + shift


def _branch_update(r_refs):
    if len(r_refs) == 1:
        return r_refs[0][...]
    ya_ref, yb_ref, w_ref = r_refs
    w = w_ref[...]
    return ya_ref[...] * w[:, 0:1] + yb_ref[...] * w[:, 1:2]


def _inproj_body(n_x, n_ctx_blk, n_r, rowmap_ref, *refs):
    x_refs, r_refs = refs[:n_x], refs[n_x:n_x + n_r]
    gate_ref, mod_ref, nw_ref, w_ref, xo_ref, u_ref, q_ref, f_ref, i_ref, g_ref, gs_ref, gh_ref = refs[n_x + n_r:]
    j = pl.program_id(0)
    row = rowmap_ref[j]
    gate = gate_ref[pl.ds(row, 1), :]
    x = x_refs[0][...] if n_x == 1 else jnp.where(j < n_ctx_blk, x_refs[0][...], x_refs[1][...])
    xn = x + gate * _branch_update(r_refs)
    xo_ref[...] = xn
    shift = mod_ref[pl.ds(row, 1), 0:D_MODEL]
    scale = mod_ref[pl.ds(row, 1), D_MODEL:2 * D_MODEL]
    h = _rms_modulate(xn, nw_ref[...], scale, shift).astype(BF16)
    outs = (u_ref, q_ref, f_ref, i_ref, g_ref, gs_ref, gh_ref)
    for ref, (a, b) in zip(outs, _IN_SPLITS):
        ref[...] = jnp.dot(h, w_ref[:, a:b], preferred_element_type=F32).astype(ref.dtype)


def _row_specs(arrays_and_maps):
    return [pl.BlockSpec((DENSE_BLOCK, a.shape[1]), lambda j, rm, f=f: (f(j), 0)) for a, f in arrays_and_maps]


def _same_block(j):
    return j


def _inproj(rowmap, x, r, gate_rows, mod, nw, w_bf16):
    xs = x if isinstance(x, tuple) else (x,)
    n_tok = sum(a.shape[0] for a in xs)
    nblk = n_tok // DENSE_BLOCK
    n_ctx_blk = xs[0].shape[0] // DENSE_BLOCK
    x_maps = [_same_block] if len(xs) == 1 else [lambda j: jnp.minimum(j, n_ctx_blk - 1), lambda j: jnp.maximum(j - n_ctx_blk, 0)]
    tok = lambda w: pl.BlockSpec((DENSE_BLOCK, w), lambda j, rm: (j, 0))
    full = lambda shape: pl.BlockSpec(shape, lambda j, rm: (0,) * len(shape))
    widths = [b - a for a, b in _IN_SPLITS]
    dtypes = [F32, BF16, F32, BF16, BF16, BF16, BF16]
    return pl.pallas_call(
        functools.partial(_inproj_body, len(xs), n_ctx_blk, len(r)),
        out_shape=[jax.ShapeDtypeStruct((n_tok, D_MODEL), F32)]
        + [jax.ShapeDtypeStruct((n_tok, w), dt) for w, dt in zip(widths, dtypes)],
        grid_spec=pltpu.PrefetchScalarGridSpec(
            num_scalar_prefetch=1, grid=(nblk,),
            in_specs=_row_specs(list(zip(xs, x_maps))) + _row_specs(r)
            + [full((SUBLANES, D_MODEL)), full((SUBLANES, 6 * D_MODEL)), full((1, D_MODEL)), full((D_MODEL, IN_COLS))],
            out_specs=[tok(D_MODEL)] + [tok(w) for w in widths]),
        compiler_params=_cparams(1),
        name="inproj",
    )(rowmap, *xs, *[a for a, _ in r], gate_rows, mod, nw, w_bf16)


def _s5_direction(rev, u_ref, perm_ref, bd_ref, cd_ref, lam_ref,
                  y_ref, fin_ref, bu_ref, sb_ref, yp_ref, carry_ref):
    ns = S5_BLOCK_STATES

    up = jnp.dot(perm_ref[...], u_ref[...].astype(BF16), preferred_element_type=F32).astype(BF16)
    bu_ref[...] = jnp.dot(up, bd_ref[0, 0], preferred_element_type=F32)

    lre = lam_ref[0, 0, 0:SUBLANES, :]
    lim = lam_ref[0, 0, SUBLANES:2 * SUBLANES, :]
    order = list(range(S5_SEG))
    if rev:
        order = order[::-1]

    def step(sr, si, k):
        br = bu_ref[pl.ds(SUBLANES * k, SUBLANES), 0:ns]
        bi = bu_ref[pl.ds(SUBLANES * k, SUBLANES), ns:2 * ns]
        return lre * sr - lim * si + br, lre * si + lim * sr + bi

    sr = jnp.zeros((SUBLANES, ns), F32)
    si = jnp.zeros((SUBLANES, ns), F32)
    for k in order:
        sr, si = step(sr, si, k)

    cr = carry_ref[0:1, :]
    ci = carry_ref[1:2, :]
    lkre = lam_ref[0, 0, 2 * SUBLANES:2 * SUBLANES + 1, :]
    lkim = lam_ref[0, 0, 2 * SUBLANES + 1:2 * SUBLANES + 2, :]
    start_r = [None] * SUBLANES
    start_i = [None] * SUBLANES
    seg_order = list(range(SUBLANES))
    if rev:
        seg_order = seg_order[::-1]
    for i in seg_order:
        start_r[i] = cr
        start_i[i] = ci
        er = sr[i:i + 1]
        ei = si[i:i + 1]
        cr, ci = lkre * cr - lkim * ci + er, lkre * ci + lkim * cr + ei
    carry_ref[0:1, :] = cr
    carry_ref[1:2, :] = ci
    fin_ref[0, 0, 0:1, :] = cr
    fin_ref[0, 0, 1:2, :] = ci

    sr = jnp.concatenate(start_r, axis=0)
    si = jnp.concatenate(start_i, axis=0)
    for n in range(0, S5_SEG, 2):
        k0, k1 = order[n], order[n + 1]
        sr0, si0 = step(sr, si, k0)
        sr, si = step(sr0, si0, k1)
        lo_r, hi_r = (sr0, sr) if k0 < k1 else (sr, sr0)
        lo_i, hi_i = (si0, si) if k0 < k1 else (si, si0)
        base = SUBLANES * min(k0, k1)
        sb_ref[pl.ds(base, 2 * SUBLANES), 0:ns] = jnp.concatenate([lo_r, hi_r], axis=0).astype(BF16)
        sb_ref[pl.ds(base, 2 * SUBLANES), ns:2 * ns] = jnp.concatenate([lo_i, hi_i], axis=0).astype(BF16)

    y = jnp.dot(sb_ref[...], cd_ref[0], preferred_element_type=F32)
    for h in range(S5_CHANNELS // LANES):
        for k in range(S5_SEG):
            yp_ref[h, pl.ds(k, SUBLANES, stride=S5_PITCH), :] = y[SUBLANES * k:SUBLANES * (k + 1), h * LANES:(h + 1) * LANES]
        for i in range(SUBLANES):
            y_ref[i * S5_SEG:(i + 1) * S5_SEG, h * LANES:(h + 1) * LANES] = yp_ref[h, pl.ds(i * S5_PITCH, S5_SEG), :]


def _s5_body(seq_ref, first_ref, last_ref,
             uf_ref, ub_ref, perm_ref, bdf_ref, bdb_ref, cd_ref, lamf_ref, lamb_ref, h0f_ref, h0b_ref,
             yf_ref, yb_ref, finf_ref, finb_ref,
             buf_ref, bub_ref, sbf_ref, sbb_ref, ypf_ref, ypb_ref, carryf_ref, carryb_ref):
    p = pl.program_id(1)
    q = pl.num_programs(1) - 1 - p

    @pl.when(first_ref[p] == 1)
    def _():
        carryf_ref[...] = h0f_ref[0, 0, 0]

    @pl.when(last_ref[q] == 1)
    def _():
        carryb_ref[...] = h0b_ref[0, 0, 0]

    _s5_direction(False, uf_ref, perm_ref, bdf_ref, cd_ref, lamf_ref,
                  yf_ref, finf_ref, buf_ref, sbf_ref, ypf_ref, carryf_ref)
    _s5_direction(True, ub_ref, perm_ref, bdb_ref, cd_ref, lamb_ref,
                  yb_ref, finb_ref, bub_ref, sbb_ref, ypb_ref, carryb_ref)


def _s5_scan(tables, u, bd, cd, lam, h0):
    seq, first, last = tables
    n_tok = u.shape[0]
    nblk = n_tok // TOKEN_BLOCK
    n_seq = h0.shape[1]
    ns = S5_BLOCK_STATES
    lam_rows = lam.shape[2]
    grid = (S5_COLBLOCKS, nblk)
    fwd = lambda a, p, sq, fi, la: p
    bwd = lambda a, p, sq, fi, la: nblk - 1 - p
    tok = lambda blk: pl.BlockSpec((TOKEN_BLOCK, S5_CHANNELS), lambda a, p, sq, fi, la: (blk(a, p, sq, fi, la), a))
    par = lambda d, shape: pl.BlockSpec((1, 1) + shape, lambda a, p, sq, fi, la: (d, a, 0, 0))
    st = lambda d, blk: pl.BlockSpec((1, 1, 1, 2, ns), lambda a, p, sq, fi, la: (d, sq[blk(a, p, sq, fi, la)], a, 0, 0))
    fin = lambda blk: pl.BlockSpec((1, 1, 2, ns), lambda a, p, sq, fi, la: (sq[blk(a, p, sq, fi, la)], a, 0, 0))
    fin_shape = jax.ShapeDtypeStruct((n_seq, S5_COLBLOCKS, 2, ns), F32)
    return pl.pallas_call(
        _s5_body,
        out_shape=[jax.ShapeDtypeStruct((n_tok, S5_WIDTH), F32), jax.ShapeDtypeStruct((n_tok, S5_WIDTH), F32),
                   fin_shape, fin_shape],
        grid_spec=pltpu.PrefetchScalarGridSpec(
            num_scalar_prefetch=3, grid=grid,
            in_specs=[
                tok(fwd), tok(bwd),
                pl.BlockSpec((TOKEN_BLOCK, TOKEN_BLOCK), lambda a, p, sq, fi, la: (0, 0)),
                par(0, (S5_CHANNELS, 2 * ns)), par(1, (S5_CHANNELS, 2 * ns)),
                pl.BlockSpec((1, 2 * ns, S5_CHANNELS), lambda a, p, sq, fi, la: (a, 0, 0)),
                par(0, (lam_rows, ns)), par(1, (lam_rows, ns)),
                st(0, fwd), st(1, bwd)],
            out_specs=[tok(fwd), tok(bwd), fin(fwd), fin(bwd)],
            scratch_shapes=[pltpu.VMEM((TOKEN_BLOCK, 2 * ns), F32)] * 2
            + [pltpu.VMEM((TOKEN_BLOCK, 2 * ns), BF16)] * 2
            + [pltpu.VMEM((S5_CHANNELS // LANES, SUBLANES * S5_PITCH, LANES), F32)] * 2
            + [pltpu.VMEM((2, ns), F32)] * 2),
        compiler_params=_cparams(2),
        name="s5_scan",
    )(seq, first, last, u, u, jnp.asarray(_S5_PERM, BF16), bd, bd, cd, lam, lam, h0, h0)


def _s5_params(lam_re, lam_im, log_step, b_re, b_im, c_re, c_im):
    g, p, c = S5_GROUPS, S5_STATE, S5_GROUP
    gl = S5_CHANNELS // c
    lr = jnp.minimum(lam_re, -1e-4)
    li = lam_im
    step = jnp.exp(log_step)[:, None]
    mag = jnp.exp(lr * step)
    ar, ai = mag * jnp.cos(li * step), mag * jnp.sin(li * step)
    magk = jnp.exp(lr * step * S5_SEG)
    akr, aki = magk * jnp.cos(li * step * S5_SEG), magk * jnp.sin(li * step * S5_SEG)
    den = lr * lr + li * li
    cr = ((ar - 1.0) * lr + ai * li) / den
    ci = (ai * lr - (ar - 1.0) * li) / den
    bbr = cr[:, :, None] * b_re - ci[:, :, None] * b_im
    bbi = cr[:, :, None] * b_im + ci[:, :, None] * b_re
    eye = jnp.eye(gl, dtype=F32)

    def bmat(bb):
        bb = bb.reshape(S5_COLBLOCKS, gl, p, c)
        return jnp.einsum('agpc,gh->agchp', bb, eye).reshape(S5_COLBLOCKS, gl * c, gl * p)

    def cmat(cc):
        cc = cc.reshape(S5_COLBLOCKS, gl, c, p)
        return jnp.einsum('agcp,gh->agphc', cc, eye).reshape(S5_COLBLOCKS, gl * p, gl * c)

    bd = jnp.concatenate([bmat(bbr), bmat(bbi)], axis=2).astype(BF16)
    cd = jnp.concatenate([cmat(c_re), -cmat(c_im)], axis=1).astype(BF16)

    def lanes(v, rows):
        v = v.reshape(S5_COLBLOCKS, 1, gl * p)
        return jnp.broadcast_to(v, (S5_COLBLOCKS, rows, gl * p))

    lam = jnp.concatenate([lanes(ar, SUBLANES), lanes(ai, SUBLANES), lanes(akr, 1), lanes(aki, 1)], axis=1)
    return bd, cd, lam


def _hg_body(rev, jmap_ref, seq_ref, first_ref, last_ref,
             q_ref, f_ref, v_ref, lb_ref, h0_ref, o_ref, s_ref):
    p = pl.program_id(0)
    c_len = HG_CHUNK

    @pl.when(first_ref[p] == 1)
    def _():
        s_ref[...] = h0_ref[...]

    sub = HG_SUB
    n_sub = c_len // sub
    ri = lax.broadcasted_iota(jnp.int32, (c_len, c_len), 0)
    ci = lax.broadcasted_iota(jnp.int32, (c_len, c_len), 1)
    mask = (ri <= ci) if rev else (ri >= ci)
    tri = mask.astype(BF16)
    rb, cb = ri // sub, ci // sub
    diag_mask = mask & (rb == cb)
    off_mask = (cb > rb) if rev else (cb < rb)
    lb = lb_ref[...]
    tot_row = 0 if rev else c_len - 1
    n_chunks = TOKEN_BLOCK // c_len
    chunk_order = range(n_chunks - 1, -1, -1) if rev else range(n_chunks)
    for c in chunk_order:
        rows = pl.ds(c * c_len, c_len)
        qin = q_ref[rows, :].astype(F32)
        q = qin * jax.nn.sigmoid(qin) * (HG_DK ** -0.5)
        f = lb + (1.0 - lb) * jax.nn.sigmoid(f_ref[rows, :])
        k = 1.0 - f
        g = jnp.log(f)
        g0 = g.astype(BF16)
        r1 = g - g0.astype(F32)
        g1 = r1.astype(BF16)
        g2 = (r1 - g1.astype(F32)).astype(BF16)
        b = (jnp.dot(tri, g0, preferred_element_type=F32) + jnp.dot(tri, g1, preferred_element_type=F32)
             + jnp.dot(tri, g2, preferred_element_type=F32))
        btot = b[tot_row:tot_row + 1, :]
        bmid = jnp.concatenate(
            [jnp.broadcast_to(b[i * sub + sub // 2:i * sub + sub // 2 + 1, :], (sub, HG_KW)) for i in range(n_sub)], axis=0)
        qd = (q * jnp.exp(b - bmid)).astype(BF16)
        kd = (k * jnp.exp(bmid - b)).astype(BF16)
        qo, ko = {}, {}
        for i in range(n_sub):
            edge = (i + 1) * sub if rev else i * sub - 1
            if 0 <= edge < c_len:
                bref = b[edge:edge + 1, :]
                rws = slice(i * sub, (i + 1) * sub)
                qo[i] = (q[rws] * jnp.exp(b[rws] - bref)).astype(BF16)
                ko[i] = (k * jnp.exp(jnp.minimum(bref - b, 0.0))).astype(BF16)
        qs = (q * jnp.exp(b)).astype(BF16)
        ks = (k * jnp.exp(btot - b)).astype(BF16)
        dec = jnp.exp(btot)
        v = v_ref[rows, :]
        vt = v.astype(F32).T.astype(BF16)
        outs = []
        for h in range(HG_HEADS):
            sl = slice(h * HG_DK, (h + 1) * HG_DK)
            near = _dot_nt(qd[:, sl], kd[:, sl])
            far = jnp.concatenate(
                [_dot_nt(qo[i][:, sl], ko[i][:, sl]) if i in qo else jnp.zeros((sub, c_len), F32)
                 for i in range(n_sub)], axis=0)
            att = jnp.where(diag_mask, near, jnp.where(off_mask, far, 0.0)).astype(BF16)
            s_old = s_ref[0, h]
            o_h = jnp.dot(att, v[:, sl], preferred_element_type=F32) + _dot_nt(qs[:, sl], s_old.astype(BF16))
            s_ref[0, h] = s_old * dec[:, sl] + jnp.dot(vt[sl, :], ks[:, sl], preferred_element_type=F32)
            outs.append(o_h)
        o_ref[rows, :] = jnp.concatenate(outs, axis=1)


def _hg_scan(rev, tables, q, f, v, lb, h0):
    jmap, seq, first, last = tables
    n_tok = q.shape[0]
    nblk = n_tok // TOKEN_BLOCK
    n_seq = h0.shape[0]
    d = 1 if rev else 0
    tok = pl.BlockSpec((TOKEN_BLOCK, HG_WIDTH), lambda p, jm, sq, fi, la: (jm[p], 0))
    st = pl.BlockSpec((1, HG_HEADS, HG_DV, HG_DK), lambda p, jm, sq, fi, la: (sq[p], 0, 0, 0))
    return pl.pallas_call(
        functools.partial(_hg_body, rev),
        out_shape=[jax.ShapeDtypeStruct((n_tok, HG_WIDTH), F32),
                   jax.ShapeDtypeStruct((n_seq, HG_HEADS, HG_DV, HG_DK), F32)],
        grid_spec=pltpu.PrefetchScalarGridSpec(
            num_scalar_prefetch=4, grid=(nblk,),
            in_specs=[tok,
                      pl.BlockSpec((TOKEN_BLOCK, HG_KW), lambda p, jm, sq, fi, la: (jm[p], d)),
                      tok,
                      pl.BlockSpec((1, HG_KW), lambda p, jm, sq, fi, la: (0, 0)),
                      st],
            out_specs=[tok, st]),
        compiler_params=_cparams(1),
        name="hg_bwd" if rev else "hg_fwd",
    )(jmap, seq, first, last, q, f, v, lb, h0)


def _merge_body(rowmap_ref, x_ref, ysf_ref, ysb_ref, u_ref, ohf_ref, ohb_ref, g_ref, gs_ref, gh_ref,
                mod_ref, d_ref, bglu_ref, hnw_ref, n2w_ref, wglu_ref, wbs_ref, wbh_ref, wout_ref,
                rhi_ref, rlo_ref, xm_ref, h2_ref, sc_ref):
    row = rowmap_ref[pl.program_id(0)]
    mrow = lambda n: mod_ref[pl.ds(row, 1), n * D_MODEL:(n + 1) * D_MODEL]
    gate1, shift2, scale2 = mrow(2), mrow(3), mrow(4)

    y = ysf_ref[...] + ysb_ref[...] + d_ref[...] * u_ref[...]
    gl = jax.nn.gelu(y)
    z = jnp.dot(gl.astype(BF16), wglu_ref[...], preferred_element_type=F32) + bglu_ref[...]
    y_s5 = gl * jax.nn.sigmoid(z)

    o = ohf_ref[...] + ohb_ref[...]
    normed = []
    for h in range(HG_HEADS):
        oh = o[:, h * HG_DV:(h + 1) * HG_DV]
        normed.append(oh * lax.rsqrt(jnp.mean(oh * oh, axis=-1, keepdims=True) + RMS_EPS))
    gout = g_ref[...].astype(F32)
    y_hg = jnp.concatenate(normed, axis=1) * hnw_ref[...] * (gout * jax.nn.sigmoid(gout))

    merged = (jax.nn.sigmoid(gs_ref[...].astype(F32)) * jnp.dot(y_s5.astype(BF16), wbs_ref[...], preferred_element_type=F32)
              + jax.nn.sigmoid(gh_ref[...].astype(F32)) * jnp.dot(y_hg.astype(BF16), wbh_ref[...], preferred_element_type=F32))
    xm = x_ref[...] + gate1 * jnp.dot(merged.astype(BF16), wout_ref[...], preferred_element_type=F32)
    xm_ref[...] = xm
    h2 = _rms_modulate(xm, n2w_ref[...], scale2, shift2)
    h2_ref[...] = h2
    h_hi, h_lo = _split_bf16(h2)
    logits = _dot_nt(rhi_ref[...], h_hi) + _dot_nt(rhi_ref[...], h_lo) + _dot_nt(rlo_ref[...], h_hi)
    sc_ref[...] = jax.nn.sigmoid(logits)


def _merge(rowmap, x, ysf, ysb, u, ohf, ohb, g, gs, gh, mod, d, bglu, hnw, n2w,
           wglu, wbs, wbh, wout, rhi, rlo):
    n_tok = x.shape[0]
    nblk = n_tok // DENSE_BLOCK
    tok = lambda w: pl.BlockSpec((DENSE_BLOCK, w), lambda j, rm: (j, 0))
    full = lambda shape: pl.BlockSpec(shape, lambda j, rm: (0,) * len(shape))
    return pl.pallas_call(
        _merge_body,
        out_shape=[jax.ShapeDtypeStruct((n_tok, D_MODEL), F32),
                   jax.ShapeDtypeStruct((n_tok, D_MODEL), F32),
                   jax.ShapeDtypeStruct((N_EXPERTS, n_tok), F32)],
        grid_spec=pltpu.PrefetchScalarGridSpec(
            num_scalar_prefetch=1, grid=(nblk,),
            in_specs=[tok(D_MODEL), tok(S5_WIDTH), tok(S5_WIDTH), tok(S5_WIDTH), tok(HG_WIDTH), tok(HG_WIDTH),
                      tok(HG_WIDTH), tok(D_MODEL), tok(D_MODEL),
                      full((SUBLANES, 6 * D_MODEL)), full((1, S5_WIDTH)), full((1, S5_WIDTH)),
                      full((1, HG_WIDTH)), full((1, D_MODEL)),
                      full((S5_WIDTH, S5_WIDTH)), full((S5_WIDTH, D_MODEL)), full((HG_WIDTH, D_MODEL)),
                      full((D_MODEL, D_MODEL)), full((N_EXPERTS, D_MODEL)), full((N_EXPERTS, D_MODEL))],
            out_specs=[tok(D_MODEL), tok(D_MODEL),
                       pl.BlockSpec((N_EXPERTS, DENSE_BLOCK), lambda j, rm: (0, j))]),
        compiler_params=_cparams(1),
        name="merge",
    )(rowmap, x, ysf, ysb, u, ohf, ohb, g, gs, gh, mod, d, bglu, hnw, n2w, wglu, wbs, wbh, wout, rhi, rlo)


def _experts_body(has_prev, be_ref, nused_ref, x_ref, wg_ref, wu_ref, wd_ref, *rest):
    y_ref, wgb_ref, wub_ref, wdb_ref = rest[1:] if has_prev else rest
    b = pl.program_id(0)
    e = be_ref[b]
    prev = be_ref[jnp.maximum(b - 1, 0)]

    @pl.when((b == 0) | (e != prev))
    def _():
        wgb_ref[...] = wg_ref[0, 0].astype(BF16)
        wub_ref[...] = wu_ref[0, 0].astype(BF16)
        wdb_ref[...] = wd_ref[0, 0].astype(BF16)

    @pl.when(b < nused_ref[0])
    def _():
        x = x_ref[...].astype(BF16)
        for n in range(0, D_MODEL, FF_TILE):
            gate = jnp.dot(x, wgb_ref[:, n:n + FF_TILE], preferred_element_type=F32)
            up = jnp.dot(x, wub_ref[:, n:n + FF_TILE], preferred_element_type=F32)
            hid = (gate * jax.nn.sigmoid(gate) * up).astype(BF16)
            part = jnp.dot(hid, wdb_ref[n:n + FF_TILE, :], preferred_element_type=F32)
            if n == 0:
                y_ref[...] = part
            else:
                y_ref[...] += part

    @pl.when(b >= nused_ref[0])
    def _():
        y_ref[...] = jnp.zeros_like(y_ref)


def _experts(layer, first_blk, n_rows_total, blk_expert, nused, xb, w_gate, w_up, w_down, y_prev):
    nblk = xb.shape[0] // MOE_ROWS
    wspec = pl.BlockSpec((1, 1, D_MODEL, D_MODEL), lambda b, be, nu: (layer, be[b], 0, 0))
    has_prev = y_prev is not None
    in_specs = [pl.BlockSpec((MOE_ROWS, D_MODEL), lambda b, be, nu: (b, 0)), wspec, wspec, wspec]
    args = [blk_expert, nused, xb, w_gate, w_up, w_down]
    if has_prev:
        in_specs.append(pl.BlockSpec(memory_space=pl.ANY))
        args.append(y_prev)
    return pl.pallas_call(
        functools.partial(_experts_body, has_prev),
        out_shape=jax.ShapeDtypeStruct((n_rows_total, D_MODEL), F32),
        grid_spec=pltpu.PrefetchScalarGridSpec(
            num_scalar_prefetch=2, grid=(nblk,),
            in_specs=in_specs,
            out_specs=pl.BlockSpec((MOE_ROWS, D_MODEL), lambda b, be, nu: (first_blk + b, 0)),
            scratch_shapes=[pltpu.VMEM((D_MODEL, D_MODEL), BF16)] * 3),
        input_output_aliases={len(args) - 1: 0} if has_prev else {},
        compiler_params=_cparams(1),
        name="experts",
    )(*args)


def _first_max(vals):
    best, idx = vals[0], jnp.zeros(vals[0].shape, jnp.int32)
    for j in range(1, len(vals)):
        upd = vals[j] > best
        idx = jnp.where(upd, j, idx)
        best = jnp.where(upd, vals[j], best)
    return best, idx


def _pick(idx, vals):
    out = vals[-1]
    for j in range(len(vals) - 2, -1, -1):
        out = jnp.where(idx == j, vals[j], out)
    return out


def _router_body(sc_ref, rb_ref, eid_ref, wts_ref, rank_ref, cnt_ref, base_ref):
    j = pl.program_id(0)

    @pl.when(j == 0)
    def _():
        base_ref[...] = jnp.zeros_like(base_ref)

    s = sc_ref[...]
    sel = s + rb_ref[...]
    srow = [s[e:e + 1, :] for e in range(N_EXPERTS)]
    brow = [sel[e:e + 1, :] for e in range(N_EXPERTS)]
    epg = EXPERTS_PER_GROUP
    gscore = []
    for g in range(N_GROUPS):
        x = brow[g * epg:(g + 1) * epg]
        pairs = [x[a] + x[b] for a in range(epg) for b in range(a + 1, epg)]
        gscore.append(functools.reduce(jnp.maximum, pairs))
    _, bestg = _first_max(gscore)
    y = [_pick(bestg, [brow[g * epg + i] for g in range(N_GROUPS)]) for i in range(epg)]
    sy = [_pick(bestg, [srow[g * epg + i] for g in range(N_GROUPS)]) for i in range(epg)]
    _, i1 = _first_max(y)
    v2 = jnp.full(y[0].shape, -jnp.inf, F32)
    i2 = jnp.zeros(y[0].shape, jnp.int32)
    for i in range(epg):
        upd = (i1 != i) & (y[i] > v2)
        i2 = jnp.where(upd, i, i2)
        v2 = jnp.where(upd, y[i], v2)
    e0 = bestg * epg + i1
    e1 = bestg * epg + i2
    w0 = _pick(i1, sy)
    w1 = _pick(i2, sy)
    wsum = w0 + w1
    eid_ref[0:1, :] = e0
    eid_ref[1:2, :] = e1
    wts_ref[0:1, :] = w0 / wsum
    wts_ref[1:2, :] = w1 / wsum

    eiota = lax.broadcasted_iota(jnp.int32, s.shape, 0)
    m0 = eiota == e0
    m1 = eiota == e1
    member = (m0 | m1).astype(BF16)
    tb = s.shape[1]
    before = (lax.broadcasted_iota(jnp.int32, (tb, tb), 0) < lax.broadcasted_iota(jnp.int32, (tb, tb), 1)).astype(BF16)
    pos = base_ref[...] + jnp.dot(member, before, preferred_element_type=F32)
    rank_ref[0:1, :] = jnp.sum(jnp.where(m0, pos, 0.0), axis=0, keepdims=True).astype(jnp.int32)
    rank_ref[1:2, :] = jnp.sum(jnp.where(m1, pos, 0.0), axis=0, keepdims=True).astype(jnp.int32)
    base = base_ref[...] + jnp.sum(member.astype(F32), axis=1, keepdims=True)
    base_ref[...] = base
    cnt_ref[...] = jnp.broadcast_to(base, cnt_ref.shape).astype(jnp.int32)


def _router(scores_t, router_b):
    n_tok = scores_t.shape[1]
    tok = lambda rows: pl.BlockSpec((rows, DENSE_BLOCK), lambda j: (0, j))
    return pl.pallas_call(
        _router_body,
        out_shape=[jax.ShapeDtypeStruct((TOP_K, n_tok), jnp.int32), jax.ShapeDtypeStruct((TOP_K, n_tok), F32),
                   jax.ShapeDtypeStruct((TOP_K, n_tok), jnp.int32), jax.ShapeDtypeStruct((N_EXPERTS, LANES), jnp.int32)],
        grid=(n_tok // DENSE_BLOCK,),
        in_specs=[tok(N_EXPERTS), pl.BlockSpec((N_EXPERTS, 1), lambda j: (0, 0))],
        out_specs=[tok(TOP_K), tok(TOP_K), tok(TOP_K), pl.BlockSpec((N_EXPERTS, LANES), lambda j: (0, 0))],
        scratch_shapes=[pltpu.VMEM((N_EXPERTS, 1), F32)],
        compiler_params=_cparams(1),
        name="router",
    )(scores_t, router_b.reshape(N_EXPERTS, 1))


def _dispatch_plan(eid, rank, counts):
    n_tok = eid.shape[1]
    n_slots = n_tok * TOP_K
    padded = (counts + MOE_ROWS - 1) // MOE_ROWS * MOE_ROWS
    pad_end = jnp.cumsum(padded)
    pad_start = pad_end - padded
    e_ids = jnp.arange(N_EXPERTS, dtype=jnp.int32)
    dest = jnp.sum(jnp.where(eid[..., None] == e_ids, pad_start, 0), axis=-1) + rank
    n_blocks = -(-n_slots // MOE_ROWS) + N_EXPERTS
    n_rows = n_blocks * MOE_ROWS
    blk_start = jnp.arange(n_blocks, dtype=jnp.int32) * MOE_ROWS
    blk_expert = jnp.minimum(jnp.sum(pad_end[None, :] <= blk_start[:, None], axis=-1), N_EXPERTS - 1).astype(jnp.int32)
    nused = (pad_end[-1:] // MOE_ROWS).astype(jnp.int32)
    n_fill = n_rows - n_slots
    gap = padded - counts
    gap_before = jnp.cumsum(gap) - gap
    j = jnp.arange(n_fill, dtype=jnp.int32)
    in_gap = (j[:, None] >= gap_before[None, :]) & (j[:, None] < (gap_before + gap)[None, :])
    fill_pos = jnp.where(jnp.any(in_gap, axis=-1),
                         jnp.sum(jnp.where(in_gap, (pad_start + counts - gap_before)[None, :], 0), axis=-1) + j,
                         pad_end[-1] - jnp.sum(gap) + j)
    keys = jnp.concatenate([dest.reshape(-1), fill_pos])
    vals = jnp.concatenate([jnp.tile(jnp.arange(n_tok, dtype=jnp.int32), TOP_K), j % n_tok])
    _, row_tok = lax.sort_key_val(keys, vals)
    return dest, blk_expert, nused, row_tok


def _final_body(n_ctx_blk, n_r, rowmap_ref, x_ref, *refs):
    r_refs, (gate_ref, nw_ref, oc_ref, ol_ref) = refs[:n_r], refs[n_r:]
    j = pl.program_id(0)
    row = rowmap_ref[j]
    xn = x_ref[...] + gate_ref[pl.ds(row, 1), :] * _branch_update(r_refs)
    ms = jnp.mean(xn * xn, axis=-1, keepdims=True)
    y = xn * lax.rsqrt(ms + RMS_EPS) * nw_ref[...]

    @pl.when(j < n_ctx_blk)
    def _():
        oc_ref[...] = y

    @pl.when(j >= n_ctx_blk)
    def _():
        ol_ref[...] = y


def _final(rowmap, x, r, gate_rows, nw, n_ctx_tok):
    n_tok = x.shape[0]
    n_ctx_blk = n_ctx_tok // DENSE_BLOCK
    tok = lambda w: pl.BlockSpec((DENSE_BLOCK, w), lambda j, rm: (j, 0))
    full = lambda shape: pl.BlockSpec(shape, lambda j, rm: (0,) * len(shape))
    return pl.pallas_call(
        functools.partial(_final_body, n_ctx_blk, len(r)),
        out_shape=[jax.ShapeDtypeStruct((n_ctx_tok, D_MODEL), F32), jax.ShapeDtypeStruct((n_tok - n_ctx_tok, D_MODEL), F32)],
        grid_spec=pltpu.PrefetchScalarGridSpec(
            num_scalar_prefetch=1, grid=(n_tok // DENSE_BLOCK,),
            in_specs=[tok(D_MODEL)] + _row_specs(r) + [full((SUBLANES, D_MODEL)), full((1, D_MODEL))],
            out_specs=[pl.BlockSpec((DENSE_BLOCK, D_MODEL), lambda j, rm: (jnp.minimum(j, n_ctx_blk - 1), 0)),
                       pl.BlockSpec((DENSE_BLOCK, D_MODEL), lambda j, rm: (jnp.maximum(j - n_ctx_blk, 0), 0))]),
        compiler_params=_cparams(1),
        name="final_norm",
    )(rowmap, x, *[a for a, _ in r], gate_rows, nw)


def _grid_pos_embed(rows):
    r = jnp.repeat(jnp.arange(rows, dtype=F32), GRID_W)
    col = jnp.tile(jnp.arange(GRID_W, dtype=F32), rows)
    quarter = D_MODEL // 4
    omega = 1.0 / (10000.0 ** (jnp.arange(quarter, dtype=F32) / quarter))

    def axis_embed(pos):
        ang = pos[:, None] * omega[None, :]
        return jnp.concatenate([jnp.sin(ang), jnp.cos(ang)], axis=-1)

    return jnp.concatenate([axis_embed(r), axis_embed(col)], axis=-1)


def kernel(x_prompt, x_sample, state_s5_re, state_s5_im, state_hg, c, c_ctx, w_ada, b_ada, norm1_w, norm2_w, w_in, s5_lam_re, s5_lam_im, s5_log_step, s5_b_re, s5_b_im, s5_c_re, s5_c_im, s5_d, s5_w_glu, s5_b_glu, hg_lb_logits, hg_norm_w, w_br_s5, w_br_hg, w_out, router_w, router_b, moe_w_gate, moe_w_up, moe_w_down, norm_f_w):
    n_ctx, ctx_len, _ = x_prompt.shape
    n_lat, lat_len, _ = x_sample.shape
    n_ctx_tok = n_ctx * ctx_len
    n_seq = n_ctx + n_lat
    assert n_lat + 1 <= SUBLANES and ctx_len % TOKEN_BLOCK == 0 and lat_len % DENSE_BLOCK == 0
    assert (n_ctx * ctx_len) % DENSE_BLOCK == 0

    seq_np, first_np, last_np, modrow_np = _block_tables(n_ctx, ctx_len, n_lat, lat_len)
    nblk = seq_np.shape[0]
    jmap_f = np.arange(nblk, dtype=np.int32)
    jmap_b = jmap_f[::-1].copy()
    tab_f = tuple(jnp.asarray(t) for t in (jmap_f, seq_np, first_np, last_np))
    tab_b = tuple(jnp.asarray(t) for t in (jmap_b, seq_np[::-1].copy(), last_np[::-1].copy(), first_np[::-1].copy()))
    tab_s5 = tuple(jnp.asarray(t) for t in (seq_np, first_np, last_np))
    rowmap = jnp.asarray(modrow_np.reshape(-1, DENSE_BLOCK // TOKEN_BLOCK)[:, 0])

    cvecs = jnp.zeros((SUBLANES, D_MODEL), F32).at[0].set(c_ctx).at[1:1 + n_lat].set(c)
    mod_all = _adaln(cvecs, w_ada, b_ada)

    probs = jax.nn.softmax(hg_lb_logits.astype(F32), axis=0)
    lower_bounds = jnp.cumsum(probs, axis=0) - probs[0]

    pos = _grid_pos_embed(lat_len // GRID_W)
    x = (x_prompt.reshape(n_ctx_tok, D_MODEL), x_sample.reshape(n_lat * lat_len, D_MODEL))
    n_ctx_blk = n_ctx_tok // DENSE_BLOCK
    n_dense_blk = (n_ctx_tok + n_lat * lat_len) // DENSE_BLOCK
    pos_blocks = lat_len // DENSE_BLOCK
    resid = ((pos, lambda j: jnp.maximum(j - n_ctx_blk, 0) % pos_blocks),)
    gate_rows = jnp.ones((SUBLANES, D_MODEL), F32).at[0].set(0.0)

    router_t = router_w.T.astype(F32)
    r_hi, r_lo = _split_bf16(router_t)
    ns = S5_BLOCK_STATES
    zeros_s5 = jnp.zeros((n_ctx, S5_COLBLOCKS, 2, ns), F32)
    zeros_hg = jnp.zeros((n_ctx, HG_HEADS, HG_DV, HG_DK), F32)

    new_s5_re, new_s5_im, new_hg = [], [], []
    for l in range(DEPTH):
        mod = mod_all[l]
        x, u, q_in, f_raw, i_in, g_out, gate_s5, gate_hg = _inproj(
            rowmap, x, resid, gate_rows, mod, norm1_w[l].reshape(1, D_MODEL), w_in[l].astype(BF16))

        par = [_s5_params(s5_lam_re[l, d], s5_lam_im[l, d], s5_log_step[l, d], s5_b_re[l], s5_b_im[l], s5_c_re[l], s5_c_im[l])
               for d in range(2)]
        h0_s5 = jnp.stack([
            jnp.concatenate([zeros_s5, jnp.stack([state_s5_re[:, l, d], state_s5_im[:, l, d]], axis=1).astype(F32).reshape(
                n_lat, 2, S5_COLBLOCKS, ns).swapaxes(1, 2)], axis=0) for d in range(2)])
        y_f, y_b, fin_f, fin_b = _s5_scan(tab_s5, u, jnp.stack([par[0][0], par[1][0]]), par[0][1],
                                          jnp.stack([par[0][2], par[1][2]]), h0_s5)
        ys = [y_f, y_b]
        s5_fin_re = [fin[:n_ctx, :, 0].reshape(n_ctx, S5_GROUPS, S5_STATE) for fin in (fin_f, fin_b)]
        s5_fin_im = [fin[:n_ctx, :, 1].reshape(n_ctx, S5_GROUPS, S5_STATE) for fin in (fin_f, fin_b)]

        os_, hg_fin = [], []
        for d, tabs in enumerate((tab_f, tab_b)):

            h0 = jnp.concatenate([zeros_hg, jnp.swapaxes(state_hg[:, l, d].astype(F32), -1, -2)], axis=0)
            o_d, hfin = _hg_scan(bool(d), tabs, q_in, f_raw, i_in, lower_bounds[l, d].reshape(1, HG_KW), h0)
            os_.append(o_d)
            hg_fin.append(jnp.swapaxes(hfin[:n_ctx], -1, -2))
        new_s5_re.append(jnp.stack(s5_fin_re, axis=1))
        new_s5_im.append(jnp.stack(s5_fin_im, axis=1))
        new_hg.append(jnp.stack(hg_fin, axis=1))

        x, h2, scores_t = _merge(
            rowmap, x, ys[0], ys[1], u, os_[0], os_[1], g_out, gate_s5, gate_hg, mod,
            s5_d[l].reshape(1, S5_WIDTH), s5_b_glu[l].reshape(1, S5_WIDTH), hg_norm_w[l].reshape(1, HG_WIDTH),
            norm2_w[l].reshape(1, D_MODEL), s5_w_glu[l].astype(BF16), w_br_s5[l].astype(BF16),
            w_br_hg[l].astype(BF16), w_out[l].astype(BF16), r_hi, r_lo)

        eid, wts, rank, counts = _router(scores_t, router_b.astype(F32))
        dest, blk_expert, nused, row_tok = _dispatch_plan(eid, rank, counts[:, 0])
        n_blocks = blk_expert.shape[0]
        cb = n_blocks // MOE_CHUNKS
        yb = None
        for ch in range(MOE_CHUNKS):
            xb = h2.at[row_tok[ch * cb * MOE_ROWS:(ch + 1) * cb * MOE_ROWS]].get(mode='promise_in_bounds')
            yb = _experts(l, ch * cb, n_blocks * MOE_ROWS, blk_expert[ch * cb:(ch + 1) * cb],
                          jnp.clip(nused - ch * cb, 0, cb), xb, moe_w_gate, moe_w_up, moe_w_down, yb)
        rows = yb.at[dest.reshape(-1)].get(mode='promise_in_bounds')
        resid = ((rows, _same_block), (rows, lambda j: j + n_dense_blk), (wts.T, _same_block))
        gate_rows = mod[:, 5 * D_MODEL:6 * D_MODEL]

    y_ctx, y_lat = _final(rowmap, x, resid, gate_rows, norm_f_w.reshape(1, D_MODEL), n_ctx_tok)
    y_prompt = y_ctx.reshape(n_ctx, ctx_len, D_MODEL)
    y_sample = y_lat.reshape(n_lat, lat_len, D_MODEL)
    new_state_s5_re = jnp.stack(new_s5_re, axis=1).astype(state_s5_re.dtype)
    new_state_s5_im = jnp.stack(new_s5_im, axis=1).astype(state_s5_im.dtype)
    new_state_hg = jnp.stack(new_hg, axis=1).astype(state_hg.dtype)
    return (y_prompt, y_sample, new_state_s5_re, new_state_s5_im, new_state_hg)
```
